```python
import math
import jax
import jax.numpy as jnp
from jax import lax
import numpy as np


D_MODEL = 1024
BATCH = 8
SEQ = 4096
DEPTH = 1

D_MIX = D_MODEL
D_FF = 2816
RMS_EPS = 1e-6
GN_EPS = 1e-5

RET_HEADS = 4
RET_DK = D_MODEL // 16
RET_DV = 2 * RET_DK
RET_CHUNK = 128
RET_THETA = 10000.0

NSA_HEADS = 8
NSA_KV_HEADS = 2
NSA_GROUP = NSA_HEADS // NSA_KV_HEADS
NSA_DH = D_MODEL // 16
CMP_LEN = 32
CMP_STRIDE = 16
CMP_HIDDEN = 256
SEL_LEN = 64
SEL_TOP = 16
WINDOW = 512
Q_BLOCK = 64
N_GATES = 3

ROPE_THETA = 500000.0
ROPE_DIM = NSA_DH // 4

NEG_INF = -1e30
FORCE_BONUS = 1e4

IN_WIDTHS = (
    RET_HEADS * RET_DK,
    RET_HEADS * RET_DK,
    RET_HEADS * RET_DV,
    RET_HEADS * RET_DV,
    NSA_HEADS * NSA_DH,
    NSA_KV_HEADS * NSA_DH,
    NSA_KV_HEADS * NSA_DH,
    NSA_KV_HEADS * NSA_DH,
    NSA_KV_HEADS * NSA_DH,
    NSA_KV_HEADS * NSA_DH,
    NSA_KV_HEADS * NSA_DH,
    NSA_HEADS * N_GATES,
)
D_IN = sum(IN_WIDTHS)

kernel_name = 'hybrid_retention_nsa_macaron'


def rms_norm(x, w):
    xf = x.astype(jnp.float32)
    y = xf * lax.rsqrt(jnp.mean(xf * xf, axis=-1, keepdims=True) + RMS_EPS)
    return (y * w.astype(jnp.float32)).astype(x.dtype)


def swiglu(h, w_gate, w_up, w_down):
    return (jax.nn.silu(h @ w_gate) * (h @ w_up)) @ w_down


def partial_rope(x, rot_dim, theta):
    seq = x.shape[1]
    half = rot_dim // 2
    inv_freq = theta ** (-2.0 * jnp.arange(half, dtype=jnp.float32) / rot_dim)
    ang = jnp.arange(seq, dtype=jnp.float32)[:, None] * inv_freq[None, :]
    cos = jnp.cos(ang)[None, :, None, :].astype(x.dtype)
    sin = jnp.sin(ang)[None, :, None, :].astype(x.dtype)
    x1 = x[..., :half]
    x2 = x[..., half:rot_dim]
    return jnp.concatenate([x1 * cos - x2 * sin, x2 * cos + x1 * sin, x[..., rot_dim:]], axis=-1)


def retention_chunkwise(q, k, v):
    bsz, seq, heads, dk = q.shape
    dv = v.shape[-1]
    c = RET_CHUNK
    n_chunks = seq // c
    gamma = 1.0 - 2.0 ** (-5.0 - jnp.arange(heads, dtype=jnp.float32))
    log_g = jnp.log(gamma)
    j = jnp.arange(c, dtype=jnp.float32)
    diff = j[:, None] - j[None, :]
    dmat = jnp.where(diff >= 0, jnp.exp(log_g[:, None, None] * jnp.maximum(diff, 0.0)), 0.0)
    xi = jnp.exp(log_g[:, None] * (j[None, :] + 1.0))
    zeta = jnp.exp(log_g[:, None] * (c - 1.0 - j[None, :]))
    g_chunk = jnp.exp(log_g * c)

    def chunks(a):
        return a.reshape(bsz, n_chunks, c, heads, a.shape[-1]).transpose(1, 0, 3, 2, 4)

    def step(state, inp):
        qi, ki, vi = inp
        inner = jnp.einsum('bhnd,bhmd->bhnm', qi, ki) * dmat
        out = (jnp.einsum('bhnm,bhme->bhne', inner, vi)
               + jnp.einsum('bhnd,bhde->bhne', qi * xi[None, :, :, None], state))
        state = (g_chunk[None, :, None, None] * state
                 + jnp.einsum('bhmd,bhme->bhde', ki * zeta[None, :, :, None], vi))
        return state, out

    state0 = jnp.zeros((bsz, heads, dk, dv), jnp.float32)
    _, out = lax.scan(step, state0, (chunks(q), chunks(k), chunks(v)))
    return out.transpose(1, 0, 3, 2, 4).reshape(bsz, seq, heads, dv)


def group_norm_heads(o, w):
    mu = jnp.mean(o, axis=-1, keepdims=True)
    var = jnp.mean(jnp.square(o - mu), axis=-1, keepdims=True)
    return (o - mu) * lax.rsqrt(var + GN_EPS) * w.astype(jnp.float32).reshape(RET_HEADS, RET_DV)


def compress_blocks(x, pe, w1, b1, w2):
    bsz, seq, hkv, dh = x.shape
    n_cmp = (seq - CMP_LEN) // CMP_STRIDE + 1
    idx = np.arange(n_cmp)[:, None] * CMP_STRIDE + np.arange(CMP_LEN)[None, :]
    blk = x[:, idx] + pe
    blk = blk.transpose(0, 1, 3, 2, 4).reshape(bsz, n_cmp, hkv, CMP_LEN * dh)
    return jax.nn.gelu(blk @ w1 + b1) @ w2


def cmp_to_sel_matrix(n_cmp, n_sel):
    c_start = np.arange(n_cmp) * CMP_STRIDE
    s_start = np.arange(n_sel) * SEL_LEN
    overlap = (np.minimum(c_start[:, None] + CMP_LEN, s_start[None, :] + SEL_LEN)
               - np.maximum(c_start[:, None], s_start[None, :]))
    return jnp.asarray(np.clip(overlap, 0, None) / CMP_LEN, dtype=jnp.float32)


def native_sparse_attention(q, k_c, v_c, k_s, v_s, k_w, v_w, gates,
                            pe_k, w1_k, b1_k, w2_k, pe_v, w1_v, b1_v, w2_v):
    bsz, seq = q.shape[:2]
    dtype = q.dtype
    n_sel = seq // SEL_LEN
    n_top = min(SEL_TOP, n_sel)
    n_qb = seq // Q_BLOCK
    scale = NSA_DH ** -0.5

    k_cmp = compress_blocks(k_c, pe_k, w1_k, b1_k, w2_k)
    v_cmp = compress_blocks(v_c, pe_v, w1_v, b1_v, w2_v)
    n_cmp = k_cmp.shape[1]
    m_cs = cmp_to_sel_matrix(n_cmp, n_sel)
    c_end = jnp.arange(n_cmp) * CMP_STRIDE + CMP_LEN - 1

    q_rot = partial_rope(q, ROPE_DIM, ROPE_THETA)
    k_s = partial_rope(k_s, ROPE_DIM, ROPE_THETA)
    k_w = partial_rope(k_w, ROPE_DIM, ROPE_THETA)

    k_sel = k_s.reshape(bsz, n_sel, SEL_LEN, NSA_KV_HEADS, NSA_DH).transpose(0, 3, 1, 2, 4)
    v_sel = v_s.reshape(bsz, n_sel, SEL_LEN, NSA_KV_HEADS, NSA_DH).transpose(0, 3, 1, 2, 4)
    pad = ((0, 0), (WINDOW, 0), (0, 0), (0, 0))
    k_win = jnp.pad(k_w, pad)
    v_win = jnp.pad(v_w, pad)

    def to_blocks(a):
        return a.reshape((bsz, n_qb, Q_BLOCK) + a.shape[2:]).swapaxes(0, 1)

    q_raw_g = q.reshape(bsz, seq, NSA_KV_HEADS, NSA_GROUP, NSA_DH)
    q_rot_g = q_rot.reshape(bsz, seq, NSA_KV_HEADS, NSA_GROUP, NSA_DH)
    b_idx = jnp.arange(bsz)[:, None, None, None]
    g_idx = jnp.arange(NSA_KV_HEADS)[None, :, None, None]
    blk_ids = jnp.arange(n_sel)
    sel_offsets = jnp.arange(SEL_LEN)

    def block(inp):
        i, qr, qp, g = inp
        t = i * Q_BLOCK + jnp.arange(Q_BLOCK)

        s_c = jnp.einsum('bqgkd,bcgd->bgkqc', qr, k_cmp).astype(jnp.float32) * scale
        mask_c = c_end[None, :] <= t[:, None]
        p_c = jax.nn.softmax(jnp.where(mask_c, s_c, NEG_INF), axis=-1) * mask_c
        o_c = jnp.einsum('bgkqc,bcgd->bqgkd', p_c.astype(dtype), v_cmp)

        imp = jnp.einsum('bgkqc,cs->bgqs', p_c, m_cs)
        cur = t // SEL_LEN
        valid = blk_ids[None, :] <= cur[:, None]
        forced = ((blk_ids[None, :] == 0) | (blk_ids[None, :] == cur[:, None])
                  | (blk_ids[None, :] == cur[:, None] - 1))
        imp = jnp.where(valid, imp + jnp.where(forced, FORCE_BONUS, 0.0), NEG_INF)
        _, sel = lax.top_k(imp, n_top)

        k_g = k_sel[b_idx, g_idx, sel]
        v_g = v_sel[b_idx, g_idx, sel]
        s_s = jnp.einsum('bqgkd,bgqnld->bgkqnl', qp, k_g).astype(jnp.float32) * scale
        kpos = sel[..., None] * SEL_LEN + sel_offsets
        mask_s = (kpos <= t[None, None, :, None, None])[:, :, None]
        s_s = jnp.where(mask_s, s_s, NEG_INF).reshape(bsz, NSA_KV_HEADS, NSA_GROUP, Q_BLOCK, n_top * SEL_LEN)
        p_s = jax.nn.softmax(s_s, axis=-1).reshape(bsz, NSA_KV_HEADS, NSA_GROUP, Q_BLOCK, n_top, SEL_LEN)
        o_s = jnp.einsum('bgkqnl,bgqnld->bqgkd', p_s.astype(dtype), v_g)

        kw_blk = lax.dynamic_slice_in_dim(k_win, i * Q_BLOCK, Q_BLOCK + WINDOW, axis=1)
        vw_blk = lax.dynamic_slice_in_dim(v_win, i * Q_BLOCK, Q_BLOCK + WINDOW, axis=1)
        wpos = i * Q_BLOCK - WINDOW + jnp.arange(Q_BLOCK + WINDOW)
        mask_w = ((wpos[None, :] <= t[:, None]) & (wpos[None, :] > t[:, None] - WINDOW)
                  & (wpos[None, :] >= 0))
        s_w = jnp.einsum('bqgkd,bjgd->bgkqj', qp, kw_blk).astype(jnp.float32) * scale
        p_w = jax.nn.softmax(jnp.where(mask_w, s_w, NEG_INF), axis=-1)
        o_w = jnp.einsum('bgkqj,bjgd->bqgkd', p_w.astype(dtype), vw_blk)

        return g[..., 0:1] * o_c + g[..., 1:2] * o_s + g[..., 2:3] * o_w

    out = lax.map(block, (jnp.arange(n_qb), to_blocks(q_raw_g), to_blocks(q_rot_g), to_blocks(gates)))
    return out.swapaxes(0, 1).reshape(bsz, seq, NSA_HEADS * NSA_DH)


def setup_inputs(seed: int = 0) -> dict:
    key = jax.random.key(seed)
    ks = jax.random.split(key, 24)
    f32 = jnp.float32

    def nrm(k, shape, scale):
        return jax.random.normal(k, shape, f32) * scale

    def gain(k, n):
        return 1.0 + 0.1 * jax.random.normal(k, (DEPTH, n), f32)

    flat = CMP_LEN * NSA_DH
    return {
        'x': nrm(ks[0], (BATCH, SEQ, D_MODEL), 1.0),
        'ffn1_norm_w': gain(ks[1], D_MODEL),
        'ffn1_w_gate': nrm(ks[2], (DEPTH, D_MODEL, D_FF), D_MODEL ** -0.5),
        'ffn1_w_up': nrm(ks[3], (DEPTH, D_MODEL, D_FF), D_MODEL ** -0.5),
        'ffn1_w_down': nrm(ks[4], (DEPTH, D_FF, D_MODEL), D_FF ** -0.5),
        'mix_norm_w': gain(ks[5], D_MODEL),
        'w_in': nrm(ks[6], (DEPTH, D_MODEL, D_IN), D_MODEL ** -0.5),
        'ret_norm_w': gain(ks[7], RET_HEADS * RET_DV),
        'cmp_pe_k': nrm(ks[8], (DEPTH, CMP_LEN, NSA_KV_HEADS, NSA_DH), 0.02),
        'cmp_k_w1': nrm(ks[9], (DEPTH, flat, CMP_HIDDEN), flat ** -0.5),
        'cmp_k_b1': nrm(ks[10], (DEPTH, CMP_HIDDEN), 0.01),
        'cmp_k_w2': nrm(ks[11], (DEPTH, CMP_HIDDEN, NSA_DH), CMP_HIDDEN ** -0.5),
        'cmp_pe_v': nrm(ks[12], (DEPTH, CMP_LEN, NSA_KV_HEADS, NSA_DH), 0.02),
        'cmp_v_w1': nrm(ks[13], (DEPTH, flat, CMP_HIDDEN), flat ** -0.5),
        'cmp_v_b1': nrm(ks[14], (DEPTH, CMP_HIDDEN), 0.01),
        'cmp_v_w2': nrm(ks[15], (DEPTH, CMP_HIDDEN, NSA_DH), CMP_HIDDEN ** -0.5),
        'w_out': nrm(ks[16], (DEPTH, D_MIX, D_MODEL), D_MIX ** -0.5),
        'ffn2_norm_w': gain(ks[17], D_MODEL),
        'ffn2_w_gate': nrm(ks[18], (DEPTH, D_MODEL, D_FF), D_MODEL ** -0.5),
        'ffn2_w_up': nrm(ks[19], (DEPTH, D_MODEL, D_FF), D_MODEL ** -0.5),
        'ffn2_w_down': nrm(ks[20], (DEPTH, D_FF, D_MODEL), D_FF ** -0.5),
        'final_norm_w': 1.0 + 0.1 * jax.random.normal(ks[21], (D_MODEL,), f32),
    }


def reference(x, ffn1_norm_w, ffn1_w_gate, ffn1_w_up, ffn1_w_down, mix_norm_w, w_in,
              ret_norm_w, cmp_pe_k, cmp_k_w1, cmp_k_b1, cmp_k_w2, cmp_pe_v, cmp_v_w1,
              cmp_v_b1, cmp_v_w2, w_out, ffn2_norm_w, ffn2_w_gate, ffn2_w_up, ffn2_w_down,
              final_norm_w):
    bsz, seq, _ = x.shape
    dtype = x.dtype
    offsets = tuple(int(o) for o in np.cumsum(IN_WIDTHS)[:-1])
    h = x
    for layer in range(DEPTH):
        h = h + 0.5 * swiglu(rms_norm(h, ffn1_norm_w[layer]), ffn1_w_gate[layer],
                             ffn1_w_up[layer], ffn1_w_down[layer])

        u = rms_norm(h, mix_norm_w[layer])
        proj = u @ w_in[layer]
        (rq, rk, rv, rg, nq, kc, vc, ksel, vsel, kwin, vwin, ng) = jnp.split(proj, offsets, axis=-1)

        rq = partial_rope(rq.reshape(bsz, seq, RET_HEADS, RET_DK), RET_DK, RET_THETA)
        rk = partial_rope(rk.reshape(bsz, seq, RET_HEADS, RET_DK), RET_DK, RET_THETA) * (RET_DK ** -0.5)
        rv = rv.reshape(bsz, seq, RET_HEADS, RET_DV)
        ret = retention_chunkwise(rq.astype(jnp.float32), rk.astype(jnp.float32), rv.astype(jnp.float32))
        ret = group_norm_heads(ret, ret_norm_w[layer]).astype(dtype).reshape(bsz, seq, RET_HEADS * RET_DV)
        ret = ret * jax.nn.silu(rg)

        kv_shape = (bsz, seq, NSA_KV_HEADS, NSA_DH)
        gates = jax.nn.sigmoid(ng).reshape(bsz, seq, NSA_KV_HEADS, NSA_GROUP, N_GATES)
        nsa = native_sparse_attention(
            nq.reshape(bsz, seq, NSA_HEADS, NSA_DH),
            kc.reshape(kv_shape), vc.reshape(kv_shape),
            ksel.reshape(kv_shape), vsel.reshape(kv_shape),
            kwin.reshape(kv_shape), vwin.reshape(kv_shape), gates,
            cmp_pe_k[layer], cmp_k_w1[layer], cmp_k_b1[layer], cmp_k_w2[layer],
            cmp_pe_v[layer], cmp_v_w1[layer], cmp_v_b1[layer], cmp_v_w2[layer])

        h = h + jnp.concatenate([ret, nsa], axis=-1) @ w_out[layer]

        h = h + 0.5 * swiglu(rms_norm(h, ffn2_norm_w[layer]), ffn2_w_gate[layer],
                             ffn2_w_up[layer], ffn2_w_down[layer])
    return rms_norm(h, final_norm_w)
```

```python
import functools

import numpy as np
import jax
import jax.numpy as jnp
from jax import lax
from jax.experimental import pallas as pl
from jax.experimental.pallas import tpu as pltpu

F32 = jnp.float32
BF16 = jnp.bfloat16

D_MODEL = 1024
D_FF = 2816
RMS_EPS = 1e-6
GN_EPS = 1e-5

RET_HEADS = 4
RET_DK = 64
RET_DV = 128
RET_THETA = 10000.0

NSA_HEADS = 8
NSA_KV_HEADS = 2
NSA_GROUP = 4
NSA_DH = 64
CMP_LEN = 32
CMP_STRIDE = 16
CMP_HIDDEN = 256
SEL_LEN = 64
SEL_TOP = 16
WINDOW = 512
N_GATES = 3
ROPE_THETA = 500000.0
ROPE_DIM = 16

NEG_INF = -1e30
FORCE_BONUS = 1e4

LANE = 128
VMEM_LIMIT = 56 * 1024 * 1024

TM = 512
RET_CHUNK = 128
TQ = 256
TK = 256
FF_SPLIT = (1280, 1536)

SEG = dict(rq=(0, 256), rk=(256, 512), rv=(512, 1024), rg=(1024, 1536), nq=(1536, 2048),
           kc=(2048, 2176), vc=(2176, 2304), ks=(2304, 2432), vs=(2432, 2560),
           kw=(2560, 2688), vw=(2688, 2816), ng=(2816, 3072))
D_CAT = 3072


def _rms(x, w):
    return x * lax.rsqrt(jnp.mean(x * x, axis=-1, keepdims=True) + RMS_EPS) * w


def _params(sem):
    return pltpu.CompilerParams(dimension_semantics=sem, vmem_limit_bytes=VMEM_LIMIT)


def _const_spec(shape):
    nd = len(shape)
    return pl.BlockSpec(shape, lambda *_: (0,) * nd, pipeline_mode=pl.Buffered(1))


def _swiglu_half_step(h, nw, wg_ref, wu_ref, wd_ref):
    hn = _rms(h, nw).astype(BF16)
    acc = None
    off = 0
    for width in FF_SPLIT:
        g = jnp.dot(hn, wg_ref[:, off:off + width], preferred_element_type=F32)
        u = jnp.dot(hn, wu_ref[:, off:off + width], preferred_element_type=F32)
        a = (g * jax.nn.sigmoid(g) * u).astype(BF16)
        y = jnp.dot(a, wd_ref[off:off + width, :], preferred_element_type=F32)
        acc = y if acc is None else acc + y
        off += width
    return h + 0.5 * acc


def _ffn1_body(x_ref, nw_ref, wg_ref, wu_ref, wd_ref, o_ref):
    o_ref[...] = _swiglu_half_step(x_ref[...], nw_ref[...], wg_ref, wu_ref, wd_ref)


def _ffn1(x2, nw, wg, wu, wd):
    t = x2.shape[0]
    tok = pl.BlockSpec((TM, D_MODEL), lambda i: (i, 0))
    return pl.pallas_call(
        _ffn1_body,
        grid=(t // TM,),
        in_specs=[tok, _const_spec((1, D_MODEL)), _const_spec((D_MODEL, D_FF)),
                  _const_spec((D_MODEL, D_FF)), _const_spec((D_FF, D_MODEL))],
        out_specs=tok,
        out_shape=jax.ShapeDtypeStruct((t, D_MODEL), F32),
        compiler_params=_params(("parallel",)),
        name="ffn1",
    )(x2, nw, wg, wu, wd)


def _ffn2_body(h_ref, ret_ref, nsa_ref, wo_ref, nw_ref, wg_ref, wu_ref, wd_ref, fw_ref, o_ref):
    half = RET_HEADS * RET_DV
    mix = (jnp.dot(ret_ref[...], wo_ref[:half, :], preferred_element_type=F32)
           + jnp.dot(nsa_ref[...], wo_ref[half:, :], preferred_element_type=F32))
    h = h_ref[...] + mix
    h = _swiglu_half_step(h, nw_ref[...], wg_ref, wu_ref, wd_ref)
    o_ref[...] = _rms(h, fw_ref[...])


def _ffn2(h1, ret, nsa, wo, nw, wg, wu, wd, fw):
    t = h1.shape[0]
    tok = pl.BlockSpec((TM, D_MODEL), lambda i: (i, 0))
    half = pl.BlockSpec((TM, D_MODEL // 2), lambda i: (i, 0))
    return pl.pallas_call(
        _ffn2_body,
        grid=(t // TM,),
        in_specs=[tok, half, half, _const_spec((D_MODEL, D_MODEL)), _const_spec((1, D_MODEL)),
                  _const_spec((D_MODEL, D_FF)), _const_spec((D_MODEL, D_FF)),
                  _const_spec((D_FF, D_MODEL)), _const_spec((1, D_MODEL))],
        out_specs=tok,
        out_shape=jax.ShapeDtypeStruct((t, D_MODEL), F32),
        compiler_params=_params(("parallel",)),
        name="ffn2",
    )(h1, ret, nsa, wo, nw, wg, wu, wd, fw)


def _rope128(x, cos, sin_a, sin_b, half):
    return (x * cos + pltpu.roll(x, LANE - half, 1) * sin_a + pltpu.roll(x, half, 1) * sin_b)


def _proj_body(h_ref, nw_ref, w_ref, rc_ref, ra_ref, rb_ref, nc_ref, na_ref, nb_ref,
               rq_ref, rk_ref, rv_ref, rg_ref, nqr_ref, nqp_ref, kc_ref, vc_ref,
               ks_ref, vs_ref, kw_ref, vw_ref, ng_ref):
    u = _rms(h_ref[...], nw_ref[...]).astype(BF16)

    def seg(name):
        a, b = SEG[name]
        return jnp.dot(u, w_ref[:, a:b], preferred_element_type=F32)

    rc, ra, rb = rc_ref[...], ra_ref[...], rb_ref[...]
    nc, na, nb = nc_ref[...], na_ref[...], nb_ref[...]
    ret_half = RET_DK // 2
    nsa_half = ROPE_DIM // 2
    nsa_scale = NSA_DH ** -0.5

    p = seg("rq")
    for j in range(2):
        sl = slice(j * LANE, (j + 1) * LANE)
        rq_ref[:, sl] = _rope128(p[:, sl], rc, ra, rb, ret_half)
    p = seg("rk")
    for j in range(2):
        sl = slice(j * LANE, (j + 1) * LANE)
        rk_ref[:, sl] = _rope128(p[:, sl], rc, ra, rb, ret_half) * (RET_DK ** -0.5)
    rv_ref[...] = seg("rv").astype(BF16)
    rg_ref[...] = seg("rg")
    p = seg("nq")
    nqr_ref[...] = (p * nsa_scale).astype(BF16)
    for j in range(4):
        sl = slice(j * LANE, (j + 1) * LANE)
        nqp_ref[:, sl] = (_rope128(p[:, sl], nc, na, nb, nsa_half) * nsa_scale).astype(BF16)
    kc_ref[...] = seg("kc")
    vc_ref[...] = seg("vc")
    ks_ref[...] = _rope128(seg("ks"), nc, na, nb, nsa_half).astype(BF16)
    vs_ref[...] = seg("vs").astype(BF16)
    kw_ref[...] = _rope128(seg("kw"), nc, na, nb, nsa_half).astype(BF16)
    vw_ref[...] = seg("vw").astype(BF16)
    ng_ref[...] = seg("ng")


def _proj(h1, nw, wcat, tabs, seq):
    t = h1.shape[0]
    per_seq = seq // TM

    def tok(width):
        return pl.BlockSpec((TM, width), lambda i: (i, 0))

    tab = pl.BlockSpec((TM, LANE), lambda i: (i % per_seq, 0))
    widths = [256, 256, 512, 512, 512, 512, 128, 128, 128, 128, 128, 128, 256]
    dtypes = [F32, F32, BF16, F32, BF16, BF16, F32, F32, BF16, BF16, BF16, BF16, F32]
    return pl.pallas_call(
        _proj_body,
        grid=(t // TM,),
        in_specs=[tok(D_MODEL), _const_spec((1, D_MODEL)), _const_spec((D_MODEL, D_CAT))] + [tab] * 6,
        out_specs=[tok(w) for w in widths],
        out_shape=[jax.ShapeDtypeStruct((t, w), d) for w, d in zip(widths, dtypes)],
        compiler_params=_params(("parallel",)),
        name="proj",
    )(h1, nw, wcat, *tabs)


def _ret_body(q_ref, k_ref, v_ref, g_ref, dm_ref, xi_ref, ze_ref, gc_ref, nw_ref, o_ref, st_ref, *, seq):
    c = RET_CHUNK
    st_ref[...] = jnp.zeros_like(st_ref)

    def chunk(n, carry):
        rows = pl.ds(pl.multiple_of(n * c, c), c)
        for h in range(RET_HEADS):
            ks = slice(h * RET_DK, (h + 1) * RET_DK)
            vs = slice(h * RET_DV, (h + 1) * RET_DV)
            q = q_ref[rows, ks]
            k = k_ref[rows, ks]
            v = v_ref[rows, vs]
            state = st_ref[h]
            inner = lax.dot_general(q.astype(BF16), k.astype(BF16), (((1,), (1,)), ((), ())),
                                    preferred_element_type=F32) * dm_ref[h]
            out = (jnp.dot(inner.astype(BF16), v, preferred_element_type=F32)
                   + jnp.dot((q * xi_ref[h]).astype(BF16), state.astype(BF16),
                             preferred_element_type=F32))
            kz = (k * ze_ref[h]).astype(BF16)
            st_ref[h] = gc_ref[h] * state + lax.dot_general(
                kz, v, (((0,), (0,)), ((), ())), preferred_element_type=F32)
            mu = jnp.mean(out, axis=-1, keepdims=True)
            d = out - mu
            var = jnp.mean(d * d, axis=-1, keepdims=True)
            y = d * lax.rsqrt(var + GN_EPS) * nw_ref[:, vs]
            gate = g_ref[rows, vs]
            o_ref[rows, vs] = (y * (gate * jax.nn.sigmoid(gate))).astype(BF16)
        return carry

    lax.fori_loop(0, seq // c, chunk, 0)


def _retention(rq, rk, rv, rg, tabs, nw, bsz, seq):
    t = bsz * seq
    c = RET_CHUNK

    def seq_blk(width):
        return pl.BlockSpec((seq, width), lambda b: (b, 0))

    return pl.pallas_call(
        functools.partial(_ret_body, seq=seq),
        grid=(bsz,),
        in_specs=[seq_blk(256), seq_blk(256), seq_blk(512), seq_blk(512),
                  _const_spec((RET_HEADS, c, c)), _const_spec((RET_HEADS, c, 1)),
                  _const_spec((RET_HEADS, c, 1)), _const_spec((RET_HEADS, 1, 1)),
                  _const_spec((1, RET_HEADS * RET_DV))],
        out_specs=seq_blk(512),
        out_shape=jax.ShapeDtypeStruct((t, RET_HEADS * RET_DV), BF16),
        scratch_shapes=[pltpu.VMEM((RET_HEADS, RET_DK, RET_DV), F32)],
        compiler_params=_params(("parallel",)),
        name="retention",
    )(rq, rk, rv, rg, *tabs, nw)


def _gelu_tanh(x):
    return 0.5 * x * (1.0 + jnp.tanh(np.sqrt(2.0 / np.pi).astype(np.float32) * (x + 0.044715 * (x * x * x))))


def _cmp_body(kc_ref, vc_ref, pek_ref, w1k_ref, b1k_ref, w2k_ref, pev_ref, w1v_ref, b1v_ref, w2v_ref,
              ko_ref, vo_ref, *, seq):
    n_grp = seq // CMP_STRIDE

    def one(x_ref, pe_ref, w1_ref, b1_ref, w2_ref, o_ref):
        first = None
        second = None
        for l in range(CMP_STRIDE):
            x = x_ref[pl.ds(l, n_grp, stride=CMP_STRIDE), :]
            a = jnp.dot((x + pe_ref[l:l + 1, :]).astype(BF16), w1_ref[l], preferred_element_type=F32)
            b = jnp.dot((x + pe_ref[CMP_STRIDE + l:CMP_STRIDE + l + 1, :]).astype(BF16),
                        w1_ref[CMP_STRIDE + l], preferred_element_type=F32)
            first = a if first is None else first + a
            second = b if second is None else second + b
        pre = first + pltpu.roll(second, n_grp - 1, 0) + b1_ref[...]
        hid = _gelu_tanh(pre).astype(BF16)
        res = jnp.dot(hid, w2_ref[...], preferred_element_type=F32)
        row = lax.broadcasted_iota(jnp.int32, res.shape, 0)
        res = jnp.where(row < n_grp - 1, res, 0.0).astype(BF16)
        for g in range(NSA_KV_HEADS):
            o_ref[g] = res[:, g * NSA_DH:(g + 1) * NSA_DH]

    one(kc_ref, pek_ref, w1k_ref, b1k_ref, w2k_ref, ko_ref)
    one(vc_ref, pev_ref, w1v_ref, b1v_ref, w2v_ref, vo_ref)


def _compress(kc, vc, wk, wv, bsz, seq):
    n_grp = seq // CMP_STRIDE
    blk = pl.BlockSpec((seq, LANE), lambda b: (b, 0))
    wspecs = [_const_spec((CMP_LEN, LANE)), _const_spec((CMP_LEN, LANE, 2 * CMP_HIDDEN)),
              _const_spec((1, 2 * CMP_HIDDEN)), _const_spec((2 * CMP_HIDDEN, LANE))]
    out = pl.BlockSpec((None, NSA_KV_HEADS, n_grp, NSA_DH), lambda b: (b, 0, 0, 0))
    shape = jax.ShapeDtypeStruct((bsz, NSA_KV_HEADS, n_grp, NSA_DH), BF16)
    return pl.pallas_call(
        functools.partial(_cmp_body, seq=seq),
        grid=(bsz,),
        in_specs=[blk, blk] + wspecs + wspecs,
        out_specs=[out, out],
        out_shape=[shape, shape],
        compiler_params=_params(("parallel",)),
        name="compress",
    )(kc, vc, *wk, *wv)


def _cmpattn_body(q_ref, kc_ref, vc_ref, mcs_ref, ng_ref, oc_ref, sb_ref, *, n_top):
    i = pl.program_id(2)
    kc = kc_ref[...]
    vc = vc_ref[...]
    n_grp = kc.shape[0]
    n_sel = mcs_ref.shape[0]
    t0 = i * TQ
    cidx = lax.broadcasted_iota(jnp.int32, (n_grp, TQ), 0)
    tpos = t0 + lax.broadcasted_iota(jnp.int32, (n_grp, TQ), 1)
    mask = (cidx * CMP_STRIDE + (CMP_LEN - 1)) <= tpos
    gates = jax.nn.sigmoid(ng_ref[...])

    psum = jnp.zeros((n_grp, TQ), F32)
    for k in range(NSA_GROUP):
        q = q_ref[:, k * NSA_DH:(k + 1) * NSA_DH]
        s = lax.dot_general(kc, q, (((1,), (1,)), ((), ())), preferred_element_type=F32)
        s = jnp.where(mask, s, NEG_INF)
        m = jnp.max(s, axis=0, keepdims=True)
        e = jnp.where(mask, jnp.exp(s - m), 0.0)
        l = jnp.sum(e, axis=0, keepdims=True)
        p = e / jnp.where(l > 0.0, l, 1.0)
        psum = psum + p
        o = lax.dot_general(p.astype(BF16), vc, (((0,), (0,)), ((), ())), preferred_element_type=F32)
        oc_ref[:, k * NSA_DH:(k + 1) * NSA_DH] = o * gates[:, k * N_GATES:k * N_GATES + 1]

    hi = psum.astype(BF16)
    lo = (psum - hi.astype(F32)).astype(BF16)
    mcs = mcs_ref[...]
    imp = (jnp.dot(mcs, hi, preferred_element_type=F32) + jnp.dot(mcs, lo, preferred_element_type=F32))
    blk = lax.broadcasted_iota(jnp.int32, (n_sel, TQ), 0)
    cur = jnp.right_shift(t0 + lax.broadcasted_iota(jnp.int32, (n_sel, TQ), 1), 6)
    valid = blk <= cur
    forced = (blk == 0) | (blk == cur) | (blk == cur - 1)
    imp = jnp.where(valid, imp + jnp.where(forced, FORCE_BONUS, 0.0), NEG_INF)
    cnt = jnp.zeros((n_sel, TQ), jnp.int32)
    for kk in range(n_sel):
        row = imp[kk:kk + 1, :]
        beats = (row > imp) | ((row == imp) & (blk > kk))
        cnt = cnt + beats.astype(jnp.int32)
    bias = jnp.where(cnt < n_top, 0.0, NEG_INF)
    sb_ref[...] = bias.T.astype(BF16)


def _cmp_attention(nq_raw, kcmp, vcmp, mcs_t, ng, bsz, seq):
    t = bsz * seq
    nq = seq // TQ
    n_grp = seq // CMP_STRIDE
    n_sel = seq // SEL_LEN
    n_top = min(SEL_TOP, n_sel)
    half = NSA_GROUP * NSA_DH
    return pl.pallas_call(
        functools.partial(_cmpattn_body, n_top=n_top),
        grid=(bsz, NSA_KV_HEADS, nq),
        in_specs=[pl.BlockSpec((TQ, half), lambda b, g, i: (b * nq + i, g)),
                  pl.BlockSpec((None, None, n_grp, NSA_DH), lambda b, g, i: (b, g, 0, 0)),
                  pl.BlockSpec((None, None, n_grp, NSA_DH), lambda b, g, i: (b, g, 0, 0)),
                  _const_spec((n_sel, n_grp)),
                  pl.BlockSpec((TQ, LANE), lambda b, g, i: (b * nq + i, g))],
        out_specs=[pl.BlockSpec((TQ, half), lambda b, g, i: (b * nq + i, g)),
                   pl.BlockSpec((None, None, TQ, n_sel), lambda b, g, i: (b, g, i, 0))],
        out_shape=[jax.ShapeDtypeStruct((t, NSA_HEADS * NSA_DH), F32),
                   jax.ShapeDtypeStruct((bsz, NSA_KV_HEADS, seq, n_sel), BF16)],
        compiler_params=_params(("parallel", "parallel", "parallel")),
        name="cmp_attn",
    )(nq_raw, kcmp, vcmp, mcs_t, ng)


def _selwin_body(q_ref, ks_ref, vs_ref, kw_ref, vw_ref, sb_ref, ex_ref, oc_ref, ng_ref, o_ref,
                 m_ref, l_ref, acc_ref, res_ref):
    i = pl.program_id(1)
    t0 = i * TQ
    tq_pos = t0 + lax.broadcasted_iota(jnp.int32, (TQ, TK), 0)
    k_off = lax.broadcasted_iota(jnp.int32, (TQ, TK), 1)
    gates = jax.nn.sigmoid(ng_ref[...])

    def reset():
        m_ref[...] = jnp.full_like(m_ref, NEG_INF)
        l_ref[...] = jnp.zeros_like(l_ref)
        acc_ref[...] = jnp.zeros_like(acc_ref)

    def flash_step(g, k_ref, v_ref, j, extra):
        rows = pl.ds(pl.multiple_of(j * TK, TK), TK)
        lanes = slice(g * NSA_DH, (g + 1) * NSA_DH)
        k = k_ref[rows, lanes]
        v = v_ref[rows, lanes]
        for kq in range(NSA_GROUP):
            hh = g * NSA_GROUP + kq
            q = q_ref[:, hh * NSA_DH:(hh + 1) * NSA_DH]
            s = lax.dot_general(q, k, (((1,), (1,)), ((), ())), preferred_element_type=F32)
            s = extra(s)
            m_old = m_ref[kq]
            m_new = jnp.maximum(m_old, jnp.max(s, axis=-1, keepdims=True))
            alpha = jnp.exp(m_old - m_new)
            p = jnp.exp(s - m_new)
            l_ref[kq] = alpha * l_ref[kq] + jnp.sum(p, axis=-1, keepdims=True)
            acc_ref[kq] = alpha * acc_ref[kq] + jnp.dot(p.astype(BF16), v, preferred_element_type=F32)
            m_ref[kq] = m_new

    for g in range(NSA_KV_HEADS):
        reset()
        sel_bias = sb_ref[g]

        def sel_step(j, carry, g=g, sel_bias=sel_bias):
            def extra(s):
                bias = jnp.dot(sel_bias, ex_ref[j], preferred_element_type=F32)
                return jnp.where(j * TK + k_off <= tq_pos, s + bias, NEG_INF)
            flash_step(g, ks_ref, vs_ref, j, extra)
            return carry

        lax.fori_loop(0, i + 1, sel_step, 0)
        for kq in range(NSA_GROUP):
            hh = g * NSA_GROUP + kq
            col = g * LANE + kq * N_GATES + 1
            res_ref[:, hh * NSA_DH:(hh + 1) * NSA_DH] = (
                oc_ref[:, hh * NSA_DH:(hh + 1) * NSA_DH]
                + (acc_ref[kq] / l_ref[kq]) * gates[:, col:col + 1])

        reset()

        def win_step(j, carry, g=g):
            def extra(s):
                kpos = j * TK + k_off
                return jnp.where((kpos <= tq_pos) & (kpos > tq_pos - WINDOW), s, NEG_INF)
            flash_step(g, kw_ref, vw_ref, j, extra)
            return carry

        lax.fori_loop(jnp.maximum(i - WINDOW // TK, 0), i + 1, win_step, 0)
        for kq in range(NSA_GROUP):
            hh = g * NSA_GROUP + kq
            col = g * LANE + kq * N_GATES + 2
            res_ref[:, hh * NSA_DH:(hh + 1) * NSA_DH] += (acc_ref[kq] / l_ref[kq]) * gates[:, col:col + 1]

    o_ref[...] = res_ref[...].astype(BF16)


def _sel_win(nq_rot, ks, vs, kw, vw, selbias, expand, ocg, ng, bsz, seq):
    t = bsz * seq
    nq = seq // TQ
    n_sel = seq // SEL_LEN
    width = NSA_HEADS * NSA_DH
    tokq = pl.BlockSpec((TQ, width), lambda b, i: (b * nq + i, 0))
    kv = pl.BlockSpec((seq, LANE), lambda b, i: (b, 0))
    return pl.pallas_call(
        _selwin_body,
        grid=(bsz, nq),
        in_specs=[tokq, kv, kv, kv, kv,
                  pl.BlockSpec((None, NSA_KV_HEADS, TQ, n_sel), lambda b, i: (b, 0, i, 0)),
                  _const_spec((seq // TK, n_sel, TK)),
                  tokq,
                  pl.BlockSpec((TQ, 2 * LANE), lambda b, i: (b * nq + i, 0))],
        out_specs=tokq,
        out_shape=jax.ShapeDtypeStruct((t, width), BF16),
        scratch_shapes=[pltpu.VMEM((NSA_GROUP, TQ, 1), F32), pltpu.VMEM((NSA_GROUP, TQ, 1), F32),
                        pltpu.VMEM((NSA_GROUP, TQ, NSA_DH), F32), pltpu.VMEM((TQ, width), F32)],
        compiler_params=_params(("parallel", "parallel")),
        name="sel_win",
    )(nq_rot, ks, vs, kw, vw, selbias, expand, ocg, ng)


def _rope_tables(seq, rot_dim, theta, head_dim):
    half = rot_dim // 2
    inv_freq = theta ** (-2.0 * jnp.arange(half, dtype=F32) / rot_dim)
    ang = jnp.arange(seq, dtype=F32)[:, None] * inv_freq[None, :]
    cos, sin = jnp.cos(ang), jnp.sin(ang)
    rest = head_dim - rot_dim
    ones = jnp.ones((seq, rest), F32)
    zr = jnp.zeros((seq, rest), F32)
    zh = jnp.zeros((seq, half), F32)
    cos_t = jnp.concatenate([cos, cos, ones], axis=1)
    sin_a = jnp.concatenate([-sin, zh, zr], axis=1)
    sin_b = jnp.concatenate([zh, sin, zr], axis=1)
    rep = LANE // head_dim
    return [jnp.tile(a, (1, rep)) for a in (cos_t, sin_a, sin_b)]


def _retention_tables():
    c = RET_CHUNK
    gamma = 1.0 - 2.0 ** (-5.0 - jnp.arange(RET_HEADS, dtype=F32))
    log_g = jnp.log(gamma)
    j = jnp.arange(c, dtype=F32)
    diff = j[:, None] - j[None, :]
    dmat = jnp.where(diff >= 0, jnp.exp(log_g[:, None, None] * jnp.maximum(diff, 0.0)), 0.0)
    xi = jnp.exp(log_g[:, None] * (j[None, :] + 1.0))[:, :, None]
    zeta = jnp.exp(log_g[:, None] * (c - 1.0 - j[None, :]))[:, :, None]
    g_chunk = jnp.exp(log_g * c)[:, None, None]
    return [dmat, xi, zeta, g_chunk]


def _cmp_to_sel_t(seq):
    n_grp = seq // CMP_STRIDE
    n_sel = seq // SEL_LEN
    c_start = np.arange(n_grp) * CMP_STRIDE
    s_start = np.arange(n_sel) * SEL_LEN
    overlap = (np.minimum(c_start[None, :] + CMP_LEN, s_start[:, None] + SEL_LEN)
               - np.maximum(c_start[None, :], s_start[:, None]))
    m = np.clip(overlap, 0, None) / CMP_LEN
    m[:, n_grp - 1] = 0.0
    return jnp.asarray(m, dtype=BF16)


def _expand_table(seq):
    n_sel = seq // SEL_LEN
    key_blk = np.arange(seq) // SEL_LEN
    e = (np.arange(n_sel)[:, None] == key_blk[None, :]).astype(np.float32)
    e = e.reshape(n_sel, seq // TK, TK).transpose(1, 0, 2)
    return jnp.asarray(e, dtype=BF16)


def _pack_w_in(w_in):
    gate_off = 1536 + 512 + 6 * 128
    per_group = NSA_GROUP * N_GATES
    pad = jnp.zeros((D_MODEL, LANE - per_group), w_in.dtype)
    cols = [w_in[:, :gate_off]]
    for g in range(NSA_KV_HEADS):
        cols += [w_in[:, gate_off + g * per_group: gate_off + (g + 1) * per_group], pad]
    return jnp.concatenate(cols, axis=1).astype(BF16)


def _pack_cmp(pe, w1, b1, w2):
    pe2 = pe.reshape(CMP_LEN, NSA_KV_HEADS * NSA_DH)
    w1l = w1.reshape(CMP_LEN, NSA_DH, CMP_HIDDEN)
    z1 = jnp.zeros_like(w1l)
    w1bd = jnp.concatenate([jnp.concatenate([w1l, z1], axis=2),
                            jnp.concatenate([z1, w1l], axis=2)], axis=1).astype(BF16)
    b1bd = jnp.concatenate([b1, b1])[None, :]
    z2 = jnp.zeros_like(w2)
    w2bd = jnp.concatenate([jnp.concatenate([w2, z2], axis=1),
                            jnp.concatenate([z2, w2], axis=1)], axis=0).astype(BF16)
    return [pe2, w1bd, b1bd, w2bd]


def kernel(x, ffn1_norm_w, ffn1_w_gate, ffn1_w_up, ffn1_w_down, mix_norm_w, w_in, ret_norm_w, cmp_pe_k, cmp_k_w1, cmp_k_b1, cmp_k_w2, cmp_pe_v, cmp_v_w1, cmp_v_b1, cmp_v_w2, w_out, ffn2_norm_w, ffn2_w_gate, ffn2_w_up, ffn2_w_down, final_norm_w):
    bsz, seq, d = x.shape
    assert d == D_MODEL and seq % TM == 0 and seq % TQ == 0 and seq >= WINDOW
    t = bsz * seq
    h = x.reshape(t, d)
    n_layers = ffn1_norm_w.shape[0]
    ret_tabs = _retention_tables()
    ret_rope = _rope_tables(seq, RET_DK, RET_THETA, RET_DK)
    nsa_rope = _rope_tables(seq, ROPE_DIM, ROPE_THETA, NSA_DH)
    mcs_t = _cmp_to_sel_t(seq)
    expand = _expand_table(seq)
    for layer in range(n_layers):
        h1 = _ffn1(h, ffn1_norm_w[layer][None, :], ffn1_w_gate[layer].astype(BF16),
                   ffn1_w_up[layer].astype(BF16), ffn1_w_down[layer].astype(BF16))
        (rq, rk, rv, rg, nq_raw, nq_rot, kc, vc, ks, vs, kw, vw, ng) = _proj(
            h1, mix_norm_w[layer][None, :], _pack_w_in(w_in[layer]), ret_rope + nsa_rope, seq)
        ret = _retention(rq, rk, rv, rg, ret_tabs, ret_norm_w[layer][None, :], bsz, seq)
        kcmp, vcmp = _compress(
            kc, vc,
            _pack_cmp(cmp_pe_k[layer], cmp_k_w1[layer], cmp_k_b1[layer], cmp_k_w2[layer]),
            _pack_cmp(cmp_pe_v[layer], cmp_v_w1[layer], cmp_v_b1[layer], cmp_v_w2[layer]),
            bsz, seq)
        ocg, selbias = _cmp_attention(nq_raw, kcmp, vcmp, mcs_t, ng, bsz, seq)
        nsa = _sel_win(nq_rot, ks, vs, kw, vw, selbias, expand, ocg, ng, bsz, seq)
        h = _ffn2(h1, ret, nsa, w_out[layer].astype(BF16), ffn2_norm_w[layer][None, :],
                  ffn2_w_gate[layer].astype(BF16), ffn2_w_up[layer].astype(BF16),
                  ffn2_w_down[layer].astype(BF16), final_norm_w[None, :])
    return h.reshape(bsz, seq, d)
```

```python
import functools

import numpy as np
import jax
import jax.numpy as jnp
from jax import lax
from jax.experimental import pallas as pl
from jax.experimental.pallas import tpu as pltpu

F32 = jnp.float32
BF16 = jnp.bfloat16

D_MODEL = 1024
D_FF = 2816
RMS_EPS = 1e-6
GN_EPS = 1e-5

RET_HEADS = 4
RET_DK = 64
RET_DV = 128
RET_THETA = 10000.0

NSA_HEADS = 8
NSA_KV_HEADS = 2
NSA_GROUP = 4
NSA_DH = 64
CMP_LEN = 32
CMP_STRIDE = 16
CMP_HIDDEN = 256
SEL_LEN = 64
SEL_TOP = 16
WINDOW = 512
N_GATES = 3
ROPE_THETA = 500000.0
ROPE_DIM = 16

NEG_INF = -1e30
FORCE_BONUS = 1e4

LANE = 128
VMEM_LIMIT = 56 * 1024 * 1024

TM = 512
RET_CHUNK = 128
TQ = 256
TK = 256
SUB_Q = 128
FF_SPLIT = (1280, 1536)

SEG = dict(rq=(0, 256), rk=(256, 512), rv=(512, 1024), rg=(1024, 1536), nq=(1536, 2048),
           kc=(2048, 2176), vc=(2176, 2304), ks=(2304, 2432), vs=(2432, 2560),
           kw=(2560, 2688), vw=(2688, 2816), ng=(2816, 3072))
D_CAT = 3072


def _rms(x, w):
    return x * lax.rsqrt(jnp.mean(x * x, axis=-1, keepdims=True) + RMS_EPS) * w


def _params(sem):
    return pltpu.CompilerParams(dimension_semantics=sem, vmem_limit_bytes=VMEM_LIMIT)


def _const_spec(shape):
    nd = len(shape)
    return pl.BlockSpec(shape, lambda *_: (0,) * nd, pipeline_mode=pl.Buffered(1))


def _swiglu_half_step(h, nw, wg_ref, wu_ref, wd_ref):
    hn = _rms(h, nw).astype(BF16)
    acc = None
    off = 0
    for width in FF_SPLIT:
        g = jnp.dot(hn, wg_ref[:, off:off + width], preferred_element_type=F32)
        u = jnp.dot(hn, wu_ref[:, off:off + width], preferred_element_type=F32)
        a = (g * jax.nn.sigmoid(g) * u).astype(BF16)
        y = jnp.dot(a, wd_ref[off:off + width, :], preferred_element_type=F32)
        acc = y if acc is None else acc + y
        off += width
    return h + 0.5 * acc


def _ffn1_body(x_ref, nw_ref, wg_ref, wu_ref, wd_ref, o_ref):
    o_ref[...] = _swiglu_half_step(x_ref[...], nw_ref[...], wg_ref, wu_ref, wd_ref)


def _ffn1(x2, nw, wg, wu, wd):
    t = x2.shape[0]
    tok = pl.BlockSpec((TM, D_MODEL), lambda i: (i, 0))
    return pl.pallas_call(
        _ffn1_body,
        grid=(t // TM,),
        in_specs=[tok, _const_spec((1, D_MODEL)), _const_spec((D_MODEL, D_FF)),
                  _const_spec((D_MODEL, D_FF)), _const_spec((D_FF, D_MODEL))],
        out_specs=tok,
        out_shape=jax.ShapeDtypeStruct((t, D_MODEL), F32),
        compiler_params=_params(("parallel",)),
        name="ffn1",
    )(x2, nw, wg, wu, wd)


def _ffn2_body(h_ref, ret_ref, nsa_ref, wo_ref, nw_ref, wg_ref, wu_ref, wd_ref, fw_ref, o_ref):
    half = RET_HEADS * RET_DV
    mix = (jnp.dot(ret_ref[...], wo_ref[:half, :], preferred_element_type=F32)
           + jnp.dot(nsa_ref[...], wo_ref[half:, :], preferred_element_type=F32))
    h = h_ref[...] + mix
    h = _swiglu_half_step(h, nw_ref[...], wg_ref, wu_ref, wd_ref)
    o_ref[...] = _rms(h, fw_ref[...])


def _ffn2(h1, ret, nsa, wo, nw, wg, wu, wd, fw):
    t = h1.shape[0]
    tok = pl.BlockSpec((TM, D_MODEL), lambda i: (i, 0))
    half = pl.BlockSpec((TM, D_MODEL // 2), lambda i: (i, 0))
    return pl.pallas_call(
        _ffn2_body,
        grid=(t // TM,),
        in_specs=[tok, half, half, _const_spec((D_MODEL, D_MODEL)), _const_spec((1, D_MODEL)),
                  _const_spec((D_MODEL, D_FF)), _const_spec((D_MODEL, D_FF)),
                  _const_spec((D_FF, D_MODEL)), _const_spec((1, D_MODEL))],
        out_specs=tok,
        out_shape=jax.ShapeDtypeStruct((t, D_MODEL), F32),
        compiler_params=_params(("parallel",)),
        name="ffn2",
    )(h1, ret, nsa, wo, nw, wg, wu, wd, fw)


def _rope128(x, cos, sin_a, sin_b, half):
    return (x * cos + pltpu.roll(x, LANE - half, 1) * sin_a + pltpu.roll(x, half, 1) * sin_b)


def _proj_body(h_ref, nw_ref, w_ref, rc_ref, ra_ref, rb_ref, nc_ref, na_ref, nb_ref,
               rq_ref, rk_ref, rv_ref, rg_ref, nqr_ref, nqp_ref, kc_ref, vc_ref,
               ks_ref, vs_ref, kw_ref, vw_ref, ng_ref):
    u = _rms(h_ref[...], nw_ref[...]).astype(BF16)

    def seg(name):
        a, b = SEG[name]
        return jnp.dot(u, w_ref[:, a:b], preferred_element_type=F32)

    rc, ra, rb = rc_ref[...], ra_ref[...], rb_ref[...]
    nc, na, nb = nc_ref[...], na_ref[...], nb_ref[...]
    ret_half = RET_DK // 2
    nsa_half = ROPE_DIM // 2
    nsa_scale = NSA_DH ** -0.5

    p = seg("rq")
    for j in range(2):
        sl = slice(j * LANE, (j + 1) * LANE)
        rq_ref[:, sl] = _rope128(p[:, sl], rc, ra, rb, ret_half)
    p = seg("rk")
    for j in range(2):
        sl = slice(j * LANE, (j + 1) * LANE)
        rk_ref[:, sl] = _rope128(p[:, sl], rc, ra, rb, ret_half) * (RET_DK ** -0.5)
    rv_ref[...] = seg("rv").astype(BF16)
    rg_ref[...] = seg("rg")
    p = seg("nq")
    nqr_ref[...] = (p * nsa_scale).astype(BF16)
    for j in range(4):
        sl = slice(j * LANE, (j + 1) * LANE)
        nqp_ref[:, sl] = (_rope128(p[:, sl], nc, na, nb, nsa_half) * nsa_scale).astype(BF16)
    kc_ref[...] = seg("kc")
    vc_ref[...] = seg("vc")
    ks_ref[...] = _rope128(seg("ks"), nc, na, nb, nsa_half).T.astype(BF16)
    vs_ref[...] = seg("vs").astype(BF16)
    kw_ref[...] = _rope128(seg("kw"), nc, na, nb, nsa_half).T.astype(BF16)
    vw_ref[...] = seg("vw").astype(BF16)
    ng_ref[...] = seg("ng")


def _proj(h1, nw, wcat, tabs, seq):
    t = h1.shape[0]
    per_seq = seq // TM

    def tok(width):
        return pl.BlockSpec((TM, width), lambda i: (i, 0))

    tab = pl.BlockSpec((TM, LANE), lambda i: (i % per_seq, 0))
    widths = [256, 256, 512, 512, 512, 512, 128, 128, 128, 128, 128, 128, 256]
    dtypes = [F32, F32, BF16, F32, BF16, BF16, F32, F32, BF16, BF16, BF16, BF16, F32]
    out_specs = [tok(w) for w in widths]
    out_shape = [jax.ShapeDtypeStruct((t, w), d) for w, d in zip(widths, dtypes)]
    for idx in (8, 10):
        out_specs[idx] = pl.BlockSpec((None, LANE, TM), lambda i: (i // per_seq, 0, i % per_seq))
        out_shape[idx] = jax.ShapeDtypeStruct((t // seq, LANE, seq), BF16)
    return pl.pallas_call(
        _proj_body,
        grid=(t // TM,),
        in_specs=[tok(D_MODEL), _const_spec((1, D_MODEL)), _const_spec((D_MODEL, D_CAT))] + [tab] * 6,
        out_specs=out_specs,
        out_shape=out_shape,
        compiler_params=_params(("parallel",)),
        name="proj",
    )(h1, nw, wcat, *tabs)


def _ret_body(q_ref, k_ref, v_ref, g_ref, dm_ref, xi_ref, ze_ref, gc_ref, nw_ref, o_ref, st_ref, *, seq):
    c = RET_CHUNK
    st_ref[...] = jnp.zeros_like(st_ref)

    def chunk(n, carry):
        rows = pl.ds(pl.multiple_of(n * c, c), c)
        for h in range(RET_HEADS):
            ks = slice(h * RET_DK, (h + 1) * RET_DK)
            vs = slice(h * RET_DV, (h + 1) * RET_DV)
            q = q_ref[rows, ks]
            k = k_ref[rows, ks]
            v = v_ref[rows, vs]
            state = st_ref[h]
            inner = lax.dot_general(q.astype(BF16), k.astype(BF16), (((1,), (1,)), ((), ())),
                                    preferred_element_type=F32) * dm_ref[h]
            out = (jnp.dot(inner.astype(BF16), v, preferred_element_type=F32)
                   + jnp.dot((q * xi_ref[h]).astype(BF16), state.astype(BF16),
                             preferred_element_type=F32))
            kz = (k * ze_ref[h]).astype(BF16)
            st_ref[h] = gc_ref[h] * state + lax.dot_general(
                kz, v, (((0,), (0,)), ((), ())), preferred_element_type=F32)
            mu = jnp.mean(out, axis=-1, keepdims=True)
            d = out - mu
            var = jnp.mean(d * d, axis=-1, keepdims=True)
            y = d * lax.rsqrt(var + GN_EPS) * nw_ref[:, vs]
            gate = g_ref[rows, vs]
            o_ref[rows, vs] = (y * (gate * jax.nn.sigmoid(gate))).astype(BF16)
        return carry

    lax.fori_loop(0, seq // c, chunk, 0)


def _retention(rq, rk, rv, rg, tabs, nw, bsz, seq):
    t = bsz * seq
    c = RET_CHUNK

    def seq_blk(width):
        return pl.BlockSpec((seq, width), lambda b: (b, 0))

    return pl.pallas_call(
        functools.partial(_ret_body, seq=seq),
        grid=(bsz,),
        in_specs=[seq_blk(256), seq_blk(256), seq_blk(512), seq_blk(512),
                  _const_spec((RET_HEADS, c, c)), _const_spec((RET_HEADS, c, 1)),
                  _const_spec((RET_HEADS, c, 1)), _const_spec((RET_HEADS, 1, 1)),
                  _const_spec((1, RET_HEADS * RET_DV))],
        out_specs=seq_blk(512),
        out_shape=jax.ShapeDtypeStruct((t, RET_HEADS * RET_DV), BF16),
        scratch_shapes=[pltpu.VMEM((RET_HEADS, RET_DK, RET_DV), F32)],
        compiler_params=_params(("parallel",)),
        name="retention",
    )(rq, rk, rv, rg, *tabs, nw)


def _gelu_tanh(x):
    return 0.5 * x * (1.0 + jnp.tanh(np.sqrt(2.0 / np.pi).astype(np.float32) * (x + 0.044715 * (x * x * x))))


def _cmp_body(kc_ref, vc_ref, pek_ref, w1k_ref, b1k_ref, w2k_ref, pev_ref, w1v_ref, b1v_ref, w2v_ref,
              ko_ref, vo_ref, *, seq):
    n_grp = seq // CMP_STRIDE

    def one(x_ref, pe_ref, w1_ref, b1_ref, w2_ref, o_ref):
        first = None
        second = None
        for l in range(CMP_STRIDE):
            x = x_ref[pl.ds(l, n_grp, stride=CMP_STRIDE), :]
            a = jnp.dot((x + pe_ref[l:l + 1, :]).astype(BF16), w1_ref[l], preferred_element_type=F32)
            b = jnp.dot((x + pe_ref[CMP_STRIDE + l:CMP_STRIDE + l + 1, :]).astype(BF16),
                        w1_ref[CMP_STRIDE + l], preferred_element_type=F32)
            first = a if first is None else first + a
            second = b if second is None else second + b
        pre = first + pltpu.roll(second, n_grp - 1, 0) + b1_ref[...]
        hid = _gelu_tanh(pre).astype(BF16)
        res = jnp.dot(hid, w2_ref[...], preferred_element_type=F32)
        row = lax.broadcasted_iota(jnp.int32, res.shape, 0)
        res = jnp.where(row < n_grp - 1, res, 0.0).astype(BF16)
        for g in range(NSA_KV_HEADS):
            o_ref[g] = res[:, g * NSA_DH:(g + 1) * NSA_DH]

    one(kc_ref, pek_ref, w1k_ref, b1k_ref, w2k_ref, ko_ref)
    one(vc_ref, pev_ref, w1v_ref, b1v_ref, w2v_ref, vo_ref)


def _compress(kc, vc, wk, wv, bsz, seq):
    n_grp = seq // CMP_STRIDE
    blk = pl.BlockSpec((seq, LANE), lambda b: (b, 0))
    wspecs = [_const_spec((CMP_LEN, LANE)), _const_spec((CMP_LEN, LANE, 2 * CMP_HIDDEN)),
              _const_spec((1, 2 * CMP_HIDDEN)), _const_spec((2 * CMP_HIDDEN, LANE))]
    out = pl.BlockSpec((None, NSA_KV_HEADS, n_grp, NSA_DH), lambda b: (b, 0, 0, 0))
    shape = jax.ShapeDtypeStruct((bsz, NSA_KV_HEADS, n_grp, NSA_DH), BF16)
    return pl.pallas_call(
        functools.partial(_cmp_body, seq=seq),
        grid=(bsz,),
        in_specs=[blk, blk] + wspecs + wspecs,
        out_specs=[out, out],
        out_shape=[shape, shape],
        compiler_params=_params(("parallel",)),
        name="compress",
    )(kc, vc, *wk, *wv)


def _cmpattn_body(q_ref, kc_ref, vc_ref, mcs_ref, ng_ref, oc_ref, sb_ref, *, n_top):
    i = pl.program_id(2)
    kc = kc_ref[...]
    vc = vc_ref[...]
    n_grp = kc.shape[0]
    n_sel = mcs_ref.shape[0]
    t0 = i * TQ
    cidx = lax.broadcasted_iota(jnp.int32, (n_grp, TQ), 0)
    tpos = t0 + lax.broadcasted_iota(jnp.int32, (n_grp, TQ), 1)
    mask = (cidx * CMP_STRIDE + (CMP_LEN - 1)) <= tpos
    gates = jax.nn.sigmoid(ng_ref[...])

    psum = jnp.zeros((n_grp, TQ), F32)
    for k in range(NSA_GROUP):
        q = q_ref[:, k * NSA_DH:(k + 1) * NSA_DH]
        s = lax.dot_general(kc, q, (((1,), (1,)), ((), ())), preferred_element_type=F32)
        s = jnp.where(mask, s, NEG_INF)
        m = jnp.max(s, axis=0, keepdims=True)
        e = jnp.where(mask, jnp.exp(s - m), 0.0)
        l = jnp.sum(e, axis=0, keepdims=True)
        p = e / jnp.where(l > 0.0, l, 1.0)
        psum = psum + p
        o = lax.dot_general(p.astype(BF16), vc, (((0,), (0,)), ((), ())), preferred_element_type=F32)
        oc_ref[:, k * NSA_DH:(k + 1) * NSA_DH] = o * gates[:, k * N_GATES:k * N_GATES + 1]

    hi = psum.astype(BF16)
    lo = (psum - hi.astype(F32)).astype(BF16)
    mcs = mcs_ref[...]
    imp = (jnp.dot(mcs, hi, preferred_element_type=F32) + jnp.dot(mcs, lo, preferred_element_type=F32))
    blk = lax.broadcasted_iota(jnp.int32, (n_sel, TQ), 0)
    cur = jnp.right_shift(t0 + lax.broadcasted_iota(jnp.int32, (n_sel, TQ), 1), 6)
    valid = blk <= cur
    forced = (blk == 0) | (blk == cur) | (blk == cur - 1)
    imp = jnp.where(valid, imp + jnp.where(forced, FORCE_BONUS, 0.0), NEG_INF)
    cnt = jnp.zeros((n_sel, TQ), jnp.int32)
    for kk in range(n_sel):
        row = imp[kk:kk + 1, :]
        beats = (row > imp) | ((row == imp) & (blk > kk))
        cnt = cnt + beats.astype(jnp.int32)
    bias = jnp.where(cnt < n_top, 0.0, NEG_INF)
    sb_ref[...] = bias.T.astype(BF16)


def _cmp_attention(nq_raw, kcmp, vcmp, mcs_t, ng, bsz, seq):
    t = bsz * seq
    nq = seq // TQ
    n_grp = seq // CMP_STRIDE
    n_sel = seq // SEL_LEN
    n_top = min(SEL_TOP, n_sel)
    half = NSA_GROUP * NSA_DH
    return pl.pallas_call(
        functools.partial(_cmpattn_body, n_top=n_top),
        grid=(bsz, NSA_KV_HEADS, nq),
        in_specs=[pl.BlockSpec((TQ, half), lambda b, g, i: (b * nq + i, g)),
                  pl.BlockSpec((None, None, n_grp, NSA_DH), lambda b, g, i: (b, g, 0, 0)),
                  pl.BlockSpec((None, None, n_grp, NSA_DH), lambda b, g, i: (b, g, 0, 0)),
                  _const_spec((n_sel, n_grp)),
                  pl.BlockSpec((TQ, LANE), lambda b, g, i: (b * nq + i, g))],
        out_specs=[pl.BlockSpec((TQ, half), lambda b, g, i: (b * nq + i, g)),
                   pl.BlockSpec((None, None, TQ, n_sel), lambda b, g, i: (b, g, i, 0))],
        out_shape=[jax.ShapeDtypeStruct((t, NSA_HEADS * NSA_DH), F32),
                   jax.ShapeDtypeStruct((bsz, NSA_KV_HEADS, seq, n_sel), BF16)],
        compiler_params=_params(("parallel", "parallel", "parallel")),
        name="cmp_attn",
    )(nq_raw, kcmp, vcmp, mcs_t, ng)


def _selwin_body(q_ref, kst_ref, vs_ref, kwt_ref, vw_ref, sb_ref, oh_ref, oc_ref, ng_ref, o_ref,
                 kt_ref, qa_ref, m_ref, l_ref, acc_ref, res_ref):
    i = pl.program_id(1)
    n_sel = oh_ref.shape[0]
    t0 = i * TQ

    @pl.when(i == 0)
    def _():
        kt_ref[...] = jnp.zeros_like(kt_ref)
        qa_ref[...] = jnp.zeros_like(qa_ref)
        for g in range(NSA_KV_HEADS):
            rows = slice(g * NSA_DH, (g + 1) * NSA_DH)
            kt_ref[2 * g, 0:NSA_DH, :] = kst_ref[rows, :]
            kt_ref[2 * g, NSA_DH:NSA_DH + n_sel, :] = oh_ref[...]
            kt_ref[2 * g + 1, 0:NSA_DH, :] = kwt_ref[rows, :]

    gates = jax.nn.sigmoid(ng_ref[...])
    q_pos = lax.broadcasted_iota(jnp.int32, (SUB_Q, TK), 0)
    k_off = lax.broadcasted_iota(jnp.int32, (SUB_Q, TK), 1)

    def reset():
        m_ref[...] = jnp.full_like(m_ref, NEG_INF)
        l_ref[...] = jnp.zeros_like(l_ref)
        acc_ref[...] = jnp.zeros_like(acc_ref)

    def step(kidx, v_ref, j, mask):
        cols = pl.ds(pl.multiple_of(j * TK, TK), TK)
        kt = kt_ref[kidx, :, cols]
        v = v_ref[cols, :]
        for c in range(TQ // SUB_Q):
            rows = slice(c * SUB_Q, (c + 1) * SUB_Q)
            rel = (j * TK - t0 - c * SUB_Q) + k_off - q_pos
            for kq in range(NSA_GROUP):
                s = jnp.dot(qa_ref[kq, rows, :], kt, preferred_element_type=F32)
                if mask == "causal":
                    s = jnp.where(rel <= 0, s, NEG_INF)
                elif mask == "lower":
                    s = jnp.where(rel > -WINDOW, s, NEG_INF)
                m_prev = m_ref[kq, rows, :]
                m_next = jnp.maximum(m_prev, jnp.max(s, axis=1, keepdims=True))
                alpha = jnp.exp(m_prev - m_next)
                p = jnp.exp(s - jnp.concatenate([m_next] * (TK // LANE), axis=1))
                l_ref[kq, rows, :] = alpha * l_ref[kq, rows, :] + jnp.sum(p, axis=1, keepdims=True)
                acc_ref[kq, rows, :] = alpha * acc_ref[kq, rows, :] + jnp.dot(
                    p.astype(BF16), v, preferred_element_type=F32)
                m_ref[kq, rows, :] = m_next

    def finalize(g, gate_idx, first):
        for kq in range(NSA_GROUP):
            hh = g * NSA_GROUP + kq
            dst = slice(hh * NSA_DH, (hh + 1) * NSA_DH)
            col = g * LANE + kq * N_GATES + gate_idx
            o = (acc_ref[kq] / l_ref[kq])[:, g * NSA_DH:(g + 1) * NSA_DH] * gates[:, col:col + 1]
            if first:
                res_ref[:, dst] = oc_ref[:, dst] + o
            else:
                res_ref[:, dst] += o

    for g in range(NSA_KV_HEADS):
        for kq in range(NSA_GROUP):
            hh = g * NSA_GROUP + kq
            qa_ref[kq, :, 0:NSA_DH] = q_ref[:, hh * NSA_DH:(hh + 1) * NSA_DH]
            qa_ref[kq, :, NSA_DH:NSA_DH + n_sel] = sb_ref[g]

        reset()

        def sel_step(j, carry, g=g):
            step(2 * g, vs_ref, j, None)
            return carry

        lax.fori_loop(0, i, sel_step, 0)
        step(2 * g, vs_ref, i, "causal")
        finalize(g, 1, True)

        reset()

        @pl.when(i >= 2)
        def _(g=g):
            step(2 * g + 1, vw_ref, i - 2, "lower")

        @pl.when(i >= 1)
        def _(g=g):
            step(2 * g + 1, vw_ref, i - 1, None)

        step(2 * g + 1, vw_ref, i, "causal")
        finalize(g, 2, False)

    o_ref[...] = res_ref[...].astype(BF16)


def _sel_win(nq_rot, kst, vs, kwt, vw, selbias, onehot_t, ocg, ng, bsz, seq):
    assert WINDOW == 2 * TK and TQ == TK
    t = bsz * seq
    nq = seq // TQ
    n_sel = seq // SEL_LEN
    width = NSA_HEADS * NSA_DH
    tokq = pl.BlockSpec((TQ, width), lambda b, i: (b * nq + i, 0))
    kv = pl.BlockSpec((seq, LANE), lambda b, i: (b, 0))
    kvt = pl.BlockSpec((None, LANE, seq), lambda b, i: (b, 0, 0))
    return pl.pallas_call(
        _selwin_body,
        grid=(bsz, nq),
        in_specs=[tokq, kvt, kv, kvt, kv,
                  pl.BlockSpec((None, NSA_KV_HEADS, TQ, n_sel), lambda b, i: (b, 0, i, 0)),
                  _const_spec((n_sel, seq)),
                  tokq,
                  pl.BlockSpec((TQ, 2 * LANE), lambda b, i: (b * nq + i, 0))],
        out_specs=tokq,
        out_shape=jax.ShapeDtypeStruct((t, width), BF16),
        scratch_shapes=[pltpu.VMEM((2 * NSA_KV_HEADS, LANE, seq), BF16),
                        pltpu.VMEM((NSA_GROUP, TQ, LANE), BF16),
                        pltpu.VMEM((NSA_GROUP, TQ, LANE), F32), pltpu.VMEM((NSA_GROUP, TQ, LANE), F32),
                        pltpu.VMEM((NSA_GROUP, TQ, LANE), F32), pltpu.VMEM((TQ, width), F32)],
        compiler_params=_params(("parallel", "arbitrary")),
        name="sel_win",
    )(nq_rot, kst, vs, kwt, vw, selbias, onehot_t, ocg, ng)


def _rope_tables(seq, rot_dim, theta, head_dim):
    half = rot_dim // 2
    inv_freq = theta ** (-2.0 * jnp.arange(half, dtype=F32) / rot_dim)
    ang = jnp.arange(seq, dtype=F32)[:, None] * inv_freq[None, :]
    cos, sin = jnp.cos(ang), jnp.sin(ang)
    rest = head_dim - rot_dim
    ones = jnp.ones((seq, rest), F32)
    zr = jnp.zeros((seq, rest), F32)
    zh = jnp.zeros((seq, half), F32)
    cos_t = jnp.concatenate([cos, cos, ones], axis=1)
    sin_a = jnp.concatenate([-sin, zh, zr], axis=1)
    sin_b = jnp.concatenate([zh, sin, zr], axis=1)
    rep = LANE // head_dim
    return [jnp.tile(a, (1, rep)) for a in (cos_t, sin_a, sin_b)]


def _retention_tables():
    c = RET_CHUNK
    gamma = 1.0 - 2.0 ** (-5.0 - jnp.arange(RET_HEADS, dtype=F32))
    log_g = jnp.log(gamma)
    j = jnp.arange(c, dtype=F32)
    diff = j[:, None] - j[None, :]
    dmat = jnp.where(diff >= 0, jnp.exp(log_g[:, None, None] * jnp.maximum(diff, 0.0)), 0.0)
    xi = jnp.exp(log_g[:, None] * (j[None, :] + 1.0))[:, :, None]
    zeta = jnp.exp(log_g[:, None] * (c - 1.0 - j[None, :]))[:, :, None]
    g_chunk = jnp.exp(log_g * c)[:, None, None]
    return [dmat, xi, zeta, g_chunk]


def _cmp_to_sel_t(seq):
    n_grp = seq // CMP_STRIDE
    n_sel = seq // SEL_LEN
    c_start = np.arange(n_grp) * CMP_STRIDE
    s_start = np.arange(n_sel) * SEL_LEN
    overlap = (np.minimum(c_start[None, :] + CMP_LEN, s_start[:, None] + SEL_LEN)
               - np.maximum(c_start[None, :], s_start[:, None]))
    m = np.clip(overlap, 0, None) / CMP_LEN
    m[:, n_grp - 1] = 0.0
    return jnp.asarray(m, dtype=BF16)


def _block_onehot_t(seq):
    n_sel = seq // SEL_LEN
    e = np.arange(n_sel)[:, None] == (np.arange(seq) // SEL_LEN)[None, :]
    return jnp.asarray(e, dtype=BF16)


def _pack_w_in(w_in):
    gate_off = 1536 + 512 + 6 * 128
    per_group = NSA_GROUP * N_GATES
    pad = jnp.zeros((D_MODEL, LANE - per_group), w_in.dtype)
    cols = [w_in[:, :gate_off]]
    for g in range(NSA_KV_HEADS):
        cols += [w_in[:, gate_off + g * per_group: gate_off + (g + 1) * per_group], pad]
    return jnp.concatenate(cols, axis=1).astype(BF16)


def _pack_cmp(pe, w1, b1, w2):
    pe2 = pe.reshape(CMP_LEN, NSA_KV_HEADS * NSA_DH)
    w1l = w1.reshape(CMP_LEN, NSA_DH, CMP_HIDDEN)
    z1 = jnp.zeros_like(w1l)
    w1bd = jnp.concatenate([jnp.concatenate([w1l, z1], axis=2),
                            jnp.concatenate([z1, w1l], axis=2)], axis=1).astype(BF16)
    b1bd = jnp.concatenate([b1, b1])[None, :]
    z2 = jnp.zeros_like(w2)
    w2bd = jnp.concatenate([jnp.concatenate([w2, z2], axis=1),
                            jnp.concatenate([z2, w2], axis=1)], axis=0).astype(BF16)
    return [pe2, w1bd, b1bd, w2bd]


def kernel(x, ffn1_norm_w, ffn1_w_gate, ffn1_w_up, ffn1_w_down, mix_norm_w, w_in, ret_norm_w, cmp_pe_k, cmp_k_w1, cmp_k_b1, cmp_k_w2, cmp_pe_v, cmp_v_w1, cmp_v_b1, cmp_v_w2, w_out, ffn2_norm_w, ffn2_w_gate, ffn2_w_up, ffn2_w_down, final_norm_w):
    bsz, seq, d = x.shape
    assert d == D_MODEL and seq % TM == 0 and seq % TQ == 0 and seq >= WINDOW
    t = bsz * seq
    h = x.reshape(t, d)
    n_layers = ffn1_norm_w.shape[0]
    ret_tabs = _retention_tables()
    ret_rope = _rope_tables(seq, RET_DK, RET_THETA, RET_DK)
    nsa_rope = _rope_tables(seq, ROPE_DIM, ROPE_THETA, NSA_DH)
    mcs_t = _cmp_to_sel_t(seq)
    onehot_t = _block_onehot_t(seq)
    for layer in range(n_layers):
        h1 = _ffn1(h, ffn1_norm_w[layer][None, :], ffn1_w_gate[layer].astype(BF16),
                   ffn1_w_up[layer].astype(BF16), ffn1_w_down[layer].astype(BF16))
        (rq, rk, rv, rg, nq_raw, nq_rot, kc, vc, ks, vs, kw, vw, ng) = _proj(
            h1, mix_norm_w[layer][None, :], _pack_w_in(w_in[layer]), ret_rope + nsa_rope, seq)
        ret = _retention(rq, rk, rv, rg, ret_tabs, ret_norm_w[layer][None, :], bsz, seq)
        kcmp, vcmp = _compress(
            kc, vc,
            _pack_cmp(cmp_pe_k[layer], cmp_k_w1[layer], cmp_k_b1[layer], cmp_k_w2[layer]),
            _pack_cmp(cmp_pe_v[layer], cmp_v_w1[layer], cmp_v_b1[layer], cmp_v_w2[layer]),
            bsz, seq)
        ocg, selbias = _cmp_attention(nq_raw, kcmp, vcmp, mcs_t, ng, bsz, seq)
        nsa = _sel_win(nq_rot, ks, vs, kw, vw, selbias, onehot_t, ocg, ng, bsz, seq)
        h = _ffn2(h1, ret, nsa, w_out[layer].astype(BF16), ffn2_norm_w[layer][None, :],
                  ffn2_w_gate[layer].astype(BF16), ffn2_w_up[layer].astype(BF16),
                  ffn2_w_down[layer].astype(BF16), final_norm_w[None, :])
    return h.reshape(bsz, seq, d)
```

```python
import functools

import numpy as np
import jax
import jax.numpy as jnp
from jax import lax
from jax.experimental import pallas as pl
from jax.experimental.pallas import tpu as pltpu

F32 = jnp.float32
BF16 = jnp.bfloat16

D_MODEL = 1024
D_FF = 2816
RMS_EPS = 1e-6
GN_EPS = 1e-5

RET_HEADS = 4
RET_DK = 64
RET_DV = 128
RET_THETA = 10000.0

NSA_HEADS = 8
NSA_KV_HEADS = 2
NSA_GROUP = 4
NSA_DH = 64
CMP_LEN = 32
CMP_STRIDE = 16
CMP_HIDDEN = 256
SEL_LEN = 64
SEL_TOP = 16
WINDOW = 512
N_GATES = 3
ROPE_THETA = 500000.0
ROPE_DIM = 16

NEG_INF = -1e30
FORCE_BONUS = 1e4

LANE = 128
VMEM_LIMIT = 56 * 1024 * 1024

TM = 512
RET_CHUNK = 128
TQ = 256
TK = 256
FF_SPLIT = (1280, 1536)

SEG = dict(rq=(0, 256), rk=(256, 512), rv=(512, 1024), rg=(1024, 1536), nq=(1536, 2048),
           kc=(2048, 2176), vc=(2176, 2304), ks=(2304, 2432), vs=(2432, 2560),
           kw=(2560, 2688), vw=(2688, 2816), ng=(2816, 2944))
D_CAT = 2944
GATE_ROWS = 32


def _rms(x, w):
    return x * lax.rsqrt(jnp.mean(x * x, axis=-1, keepdims=True) + RMS_EPS) * w


def _params(sem):
    return pltpu.CompilerParams(dimension_semantics=sem, vmem_limit_bytes=VMEM_LIMIT)


def _const_spec(shape):
    nd = len(shape)
    return pl.BlockSpec(shape, lambda *_: (0,) * nd, pipeline_mode=pl.Buffered(1))


def _swiglu_half_step(h, nw, wg_ref, wu_ref, wd_ref):
    hn = _rms(h, nw).astype(BF16)
    acc = None
    off = 0
    for width in FF_SPLIT:
        g = jnp.dot(hn, wg_ref[:, off:off + width], preferred_element_type=F32)
        u = jnp.dot(hn, wu_ref[:, off:off + width], preferred_element_type=F32)
        a = (g * jax.nn.sigmoid(g) * u).astype(BF16)
        y = jnp.dot(a, wd_ref[off:off + width, :], preferred_element_type=F32)
        acc = y if acc is None else acc + y
        off += width
    return h + 0.5 * acc


def _ffn1_body(x_ref, nw_ref, wg_ref, wu_ref, wd_ref, o_ref):
    o_ref[...] = _swiglu_half_step(x_ref[...], nw_ref[...], wg_ref, wu_ref, wd_ref)


def _ffn1(x2, nw, wg, wu, wd):
    t = x2.shape[0]
    tok = pl.BlockSpec((TM, D_MODEL), lambda i: (i, 0))
    return pl.pallas_call(
        _ffn1_body,
        grid=(t // TM,),
        in_specs=[tok, _const_spec((1, D_MODEL)), _const_spec((D_MODEL, D_FF)),
                  _const_spec((D_MODEL, D_FF)), _const_spec((D_FF, D_MODEL))],
        out_specs=tok,
        out_shape=jax.ShapeDtypeStruct((t, D_MODEL), F32),
        compiler_params=_params(("parallel",)),
        name="ffn1",
    )(x2, nw, wg, wu, wd)


def _ffn2_body(h_ref, ret_ref, nsa_ref, wo_ref, nw_ref, wg_ref, wu_ref, wd_ref, fw_ref, o_ref):
    half = RET_HEADS * RET_DV
    mix = (jnp.dot(ret_ref[...], wo_ref[:half, :], preferred_element_type=F32)
           + jnp.dot(nsa_ref[...], wo_ref[half:, :], preferred_element_type=F32))
    h = h_ref[...] + mix
    h = _swiglu_half_step(h, nw_ref[...], wg_ref, wu_ref, wd_ref)
    o_ref[...] = _rms(h, fw_ref[...])


def _ffn2(h1, ret, nsa, wo, nw, wg, wu, wd, fw):
    t = h1.shape[0]
    tok = pl.BlockSpec((TM, D_MODEL), lambda i: (i, 0))
    half = pl.BlockSpec((TM, D_MODEL // 2), lambda i: (i, 0))
    return pl.pallas_call(
        _ffn2_body,
        grid=(t // TM,),
        in_specs=[tok, half, half, _const_spec((D_MODEL, D_MODEL)), _const_spec((1, D_MODEL)),
                  _const_spec((D_MODEL, D_FF)), _const_spec((D_MODEL, D_FF)),
                  _const_spec((D_FF, D_MODEL)), _const_spec((1, D_MODEL))],
        out_specs=tok,
        out_shape=jax.ShapeDtypeStruct((t, D_MODEL), F32),
        compiler_params=_params(("parallel",)),
        name="ffn2",
    )(h1, ret, nsa, wo, nw, wg, wu, wd, fw)


def _rope128(x, cos, sin_a, sin_b, half):
    return (x * cos + pltpu.roll(x, LANE - half, 1) * sin_a + pltpu.roll(x, half, 1) * sin_b)


def _proj_body(h_ref, nw_ref, w_ref, rc_ref, ra_ref, rb_ref, nc_ref, na_ref, nb_ref,
               rq_ref, rk_ref, rv_ref, rg_ref, qr_ref, qp_ref, kc_ref, vc_ref,
               ks_ref, vs_ref, kw_ref, vw_ref, ng_ref):
    u = _rms(h_ref[...], nw_ref[...]).astype(BF16)

    def seg(name):
        a, b = SEG[name]
        return jnp.dot(u, w_ref[:, a:b], preferred_element_type=F32)

    rc, ra, rb = rc_ref[...], ra_ref[...], rb_ref[...]
    nc, na, nb = nc_ref[...], na_ref[...], nb_ref[...]
    ret_half = RET_DK // 2
    nsa_half = ROPE_DIM // 2
    nsa_scale = NSA_DH ** -0.5

    p = seg("rq")
    for j in range(2):
        sl = slice(j * LANE, (j + 1) * LANE)
        rq_ref[:, sl] = _rope128(p[:, sl], rc, ra, rb, ret_half)
    p = seg("rk")
    for j in range(2):
        sl = slice(j * LANE, (j + 1) * LANE)
        rk_ref[:, sl] = _rope128(p[:, sl], rc, ra, rb, ret_half) * (RET_DK ** -0.5)
    rv_ref[...] = seg("rv").astype(BF16)
    rg_ref[...] = seg("rg")
    p = seg("nq")
    for j in range(4):
        sl = slice(j * LANE, (j + 1) * LANE)
        qr_ref[sl, :] = (p[:, sl] * nsa_scale).T.astype(BF16)
        qp_ref[sl, :] = (_rope128(p[:, sl], nc, na, nb, nsa_half) * nsa_scale).T.astype(BF16)
    kc_ref[...] = seg("kc")
    vc_ref[...] = seg("vc")
    ks_ref[...] = _rope128(seg("ks"), nc, na, nb, nsa_half).astype(BF16)
    vs_ref[...] = seg("vs").T.astype(BF16)
    kw_ref[...] = _rope128(seg("kw"), nc, na, nb, nsa_half).astype(BF16)
    vw_ref[...] = seg("vw").T.astype(BF16)
    ng_ref[...] = seg("ng").T[:GATE_ROWS, :]


def _proj(h1, nw, wcat, tabs, seq):
    t = h1.shape[0]
    bsz = t // seq
    per_seq = seq // TM

    def tok(width, dtype):
        return pl.BlockSpec((TM, width), lambda i: (i, 0)), jax.ShapeDtypeStruct((t, width), dtype)

    def tr(rows, dtype):
        return (pl.BlockSpec((None, rows, TM), lambda i: (i // per_seq, 0, i % per_seq)),
                jax.ShapeDtypeStruct((bsz, rows, seq), dtype))

    tab = pl.BlockSpec((TM, LANE), lambda i: (i % per_seq, 0))
    width = NSA_HEADS * NSA_DH
    outs = [tok(256, F32), tok(256, F32), tok(512, BF16), tok(512, F32),
            tr(width, BF16), tr(width, BF16),
            tok(LANE, F32), tok(LANE, F32),
            tok(LANE, BF16), tr(LANE, BF16), tok(LANE, BF16), tr(LANE, BF16),
            tr(GATE_ROWS, F32)]
    return pl.pallas_call(
        _proj_body,
        grid=(t // TM,),
        in_specs=[tok(D_MODEL, F32)[0], _const_spec((1, D_MODEL)), _const_spec((D_MODEL, D_CAT))] + [tab] * 6,
        out_specs=[o[0] for o in outs],
        out_shape=[o[1] for o in outs],
        compiler_params=_params(("parallel",)),
        name="proj",
    )(h1, nw, wcat, *tabs)


def _ret_body(q_ref, k_ref, v_ref, g_ref, dm_ref, xi_ref, ze_ref, gc_ref, nw_ref, o_ref, st_ref, *, seq):
    c = RET_CHUNK
    st_ref[...] = jnp.zeros_like(st_ref)

    def chunk(n, carry):
        rows = pl.ds(pl.multiple_of(n * c, c), c)
        heads = range(RET_HEADS)
        ks = [slice(h * RET_DK, (h + 1) * RET_DK) for h in heads]
        vs = [slice(h * RET_DV, (h + 1) * RET_DV) for h in heads]
        q = [q_ref[rows, ks[h]] for h in heads]
        k = [k_ref[rows, ks[h]] for h in heads]
        v = [v_ref[rows, vs[h]] for h in heads]
        state = [st_ref[h] for h in heads]
        inner = [lax.dot_general(q[h].astype(BF16), k[h].astype(BF16), (((1,), (1,)), ((), ())),
                                 preferred_element_type=F32) for h in heads]
        cross = [jnp.dot((q[h] * xi_ref[h]).astype(BF16), state[h].astype(BF16),
                         preferred_element_type=F32) for h in heads]
        upd = [lax.dot_general((k[h] * ze_ref[h]).astype(BF16), v[h], (((0,), (0,)), ((), ())),
                               preferred_element_type=F32) for h in heads]
        for h in heads:
            st_ref[h] = gc_ref[h] * state[h] + upd[h]
        for h in heads:
            out = jnp.dot((inner[h] * dm_ref[h]).astype(BF16), v[h], preferred_element_type=F32) + cross[h]
            mu = jnp.mean(out, axis=-1, keepdims=True)
            d = out - mu
            var = jnp.mean(d * d, axis=-1, keepdims=True)
            y = d * lax.rsqrt(var + GN_EPS) * nw_ref[:, vs[h]]
            gate = g_ref[rows, vs[h]]
            o_ref[rows, vs[h]] = (y * (gate * jax.nn.sigmoid(gate))).astype(BF16)
        return carry

    lax.fori_loop(0, seq // c, chunk, 0, unroll=2)


def _retention(rq, rk, rv, rg, tabs, nw, bsz, seq):
    t = bsz * seq
    c = RET_CHUNK

    def seq_blk(width):
        return pl.BlockSpec((seq, width), lambda b: (b, 0))

    return pl.pallas_call(
        functools.partial(_ret_body, seq=seq),
        grid=(bsz,),
        in_specs=[seq_blk(256), seq_blk(256), seq_blk(512), seq_blk(512),
                  _const_spec((RET_HEADS, c, c)), _const_spec((RET_HEADS, c, RET_DK)),
                  _const_spec((RET_HEADS, c, RET_DK)), _const_spec((RET_HEADS, 1, 1)),
                  _const_spec((1, RET_HEADS * RET_DV))],
        out_specs=seq_blk(512),
        out_shape=jax.ShapeDtypeStruct((t, RET_HEADS * RET_DV), BF16),
        scratch_shapes=[pltpu.VMEM((RET_HEADS, RET_DK, RET_DV), F32)],
        compiler_params=_params(("parallel",)),
        name="retention",
    )(rq, rk, rv, rg, *tabs, nw)


def _gelu_tanh(x):
    return 0.5 * x * (1.0 + jnp.tanh(np.sqrt(2.0 / np.pi).astype(np.float32) * (x + 0.044715 * (x * x * x))))


def _cmp_body(kc_ref, vc_ref, pek_ref, w1k_ref, b1k_ref, w2k_ref, pev_ref, w1v_ref, b1v_ref, w2v_ref,
              ko_ref, vo_ref, *, seq):
    n_grp = seq // CMP_STRIDE

    def one(x_ref, pe_ref, w1_ref, b1_ref, w2_ref, o_ref, transposed):
        first = None
        second = None
        for l in range(CMP_STRIDE):
            x = x_ref[pl.ds(l, n_grp, stride=CMP_STRIDE), :]
            a = jnp.dot((x + pe_ref[l:l + 1, :]).astype(BF16), w1_ref[l], preferred_element_type=F32)
            b = jnp.dot((x + pe_ref[CMP_STRIDE + l:CMP_STRIDE + l + 1, :]).astype(BF16),
                        w1_ref[CMP_STRIDE + l], preferred_element_type=F32)
            first = a if first is None else first + a
            second = b if second is None else second + b
        pre = first + pltpu.roll(second, n_grp - 1, 0) + b1_ref[...]
        hid = _gelu_tanh(pre).astype(BF16)
        res = jnp.dot(hid, w2_ref[...], preferred_element_type=F32)
        row = lax.broadcasted_iota(jnp.int32, res.shape, 0)
        res = jnp.where(row < n_grp - 1, res, 0.0)
        if transposed:
            o_ref[...] = res.T.astype(BF16)
        else:
            o_ref[...] = res.astype(BF16)

    one(kc_ref, pek_ref, w1k_ref, b1k_ref, w2k_ref, ko_ref, False)
    one(vc_ref, pev_ref, w1v_ref, b1v_ref, w2v_ref, vo_ref, True)


def _compress(kc, vc, wk, wv, bsz, seq):
    n_grp = seq // CMP_STRIDE
    blk = pl.BlockSpec((seq, LANE), lambda b: (b, 0))
    wspecs = [_const_spec((CMP_LEN, LANE)), _const_spec((CMP_LEN, LANE, 2 * CMP_HIDDEN)),
              _const_spec((1, 2 * CMP_HIDDEN)), _const_spec((2 * CMP_HIDDEN, LANE))]
    return pl.pallas_call(
        functools.partial(_cmp_body, seq=seq),
        grid=(bsz,),
        in_specs=[blk, blk] + wspecs + wspecs,
        out_specs=[pl.BlockSpec((None, n_grp, LANE), lambda b: (b, 0, 0)),
                   pl.BlockSpec((None, LANE, n_grp), lambda b: (b, 0, 0))],
        out_shape=[jax.ShapeDtypeStruct((bsz, n_grp, LANE), BF16),
                   jax.ShapeDtypeStruct((bsz, LANE, n_grp), BF16)],
        compiler_params=_params(("parallel",)),
        name="compress",
    )(kc, vc, *wk, *wv)


def _nsa_body(qr_ref, qp_ref, kc_ref, vc_ref, ks_ref, vs_ref, kw_ref, vw_ref, ng_ref, mcs_ref, oh_ref,
              o_ref, imp_ref, cnt_ref, qa_ref, m_ref, l_ref, acc_ref, res_ref, *, n_top):
    i = pl.program_id(1)
    n_grp = kc_ref.shape[0]
    n_sel = mcs_ref.shape[0]
    t0 = i * TQ
    gates = jax.nn.sigmoid(ng_ref[...])
    lane_head = lax.broadcasted_iota(jnp.int32, (TK, LANE), 1) // NSA_DH
    rel0 = (lax.broadcasted_iota(jnp.int32, (TK, TQ), 0)
            - lax.broadcasted_iota(jnp.int32, (TK, TQ), 1))

    def gate(g, kq, idx):
        r = (g * NSA_GROUP + kq) * N_GATES + idx
        return gates[r:r + 1, :]

    def reset():
        m_ref[...] = jnp.full_like(m_ref, NEG_INF)
        l_ref[...] = jnp.zeros_like(l_ref)
        acc_ref[...] = jnp.zeros_like(acc_ref)

    def scores_of(g, task):
        br, k_ref, v_ref, other, j, kind, arg = task
        rows = pl.ds(pl.multiple_of(j * TK, TK), TK)
        k_aug = jnp.where(lane_head == g, k_ref[rows, :], other(rows))
        return [jnp.dot(k_aug, qa_ref[kq], preferred_element_type=F32) for kq in range(NSA_GROUP)]

    def softmax_pv(g, task, scores):
        br, k_ref, v_ref, other, j, kind, arg = task
        rows = pl.ds(pl.multiple_of(j * TK, TK), TK)
        v_t = v_ref[:, rows]
        probs, alphas = [], []
        for kq in range(NSA_GROUP):
            s = scores[kq]
            if kind == "causal":
                s = jnp.where(rel0 + arg <= 0, s, NEG_INF)
            elif kind == "lower":
                s = jnp.where(rel0 > arg, s, NEG_INF)
            elif kind == "bias":
                s = s + arg
            m_prev = m_ref[br, kq]
            m_next = jnp.maximum(m_prev, jnp.max(s, axis=0, keepdims=True))
            alpha = jnp.exp(m_prev - m_next)
            p = jnp.exp(s - m_next)
            l_ref[br, kq] = alpha * l_ref[br, kq] + jnp.sum(p, axis=0, keepdims=True)
            m_ref[br, kq] = m_next
            probs.append(p.astype(BF16))
            alphas.append(alpha)
        for kq in range(NSA_GROUP):
            acc_ref[br, kq] = alphas[kq] * acc_ref[br, kq] + jnp.dot(
                v_t, probs[kq], preferred_element_type=F32)

    def run_tasks(g, tasks):
        pending = scores_of(g, tasks[0])
        for n, task in enumerate(tasks):
            following = scores_of(g, tasks[n + 1]) if n + 1 < len(tasks) else None
            softmax_pv(g, task, pending)
            pending = following

    def finalize(g, br, idx):
        for kq in range(NSA_GROUP):
            hh = g * NSA_GROUP + kq
            dst = slice(hh * NSA_DH, (hh + 1) * NSA_DH)
            o = acc_ref[br, kq, g * NSA_DH:(g + 1) * NSA_DH, :] / l_ref[br, kq]
            res_ref[dst, :] += o * gate(g, kq, idx)

    c_idx = lax.broadcasted_iota(jnp.int32, (n_grp, TQ), 0)
    t_cmp = t0 + lax.broadcasted_iota(jnp.int32, (n_grp, TQ), 1)
    cmp_mask = (c_idx * CMP_STRIDE + (CMP_LEN - 1)) <= t_cmp
    blk = lax.broadcasted_iota(jnp.int32, (n_sel, TQ), 0)
    cur = jnp.right_shift(t0 + lax.broadcasted_iota(jnp.int32, (n_sel, TQ), 1), 6)

    for g in range(NSA_KV_HEADS):
        head_rows = slice(g * NSA_DH, (g + 1) * NSA_DH)
        kc = kc_ref[:, head_rows.start:head_rows.stop]
        vc_t = vc_ref[head_rows, :]
        heads = [g * NSA_GROUP + kq for kq in range(NSA_GROUP)]
        cmp_scores = [jnp.dot(kc, qr_ref[hh * NSA_DH:(hh + 1) * NSA_DH, :], preferred_element_type=F32)
                      for hh in heads]
        psum = jnp.zeros((n_grp, TQ), F32)
        cmp_probs = []
        for s in cmp_scores:
            s = jnp.where(cmp_mask, s, NEG_INF)
            m = jnp.max(s, axis=0, keepdims=True)
            e = jnp.where(cmp_mask, jnp.exp(s - m), 0.0)
            l = jnp.sum(e, axis=0, keepdims=True)
            p = e / jnp.where(l > 0.0, l, 1.0)
            psum = psum + p
            cmp_probs.append(p.astype(BF16))
        for kq, hh in enumerate(heads):
            o = jnp.dot(vc_t, cmp_probs[kq], preferred_element_type=F32)
            res_ref[hh * NSA_DH:(hh + 1) * NSA_DH, :] = o * gate(g, kq, 0)

        hi = psum.astype(BF16)
        lo = (psum - hi.astype(F32)).astype(BF16)
        mcs = mcs_ref[...]
        imp = jnp.dot(mcs, hi, preferred_element_type=F32) + jnp.dot(mcs, lo, preferred_element_type=F32)
        valid = blk <= cur
        forced = (blk == 0) | (blk == cur) | (blk == cur - 1)
        imp = jnp.where(valid, imp + jnp.where(forced, FORCE_BONUS, 0.0), NEG_INF)
        imp_ref[...] = imp
        cnt_ref[...] = jnp.zeros_like(cnt_ref)

        per_tile = TQ // SEL_LEN
        for rnd in range(n_sel // per_tile):
            @pl.when(rnd <= i)
            def _(rnd=rnd):
                sub = 8
                contenders = [imp_ref[kk:kk + 1, :] for kk in range(rnd * per_tile, (rnd + 1) * per_tile)]
                for jg in range(n_sel // sub):
                    rows_j = slice(jg * sub, (jg + 1) * sub)
                    grp = imp_ref[rows_j, :]
                    cnt = cnt_ref[rows_j, :]
                    for off, row in enumerate(contenders):
                        kk = rnd * per_tile + off
                        if kk < jg * sub:
                            beats = row >= grp
                        elif kk >= (jg + 1) * sub:
                            beats = row > grp
                        else:
                            idx = jg * sub + lax.broadcasted_iota(jnp.int32, (sub, TQ), 0)
                            beats = (row > grp) | ((row == grp) & (idx > kk))
                        cnt = cnt + jnp.where(beats, 1.0, 0.0)
                    cnt_ref[rows_j, :] = cnt

        bias = jnp.where(cnt_ref[...] < n_top, 0.0, NEG_INF).astype(BF16)
        pad = jnp.zeros((NSA_DH - n_sel, TQ), BF16)
        for kq in range(NSA_GROUP):
            hh = g * NSA_GROUP + kq
            q_t = qp_ref[hh * NSA_DH:(hh + 1) * NSA_DH, :]
            parts = [q_t, bias, pad] if g == 0 else [bias, pad, q_t]
            qa_ref[kq] = jnp.concatenate([x for x in parts if x.shape[0] > 0], axis=0)

        reset()
        one_hot = lambda rows: oh_ref[rows, :]
        zeros = lambda rows: jnp.zeros((TK, LANE), BF16)

        def sel_pair(jj, carry, g=g):
            run_tasks(g, [(0, ks_ref, vs_ref, one_hot, 2 * jj, None, None),
                          (0, ks_ref, vs_ref, one_hot, 2 * jj + 1, None, None)])
            return carry

        lax.fori_loop(0, i // 2, sel_pair, 0)
        odd = (i % 2) == 1
        run_tasks(g, [
            (0, ks_ref, vs_ref, one_hot, jnp.maximum(i - 1, 0), "bias", jnp.where(odd, 0.0, NEG_INF)),
            (0, ks_ref, vs_ref, one_hot, i, "causal", 0),
            (1, kw_ref, vw_ref, zeros, jnp.maximum(i - 2, 0), "lower", jnp.where(i >= 2, 0, 2 ** 30)),
            (1, kw_ref, vw_ref, zeros, jnp.maximum(i - 1, 0), "bias", jnp.where(i >= 1, 0.0, NEG_INF)),
            (1, kw_ref, vw_ref, zeros, i, "causal", 0),
        ])
        finalize(g, 0, 1)
        finalize(g, 1, 2)

    o_ref[...] = res_ref[...].T.astype(BF16)


def _nsa(qr_t, qp_t, kcmp, vcmp_t, ks, vs_t, kw, vw_t, ng_t, mcs_t, onehot, bsz, seq):
    assert WINDOW == 2 * TK and TQ == TK and TQ % SEL_LEN == 0
    t = bsz * seq
    nq = seq // TQ
    n_grp = seq // CMP_STRIDE
    n_sel = seq // SEL_LEN
    assert n_sel <= NSA_DH and n_sel % 16 == 0
    width = NSA_HEADS * NSA_DH
    q_t = pl.BlockSpec((None, width, TQ), lambda b, i: (b, 0, i))
    kv = pl.BlockSpec((seq, LANE), lambda b, i: (b, 0))
    kv_t = pl.BlockSpec((None, LANE, seq), lambda b, i: (b, 0, 0))
    return pl.pallas_call(
        functools.partial(_nsa_body, n_top=min(SEL_TOP, n_sel)),
        grid=(bsz, nq),
        in_specs=[q_t, q_t,
                  pl.BlockSpec((None, n_grp, LANE), lambda b, i: (b, 0, 0)),
                  pl.BlockSpec((None, LANE, n_grp), lambda b, i: (b, 0, 0)),
                  kv, kv_t, kv, kv_t,
                  pl.BlockSpec((None, GATE_ROWS, TQ), lambda b, i: (b, 0, i)),
                  _const_spec((n_sel, n_grp)), _const_spec((seq, LANE))],
        out_specs=pl.BlockSpec((TQ, width), lambda b, i: (b * nq + i, 0)),
        out_shape=jax.ShapeDtypeStruct((t, width), BF16),
        scratch_shapes=[pltpu.VMEM((n_sel, TQ), F32), pltpu.VMEM((n_sel, TQ), F32),
                        pltpu.VMEM((NSA_GROUP, LANE, TQ), BF16),
                        pltpu.VMEM((2, NSA_GROUP, 1, TQ), F32), pltpu.VMEM((2, NSA_GROUP, 1, TQ), F32),
                        pltpu.VMEM((2, NSA_GROUP, LANE, TQ), F32), pltpu.VMEM((width, TQ), F32)],
        compiler_params=_params(("parallel", "parallel")),
        name="nsa",
    )(qr_t, qp_t, kcmp, vcmp_t, ks, vs_t, kw, vw_t, ng_t, mcs_t, onehot)


def _rope_tables(seq, rot_dim, theta, head_dim):
    half = rot_dim // 2
    inv_freq = theta ** (-2.0 * jnp.arange(half, dtype=F32) / rot_dim)
    ang = jnp.arange(seq, dtype=F32)[:, None] * inv_freq[None, :]
    cos, sin = jnp.cos(ang), jnp.sin(ang)
    rest = head_dim - rot_dim
    ones = jnp.ones((seq, rest), F32)
    zr = jnp.zeros((seq, rest), F32)
    zh = jnp.zeros((seq, half), F32)
    cos_t = jnp.concatenate([cos, cos, ones], axis=1)
    sin_a = jnp.concatenate([-sin, zh, zr], axis=1)
    sin_b = jnp.concatenate([zh, sin, zr], axis=1)
    rep = LANE // head_dim
    return [jnp.tile(a, (1, rep)) for a in (cos_t, sin_a, sin_b)]


def _retention_tables():
    c = RET_CHUNK
    gamma = 1.0 - 2.0 ** (-5.0 - jnp.arange(RET_HEADS, dtype=F32))
    log_g = jnp.log(gamma)
    j = jnp.arange(c, dtype=F32)
    diff = j[:, None] - j[None, :]
    dmat = jnp.where(diff >= 0, jnp.exp(log_g[:, None, None] * jnp.maximum(diff, 0.0)), 0.0)
    xi = jnp.exp(log_g[:, None] * (j[None, :] + 1.0))[:, :, None]
    zeta = jnp.exp(log_g[:, None] * (c - 1.0 - j[None, :]))[:, :, None]
    xi = jnp.broadcast_to(xi, (RET_HEADS, c, RET_DK))
    zeta = jnp.broadcast_to(zeta, (RET_HEADS, c, RET_DK))
    g_chunk = jnp.exp(log_g * c)[:, None, None]
    return [dmat, xi, zeta, g_chunk]


def _cmp_to_sel_t(seq):
    n_grp = seq // CMP_STRIDE
    n_sel = seq // SEL_LEN
    c_start = np.arange(n_grp) * CMP_STRIDE
    s_start = np.arange(n_sel) * SEL_LEN
    overlap = (np.minimum(c_start[None, :] + CMP_LEN, s_start[:, None] + SEL_LEN)
               - np.maximum(c_start[None, :], s_start[:, None]))
    m = np.clip(overlap, 0, None) / CMP_LEN
    m[:, n_grp - 1] = 0.0
    return jnp.asarray(m, dtype=BF16)


def _block_onehot(seq):
    n_sel = seq // SEL_LEN
    e = (np.arange(seq) // SEL_LEN)[:, None] == np.arange(NSA_DH)[None, :]
    return jnp.asarray(np.concatenate([e, e], axis=1), dtype=BF16)


def _pack_w_in(w_in):
    pad = jnp.zeros((D_MODEL, D_CAT - w_in.shape[1]), w_in.dtype)
    return jnp.concatenate([w_in, pad], axis=1).astype(BF16)


def _pack_cmp(pe, w1, b1, w2):
    pe2 = pe.reshape(CMP_LEN, NSA_KV_HEADS * NSA_DH)
    w1l = w1.reshape(CMP_LEN, NSA_DH, CMP_HIDDEN)
    z1 = jnp.zeros_like(w1l)
    w1bd = jnp.concatenate([jnp.concatenate([w1l, z1], axis=2),
                            jnp.concatenate([z1, w1l], axis=2)], axis=1).astype(BF16)
    b1bd = jnp.concatenate([b1, b1])[None, :]
    z2 = jnp.zeros_like(w2)
    w2bd = jnp.concatenate([jnp.concatenate([w2, z2], axis=1),
                            jnp.concatenate([z2, w2], axis=1)], axis=0).astype(BF16)
    return [pe2, w1bd, b1bd, w2bd]


def kernel(x, ffn1_norm_w, ffn1_w_gate, ffn1_w_up, ffn1_w_down, mix_norm_w, w_in, ret_norm_w, cmp_pe_k, cmp_k_w1, cmp_k_b1, cmp_k_w2, cmp_pe_v, cmp_v_w1, cmp_v_b1, cmp_v_w2, w_out, ffn2_norm_w, ffn2_w_gate, ffn2_w_up, ffn2_w_down, final_norm_w):
    bsz, seq, d = x.shape
    assert d == D_MODEL and seq % TM == 0 and seq % TQ == 0 and seq >= WINDOW
    assert ffn1_norm_w.shape[0] == 1, "the final norm is fused into the (single) layer's second FFN"
    t = bsz * seq
    h = x.reshape(t, d)
    n_layers = ffn1_norm_w.shape[0]
    ret_tabs = _retention_tables()
    ret_rope = _rope_tables(seq, RET_DK, RET_THETA, RET_DK)
    nsa_rope = _rope_tables(seq, ROPE_DIM, ROPE_THETA, NSA_DH)
    mcs_t = _cmp_to_sel_t(seq)
    onehot = _block_onehot(seq)
    for layer in range(n_layers):
        h1 = _ffn1(h, ffn1_norm_w[layer][None, :], ffn1_w_gate[layer].astype(BF16),
                   ffn1_w_up[layer].astype(BF16), ffn1_w_down[layer].astype(BF16))
        (rq, rk, rv, rg, qr_t, qp_t, kc, vc, ks, vs_t, kw, vw_t, ng_t) = _proj(
            h1, mix_norm_w[layer][None, :], _pack_w_in(w_in[layer]), ret_rope + nsa_rope, seq)
        ret = _retention(rq, rk, rv, rg, ret_tabs, ret_norm_w[layer][None, :], bsz, seq)
        kcmp, vcmp_t = _compress(
            kc, vc,
            _pack_cmp(cmp_pe_k[layer], cmp_k_w1[layer], cmp_k_b1[layer], cmp_k_w2[layer]),
            _pack_cmp(cmp_pe_v[layer], cmp_v_w1[layer], cmp_v_b1[layer], cmp_v_w2[layer]),
            bsz, seq)
        nsa = _nsa(qr_t, qp_t, kcmp, vcmp_t, ks, vs_t, kw, vw_t, ng_t, mcs_t, onehot, bsz, seq)
        h = _ffn2(h1, ret, nsa, w_out[layer].astype(BF16), ffn2_norm_w[layer][None, :],
                  ffn2_w_gate[layer].astype(BF16), ffn2_w_up[layer].astype(BF16),
                  ffn2_w_down[layer].astype(BF16), final_norm_w[None, :])
    return h.reshape(bsz, seq, d)
```

```python
import functools

import numpy as np
import jax
import jax.numpy as jnp
from jax import lax
from jax.experimental import pallas as pl
from jax.experimental.pallas import tpu as pltpu

F32 = jnp.float32
BF16 = jnp.bfloat16

D_MODEL = 1024
D_FF = 2816
RMS_EPS = 1e-6
GN_EPS = 1e-5

RET_HEADS = 4
RET_DK = 64
RET_DV = 128
RET_THETA = 10000.0

NSA_HEADS = 8
NSA_KV_HEADS = 2
NSA_GROUP = 4
NSA_DH = 64
CMP_LEN = 32
CMP_STRIDE = 16
CMP_HIDDEN = 256
SEL_LEN = 64
SEL_TOP = 16
WINDOW = 512
N_GATES = 3
ROPE_THETA = 500000.0
ROPE_DIM = 16

NEG_INF = -1e30
LOG2E = 1.4426950408889634
FORCE_BONUS = 1e4

LANE = 128
VMEM_LIMIT = 56 * 1024 * 1024

TM = 512
RET_CHUNK = 128
TQ = 256
TK = 256
SEL_UNROLL = 4
LOOP_LOOKAHEAD = 6
ACC_ROWS = 80
FF_SPLIT = (1280, 1536)

SEG = dict(rq=(0, 256), rk=(256, 512), rv=(512, 1024), rg=(1024, 1536), nq=(1536, 2048),
           kvc=(2048, 2304), kvs=(2304, 2560), kvw=(2560, 2816), ng=(2816, 2944))
D_CAT = 2944
GATE_ROWS = 32


def _rms(x, w):
    return x * lax.rsqrt(jnp.mean(x * x, axis=-1, keepdims=True) + RMS_EPS) * w


def _params(sem):
    return pltpu.CompilerParams(dimension_semantics=sem, vmem_limit_bytes=VMEM_LIMIT)


def _const_spec(shape):
    nd = len(shape)
    return pl.BlockSpec(shape, lambda *_: (0,) * nd, pipeline_mode=pl.Buffered(1))


def _swiglu_half_step(h, nw, wg_ref, wu_ref, wd_ref):
    hn = _rms(h, nw).astype(BF16)
    acc = None
    off = 0
    for width in FF_SPLIT:
        g = jnp.dot(hn, wg_ref[:, off:off + width], preferred_element_type=F32)
        u = jnp.dot(hn, wu_ref[:, off:off + width], preferred_element_type=F32)
        a = (g * jax.nn.sigmoid(g) * u).astype(BF16)
        y = jnp.dot(a, wd_ref[off:off + width, :], preferred_element_type=F32)
        acc = y if acc is None else acc + y
        off += width
    return h + 0.5 * acc


def _ffn1_body(x_ref, nw_ref, wg_ref, wu_ref, wd_ref, o_ref):
    o_ref[...] = _swiglu_half_step(x_ref[...], nw_ref[...], wg_ref, wu_ref, wd_ref)


def _ffn1(x2, nw, wg, wu, wd):
    t = x2.shape[0]
    tok = pl.BlockSpec((TM, D_MODEL), lambda i: (i, 0))
    return pl.pallas_call(
        _ffn1_body,
        grid=(t // TM,),
        in_specs=[tok, _const_spec((1, D_MODEL)), _const_spec((D_MODEL, D_FF)),
                  _const_spec((D_MODEL, D_FF)), _const_spec((D_FF, D_MODEL))],
        out_specs=tok,
        out_shape=jax.ShapeDtypeStruct((t, D_MODEL), F32),
        compiler_params=_params(("parallel",)),
        name="ffn1",
    )(x2, nw, wg, wu, wd)


def _ffn2_body(h_ref, ret_ref, nsa_ref, wo_ref, nw_ref, wg_ref, wu_ref, wd_ref, fw_ref, o_ref):
    half = RET_HEADS * RET_DV
    mix = (jnp.dot(ret_ref[...], wo_ref[:half, :], preferred_element_type=F32)
           + jnp.dot(nsa_ref[...], wo_ref[half:, :], preferred_element_type=F32))
    h = h_ref[...] + mix
    h = _swiglu_half_step(h, nw_ref[...], wg_ref, wu_ref, wd_ref)
    o_ref[...] = _rms(h, fw_ref[...])


def _ffn2(h1, ret, nsa, wo, nw, wg, wu, wd, fw):
    t = h1.shape[0]
    tok = pl.BlockSpec((TM, D_MODEL), lambda i: (i, 0))
    half = pl.BlockSpec((TM, D_MODEL // 2), lambda i: (i, 0))
    return pl.pallas_call(
        _ffn2_body,
        grid=(t // TM,),
        in_specs=[tok, half, half, _const_spec((D_MODEL, D_MODEL)), _const_spec((1, D_MODEL)),
                  _const_spec((D_MODEL, D_FF)), _const_spec((D_MODEL, D_FF)),
                  _const_spec((D_FF, D_MODEL)), _const_spec((1, D_MODEL))],
        out_specs=tok,
        out_shape=jax.ShapeDtypeStruct((t, D_MODEL), F32),
        compiler_params=_params(("parallel",)),
        name="ffn2",
    )(h1, ret, nsa, wo, nw, wg, wu, wd, fw)


def _rope128(x, cos, sin_a, sin_b, half):
    return (x * cos + pltpu.roll(x, LANE - half, 1) * sin_a + pltpu.roll(x, half, 1) * sin_b)


def _proj_body(h_ref, nw_ref, w_ref, rc_ref, ra_ref, rb_ref, nc_ref, na_ref, nb_ref,
               rq_ref, rk_ref, rv_ref, rg_ref, qr_ref, qp_ref, kc_ref, vc_ref,
               ks_ref, vs_ref, kw_ref, vw_ref, ng_ref):
    u = _rms(h_ref[...], nw_ref[...]).astype(BF16)

    def seg(name):
        a, b = SEG[name]
        return jnp.dot(u, w_ref[:, a:b], preferred_element_type=F32)

    rc, ra, rb = rc_ref[...], ra_ref[...], rb_ref[...]
    nc, na, nb = nc_ref[...], na_ref[...], nb_ref[...]
    ret_half = RET_DK // 2
    nsa_half = ROPE_DIM // 2
    nsa_scale = NSA_DH ** -0.5 * LOG2E

    p = seg("rq")
    for j in range(2):
        sl = slice(j * LANE, (j + 1) * LANE)
        rq_ref[:, sl] = _rope128(p[:, sl], rc, ra, rb, ret_half)
    p = seg("rk")
    for j in range(2):
        sl = slice(j * LANE, (j + 1) * LANE)
        rk_ref[:, sl] = _rope128(p[:, sl], rc, ra, rb, ret_half) * (RET_DK ** -0.5)
    rv_ref[...] = seg("rv").astype(BF16)
    rg_ref[...] = seg("rg")
    p = seg("nq")
    for j in range(4):
        sl = slice(j * LANE, (j + 1) * LANE)
        qr_ref[sl, :] = (p[:, sl] * nsa_scale).T.astype(BF16)
        qp_ref[sl, :] = (_rope128(p[:, sl], nc, na, nb, nsa_half) * nsa_scale).T.astype(BF16)
    p = seg("kvc")
    kc_ref[...] = p[:, :LANE]
    vc_ref[...] = p[:, LANE:]
    p = seg("kvs")
    ks_ref[...] = _rope128(p[:, :LANE], nc, na, nb, nsa_half).astype(BF16)
    vs_ref[...] = p[:, LANE:].T.astype(BF16)
    p = seg("kvw")
    kw_ref[...] = _rope128(p[:, :LANE], nc, na, nb, nsa_half).astype(BF16)
    vw_ref[...] = p[:, LANE:].T.astype(BF16)
    ng_ref[...] = seg("ng").T[:GATE_ROWS, :]


def _proj(h1, nw, wcat, tabs, seq):
    t = h1.shape[0]
    bsz = t // seq
    per_seq = seq // TM

    def tok(width, dtype):
        return pl.BlockSpec((TM, width), lambda i: (i, 0)), jax.ShapeDtypeStruct((t, width), dtype)

    def tr(rows, dtype):
        return (pl.BlockSpec((None, rows, TM), lambda i: (i // per_seq, 0, i % per_seq)),
                jax.ShapeDtypeStruct((bsz, rows, seq), dtype))

    tab = pl.BlockSpec((TM, LANE), lambda i: (i % per_seq, 0))
    width = NSA_HEADS * NSA_DH
    outs = [tok(256, F32), tok(256, F32), tok(512, BF16), tok(512, F32),
            tr(width, BF16), tr(width, BF16),
            tok(LANE, F32), tok(LANE, F32),
            tok(LANE, BF16), tr(LANE, BF16), tok(LANE, BF16), tr(LANE, BF16),
            tr(GATE_ROWS, F32)]
    return pl.pallas_call(
        _proj_body,
        grid=(t // TM,),
        in_specs=[tok(D_MODEL, F32)[0], _const_spec((1, D_MODEL)), _const_spec((D_MODEL, D_CAT))] + [tab] * 6,
        out_specs=[o[0] for o in outs],
        out_shape=[o[1] for o in outs],
        compiler_params=_params(("parallel",)),
        name="proj",
    )(h1, nw, wcat, *tabs)


def _ret_body(q_ref, k_ref, v_ref, g_ref, dm_ref, xi_ref, ze_ref, gc_ref, nw_ref, o_ref, st_ref, *, seq):
    c = RET_CHUNK
    st_ref[...] = jnp.zeros_like(st_ref)

    def chunk(n, carry):
        rows = pl.ds(pl.multiple_of(n * c, c), c)
        heads = range(RET_HEADS)
        ks = [slice(h * RET_DK, (h + 1) * RET_DK) for h in heads]
        vs = [slice(h * RET_DV, (h + 1) * RET_DV) for h in heads]
        q = [q_ref[rows, ks[h]] for h in heads]
        k = [k_ref[rows, ks[h]] for h in heads]
        v = [v_ref[rows, vs[h]] for h in heads]
        state = [st_ref[h] for h in heads]
        inner = [lax.dot_general(q[h].astype(BF16), k[h].astype(BF16), (((1,), (1,)), ((), ())),
                                 preferred_element_type=F32) for h in heads]
        cross = [jnp.dot((q[h] * xi_ref[h]).astype(BF16), state[h].astype(BF16),
                         preferred_element_type=F32) for h in heads]
        upd = [lax.dot_general((k[h] * ze_ref[h]).astype(BF16), v[h], (((0,), (0,)), ((), ())),
                               preferred_element_type=F32) for h in heads]
        for h in heads:
            st_ref[h] = gc_ref[h] * state[h] + upd[h]
        for h in heads:
            out = jnp.dot((inner[h] * dm_ref[h]).astype(BF16), v[h], preferred_element_type=F32) + cross[h]
            mu = jnp.mean(out, axis=-1, keepdims=True)
            d = out - mu
            var = jnp.mean(d * d, axis=-1, keepdims=True)
            y = d * lax.rsqrt(var + GN_EPS) * nw_ref[:, vs[h]]
            gate = g_ref[rows, vs[h]]
            o_ref[rows, vs[h]] = (y * (gate * jax.nn.sigmoid(gate))).astype(BF16)
        return carry

    lax.fori_loop(0, seq // c, chunk, 0, unroll=2)


def _retention(rq, rk, rv, rg, tabs, nw, bsz, seq):
    t = bsz * seq
    c = RET_CHUNK

    def seq_blk(width):
        return pl.BlockSpec((seq, width), lambda b: (b, 0))

    return pl.pallas_call(
        functools.partial(_ret_body, seq=seq),
        grid=(bsz,),
        in_specs=[seq_blk(256), seq_blk(256), seq_blk(512), seq_blk(512),
                  _const_spec((RET_HEADS, c, c)), _const_spec((RET_HEADS, c, RET_DK)),
                  _const_spec((RET_HEADS, c, RET_DK)), _const_spec((RET_HEADS, 1, 1)),
                  _const_spec((1, RET_HEADS * RET_DV))],
        out_specs=seq_blk(512),
        out_shape=jax.ShapeDtypeStruct((t, RET_HEADS * RET_DV), BF16),
        scratch_shapes=[pltpu.VMEM((RET_HEADS, RET_DK, RET_DV), F32)],
        compiler_params=_params(("parallel",)),
        name="retention",
    )(rq, rk, rv, rg, *tabs, nw)


def _gelu_tanh(x):
    return 0.5 * x * (1.0 + jnp.tanh(np.sqrt(2.0 / np.pi).astype(np.float32) * (x + 0.044715 * (x * x * x))))


def _cmp_body(kc_ref, vc_ref, pek_ref, w1k_ref, b1k_ref, w2k_ref, pev_ref, w1v_ref, b1v_ref, w2v_ref,
              ko_ref, vo_ref, *, seq):
    n_grp = seq // CMP_STRIDE

    def one(x_ref, pe_ref, w1_ref, b1_ref, w2_ref, o_ref, transposed):
        first = None
        second = None
        for l in range(CMP_STRIDE):
            x = x_ref[pl.ds(l, n_grp, stride=CMP_STRIDE), :]
            a = jnp.dot((x + pe_ref[l:l + 1, :]).astype(BF16), w1_ref[l], preferred_element_type=F32)
            b = jnp.dot((x + pe_ref[CMP_STRIDE + l:CMP_STRIDE + l + 1, :]).astype(BF16),
                        w1_ref[CMP_STRIDE + l], preferred_element_type=F32)
            first = a if first is None else first + a
            second = b if second is None else second + b
        pre = first + pltpu.roll(second, n_grp - 1, 0) + b1_ref[...]
        hid = _gelu_tanh(pre).astype(BF16)
        res = jnp.dot(hid, w2_ref[...], preferred_element_type=F32)
        row = lax.broadcasted_iota(jnp.int32, res.shape, 0)
        res = jnp.where(row < n_grp - 1, res, 0.0)
        if transposed:
            o_ref[...] = res.T.astype(BF16)
        else:
            o_ref[...] = res.astype(BF16)

    one(kc_ref, pek_ref, w1k_ref, b1k_ref, w2k_ref, ko_ref, False)
    one(vc_ref, pev_ref, w1v_ref, b1v_ref, w2v_ref, vo_ref, True)


def _compress(kc, vc, wk, wv, bsz, seq):
    n_grp = seq // CMP_STRIDE
    blk = pl.BlockSpec((seq, LANE), lambda b: (b, 0))
    wspecs = [_const_spec((CMP_LEN, LANE)), _const_spec((CMP_LEN, LANE, 2 * CMP_HIDDEN)),
              _const_spec((1, 2 * CMP_HIDDEN)), _const_spec((2 * CMP_HIDDEN, LANE))]
    return pl.pallas_call(
        functools.partial(_cmp_body, seq=seq),
        grid=(bsz,),
        in_specs=[blk, blk] + wspecs + wspecs,
        out_specs=[pl.BlockSpec((None, n_grp, LANE), lambda b: (b, 0, 0)),
                   pl.BlockSpec((None, LANE, n_grp), lambda b: (b, 0, 0))],
        out_shape=[jax.ShapeDtypeStruct((bsz, n_grp, LANE), BF16),
                   jax.ShapeDtypeStruct((bsz, LANE, n_grp), BF16)],
        compiler_params=_params(("parallel",)),
        name="compress",
    )(kc, vc, *wk, *wv)


def _nsa_body(qr_ref, qp_ref, kc_ref, vc_ref, ks_ref, vs_ref, kw_ref, vw_ref, ng_ref, mcs_ref, oh_ref,
              o_ref, imp_ref, cnt_ref, qa_ref, m_ref, acc_ref, res_ref, *, n_top):
    i = pl.program_id(1)
    n_grp = kc_ref.shape[0]
    n_sel = mcs_ref.shape[0]
    t0 = i * TQ
    gates = jax.nn.sigmoid(ng_ref[...])
    lane_head = lax.broadcasted_iota(jnp.int32, (TK, LANE), 1) // NSA_DH
    rel0 = (lax.broadcasted_iota(jnp.int32, (TK, TQ), 0)
            - lax.broadcasted_iota(jnp.int32, (TK, TQ), 1))

    def gate(g, kq, idx):
        r = (g * NSA_GROUP + kq) * N_GATES + idx
        return gates[r:r + 1, :]

    def reset():
        m_ref[...] = jnp.full_like(m_ref, NEG_INF)
        acc_ref[...] = jnp.zeros_like(acc_ref)

    def load_keys(g, task):
        br, k_ref, v_ref, other, j, kind, arg = task
        rows = pl.ds(pl.multiple_of(j * TK, TK), TK)
        k_aug = jnp.where(lane_head == g, k_ref[rows, :], other(rows))
        v_t = jnp.concatenate([v_ref[g * NSA_DH:(g + 1) * NSA_DH, rows],
                               jnp.ones((ACC_ROWS - NSA_DH, TK), BF16)], axis=0)
        return k_aug, v_t

    def softmax(task, kq, s):
        br, k_ref, v_ref, other, j, kind, arg = task
        if kind == "causal":
            s = jnp.where(rel0 <= 0, s, NEG_INF)
        elif kind == "lower":
            s = jnp.where(rel0 > arg, s, NEG_INF)
        elif kind == "bias":
            s = s + arg
        m_prev = m_ref[br, kq]
        m_next = jnp.maximum(m_prev, jnp.max(s, axis=0, keepdims=True))
        m_ref[br, kq] = m_next
        return jnp.exp2(s - m_next).astype(BF16), jnp.exp2(m_prev - m_next)

    def run_tasks(g, tasks, lookahead, pv_group):
        operands = [load_keys(g, task) for task in tasks]
        units = [(n, kq) for n in range(len(tasks)) for kq in range(NSA_GROUP)]
        score = lambda u: jnp.dot(operands[u[0]][0], qa_ref[u[1]], preferred_element_type=F32)
        pending = [score(u) for u in units[:lookahead]]
        done = []
        for pos, (n, kq) in enumerate(units):
            if pos + lookahead < len(units):
                pending.append(score(units[pos + lookahead]))
            done.append((n, kq) + softmax(tasks[n], kq, pending.pop(0)))
            if len(done) == pv_group:
                for n2, kq2, p, alpha in done:
                    br = tasks[n2][0]
                    acc_ref[br, kq2] = alpha * acc_ref[br, kq2] + jnp.dot(
                        operands[n2][1], p, preferred_element_type=F32)
                done = []

    def finalize(g, br, idx):
        for kq in range(NSA_GROUP):
            hh = g * NSA_GROUP + kq
            dst = slice(hh * NSA_DH, (hh + 1) * NSA_DH)
            o = acc_ref[br, kq, 0:NSA_DH, :] / acc_ref[br, kq, NSA_DH:NSA_DH + 1, :]
            res_ref[dst, :] += o * gate(g, kq, idx)

    c_idx = lax.broadcasted_iota(jnp.int32, (n_grp, TQ), 0)
    t_cmp = t0 + lax.broadcasted_iota(jnp.int32, (n_grp, TQ), 1)
    cmp_mask = (c_idx * CMP_STRIDE + (CMP_LEN - 1)) <= t_cmp
    blk = lax.broadcasted_iota(jnp.int32, (n_sel, TQ), 0)
    cur = jnp.right_shift(t0 + lax.broadcasted_iota(jnp.int32, (n_sel, TQ), 1), 6)

    for g in range(NSA_KV_HEADS):
        head_rows = slice(g * NSA_DH, (g + 1) * NSA_DH)
        kc = kc_ref[:, head_rows.start:head_rows.stop]
        vc_t = vc_ref[head_rows, :]
        heads = [g * NSA_GROUP + kq for kq in range(NSA_GROUP)]
        cmp_scores = [jnp.dot(kc, qr_ref[hh * NSA_DH:(hh + 1) * NSA_DH, :], preferred_element_type=F32)
                      for hh in heads]
        psum = jnp.zeros((n_grp, TQ), F32)
        cmp_probs = []
        for s in cmp_scores:
            s = jnp.where(cmp_mask, s, NEG_INF)
            m = jnp.max(s, axis=0, keepdims=True)
            e = jnp.exp2(s - m)
            l = jnp.sum(e, axis=0, keepdims=True)
            p = e * jnp.where(m > 0.5 * NEG_INF, 1.0 / l, 0.0)
            psum = psum + p
            cmp_probs.append(p.astype(BF16))
        for kq, hh in enumerate(heads):
            o = jnp.dot(vc_t, cmp_probs[kq], preferred_element_type=F32)
            res_ref[hh * NSA_DH:(hh + 1) * NSA_DH, :] = o * gate(g, kq, 0)

        hi = psum.astype(BF16)
        lo = (psum - hi.astype(F32)).astype(BF16)
        mcs = mcs_ref[...]
        imp = jnp.dot(mcs, hi, preferred_element_type=F32) + jnp.dot(mcs, lo, preferred_element_type=F32)
        valid = blk <= cur
        forced = (blk == 0) | (blk == cur) | (blk == cur - 1)
        imp = jnp.where(valid, imp + jnp.where(forced, FORCE_BONUS, 0.0), NEG_INF)
        imp_ref[...] = imp
        cnt_ref[...] = jnp.zeros_like(cnt_ref)

        per_tile = TQ // SEL_LEN
        for rnd in range(n_sel // per_tile):
            @pl.when(rnd <= i)
            def _(rnd=rnd):
                sub = 8
                contenders = [imp_ref[kk:kk + 1, :] for kk in range(rnd * per_tile, (rnd + 1) * per_tile)]
                for jg in range(n_sel // sub):
                    rows_j = slice(jg * sub, (jg + 1) * sub)
                    grp = imp_ref[rows_j, :]
                    cnt = cnt_ref[rows_j, :]
                    for off, row in enumerate(contenders):
                        kk = rnd * per_tile + off
                        if kk < jg * sub:
                            beats = row >= grp
                        elif kk >= (jg + 1) * sub:
                            beats = row > grp
                        else:
                            idx = jg * sub + lax.broadcasted_iota(jnp.int32, (sub, TQ), 0)
                            beats = (row > grp) | ((row == grp) & (idx > kk))
                        cnt = cnt + jnp.where(beats, 1.0, 0.0)
                    cnt_ref[rows_j, :] = cnt

        bias = jnp.where(cnt_ref[...] < n_top, 0.0, NEG_INF).astype(BF16)
        pad = jnp.zeros((NSA_DH - n_sel, TQ), BF16)
        for kq in range(NSA_GROUP):
            hh = g * NSA_GROUP + kq
            q_t = qp_ref[hh * NSA_DH:(hh + 1) * NSA_DH, :]
            parts = [q_t, bias, pad] if g == 0 else [bias, pad, q_t]
            qa_ref[kq] = jnp.concatenate([x for x in parts if x.shape[0] > 0], axis=0)

        reset()
        one_hot = lambda rows: oh_ref[rows, :]
        zeros = lambda rows: jnp.zeros((TK, LANE), BF16)

        def full_tiles(first, count):
            return [(0, ks_ref, vs_ref, one_hot, first + n, None, None) for n in range(count)]

        def sel_quad(jj, carry, g=g):
            run_tasks(g, full_tiles(SEL_UNROLL * jj, SEL_UNROLL), LOOP_LOOKAHEAD, 1)
            return carry

        lax.fori_loop(0, i // SEL_UNROLL, sel_quad, 0)

        for left in range(1, SEL_UNROLL):
            @pl.when(i % SEL_UNROLL == left)
            def _(g=g, left=left):
                run_tasks(g, full_tiles((i // SEL_UNROLL) * SEL_UNROLL, left), LOOP_LOOKAHEAD, 1)

        run_tasks(g, [
            (0, ks_ref, vs_ref, one_hot, i, "causal", None),
            (1, kw_ref, vw_ref, zeros, jnp.maximum(i - 2, 0), "lower", jnp.where(i >= 2, 0, 2 ** 30)),
            (1, kw_ref, vw_ref, zeros, jnp.maximum(i - 1, 0), "bias", jnp.where(i >= 1, 0.0, NEG_INF)),
            (1, kw_ref, vw_ref, zeros, i, "causal", None),
        ], NSA_GROUP, NSA_GROUP)
        finalize(g, 0, 1)
        finalize(g, 1, 2)

    o_ref[...] = res_ref[...].T.astype(BF16)


def _nsa(qr_t, qp_t, kcmp, vcmp_t, ks, vs_t, kw, vw_t, ng_t, mcs_t, onehot, bsz, seq):
    assert WINDOW == 2 * TK and TQ == TK and TQ % SEL_LEN == 0 and SEL_UNROLL == 4
    t = bsz * seq
    nq = seq // TQ
    n_grp = seq // CMP_STRIDE
    n_sel = seq // SEL_LEN
    assert n_sel <= NSA_DH and n_sel % 16 == 0
    width = NSA_HEADS * NSA_DH
    q_t = pl.BlockSpec((None, width, TQ), lambda b, i: (b, 0, i))
    kv = pl.BlockSpec((seq, LANE), lambda b, i: (b, 0))
    kv_t = pl.BlockSpec((None, LANE, seq), lambda b, i: (b, 0, 0))
    return pl.pallas_call(
        functools.partial(_nsa_body, n_top=min(SEL_TOP, n_sel)),
        grid=(bsz, nq),
        in_specs=[q_t, q_t,
                  pl.BlockSpec((None, n_grp, LANE), lambda b, i: (b, 0, 0)),
                  pl.BlockSpec((None, LANE, n_grp), lambda b, i: (b, 0, 0)),
                  kv, kv_t, kv, kv_t,
                  pl.BlockSpec((None, GATE_ROWS, TQ), lambda b, i: (b, 0, i)),
                  _const_spec((n_sel, n_grp)), _const_spec((seq, LANE))],
        out_specs=pl.BlockSpec((TQ, width), lambda b, i: (b * nq + i, 0)),
        out_shape=jax.ShapeDtypeStruct((t, width), BF16),
        scratch_shapes=[pltpu.VMEM((n_sel, TQ), F32), pltpu.VMEM((n_sel, TQ), F32),
                        pltpu.VMEM((NSA_GROUP, LANE, TQ), BF16),
                        pltpu.VMEM((2, NSA_GROUP, 1, TQ), F32),
                        pltpu.VMEM((2, NSA_GROUP, ACC_ROWS, TQ), F32), pltpu.VMEM((width, TQ), F32)],
        compiler_params=_params(("parallel", "parallel")),
        name="nsa",
    )(qr_t, qp_t, kcmp, vcmp_t, ks, vs_t, kw, vw_t, ng_t, mcs_t, onehot)


def _rope_tables(seq, rot_dim, theta, head_dim):
    half = rot_dim // 2
    inv_freq = theta ** (-2.0 * jnp.arange(half, dtype=F32) / rot_dim)
    ang = jnp.arange(seq, dtype=F32)[:, None] * inv_freq[None, :]
    cos, sin = jnp.cos(ang), jnp.sin(ang)
    rest = head_dim - rot_dim
    ones = jnp.ones((seq, rest), F32)
    zr = jnp.zeros((seq, rest), F32)
    zh = jnp.zeros((seq, half), F32)
    cos_t = jnp.concatenate([cos, cos, ones], axis=1)
    sin_a = jnp.concatenate([-sin, zh, zr], axis=1)
    sin_b = jnp.concatenate([zh, sin, zr], axis=1)
    rep = LANE // head_dim
    return [jnp.tile(a, (1, rep)) for a in (cos_t, sin_a, sin_b)]


def _retention_tables():
    c = RET_CHUNK
    gamma = 1.0 - 2.0 ** (-5.0 - jnp.arange(RET_HEADS, dtype=F32))
    log_g = jnp.log(gamma)
    j = jnp.arange(c, dtype=F32)
    diff = j[:, None] - j[None, :]
    dmat = jnp.where(diff >= 0, jnp.exp(log_g[:, None, None] * jnp.maximum(diff, 0.0)), 0.0)
    xi = jnp.exp(log_g[:, None] * (j[None, :] + 1.0))[:, :, None]
    zeta = jnp.exp(log_g[:, None] * (c - 1.0 - j[None, :]))[:, :, None]
    xi = jnp.broadcast_to(xi, (RET_HEADS, c, RET_DK))
    zeta = jnp.broadcast_to(zeta, (RET_HEADS, c, RET_DK))
    g_chunk = jnp.exp(log_g * c)[:, None, None]
    return [dmat, xi, zeta, g_chunk]


def _cmp_to_sel_t(seq):
    n_grp = seq // CMP_STRIDE
    n_sel = seq // SEL_LEN
    c_start = np.arange(n_grp) * CMP_STRIDE
    s_start = np.arange(n_sel) * SEL_LEN
    overlap = (np.minimum(c_start[None, :] + CMP_LEN, s_start[:, None] + SEL_LEN)
               - np.maximum(c_start[None, :], s_start[:, None]))
    m = np.clip(overlap, 0, None) / CMP_LEN
    m[:, n_grp - 1] = 0.0
    return jnp.asarray(m, dtype=BF16)


def _block_onehot(seq):
    n_sel = seq // SEL_LEN
    e = (np.arange(seq) // SEL_LEN)[:, None] == np.arange(NSA_DH)[None, :]
    return jnp.asarray(np.concatenate([e, e], axis=1), dtype=BF16)


def _pack_w_in(w_in):
    pad = jnp.zeros((D_MODEL, D_CAT - w_in.shape[1]), w_in.dtype)
    return jnp.concatenate([w_in, pad], axis=1).astype(BF16)


def _pack_cmp(pe, w1, b1, w2):
    pe2 = pe.reshape(CMP_LEN, NSA_KV_HEADS * NSA_DH)
    w1l = w1.reshape(CMP_LEN, NSA_DH, CMP_HIDDEN)
    z1 = jnp.zeros_like(w1l)
    w1bd = jnp.concatenate([jnp.concatenate([w1l, z1], axis=2),
                            jnp.concatenate([z1, w1l], axis=2)], axis=1).astype(BF16)
    b1bd = jnp.concatenate([b1, b1])[None, :]
    z2 = jnp.zeros_like(w2)
    w2bd = jnp.concatenate([jnp.concatenate([w2, z2], axis=1),
                            jnp.concatenate([z2, w2], axis=1)], axis=0).astype(BF16)
    return [pe2, w1bd, b1bd, w2bd]


def kernel(x, ffn1_norm_w, ffn1_w_gate, ffn1_w_up, ffn1_w_down, mix_norm_w, w_in, ret_norm_w, cmp_pe_k, cmp_k_w1, cmp_k_b1, cmp_k_w2, cmp_pe_v, cmp_v_w1, cmp_v_b1, cmp_v_w2, w_out, ffn2_norm_w, ffn2_w_gate, ffn2_w_up, ffn2_w_down, final_norm_w):
    bsz, seq, d = x.shape
    assert d == D_MODEL and seq % TM == 0 and seq % TQ == 0 and seq >= WINDOW
    assert ffn1_norm_w.shape[0] == 1, "the final norm is fused into the (single) layer's second FFN"
    t = bsz * seq
    h = x.reshape(t, d)
    n_layers = ffn1_norm_w.shape[0]
    ret_tabs = _retention_tables()
    ret_rope = _rope_tables(seq, RET_DK, RET_THETA, RET_DK)
    nsa_rope = _rope_tables(seq, ROPE_DIM, ROPE_THETA, NSA_DH)
    mcs_t = _cmp_to_sel_t(seq)
    onehot = _block_onehot(seq)
    for layer in range(n_layers):
        h1 = _ffn1(h, ffn1_norm_w[layer][None, :], ffn1_w_gate[layer].astype(BF16),
                   ffn1_w_up[layer].astype(BF16), ffn1_w_down[layer].astype(BF16))
        (rq, rk, rv, rg, qr_t, qp_t, kc, vc, ks, vs_t, kw, vw_t, ng_t) = _proj(
            h1, mix_norm_w[layer][None, :], _pack_w_in(w_in[layer]), ret_rope + nsa_rope, seq)
        ret = _retention(rq, rk, rv, rg, ret_tabs, ret_norm_w[layer][None, :], bsz, seq)
        kcmp, vcmp_t = _compress(
            kc, vc,
            _pack_cmp(cmp_pe_k[layer], cmp_k_w1[layer], cmp_k_b1[layer], cmp_k_w2[layer]),
            _pack_cmp(cmp_pe_v[layer], cmp_v_w1[layer], cmp_v_b1[layer], cmp_v_w2[layer]),
            bsz, seq)
        nsa = _nsa(qr_t, qp_t, kcmp, vcmp_t, ks, vs_t, kw, vw_t, ng_t, mcs_t, onehot, bsz, seq)
        h = _ffn2(h1, ret, nsa, w_out[layer].astype(BF16), ffn2_norm_w[layer][None, :],
                  ffn2_w_gate[layer].astype(BF16), ffn2_w_up[layer].astype(BF16),
                  ffn2_w_down[layer].astype(BF16), final_norm_w[None, :])
    return h.reshape(bsz, seq, d)
```

```python
import functools

import numpy as np
import jax
import jax.numpy as jnp
from jax import lax
from jax.experimental import pallas as pl
from jax.experimental.pallas import tpu as pltpu

F32 = jnp.float32
BF16 = jnp.bfloat16

D_MODEL = 1024
D_FF = 2816
RMS_EPS = 1e-6
GN_EPS = 1e-5

RET_HEADS = 4
RET_DK = 64
RET_DV = 128
RET_THETA = 10000.0

NSA_HEADS = 8
NSA_KV_HEADS = 2
NSA_GROUP = 4
NSA_DH = 64
CMP_LEN = 32
CMP_STRIDE = 16
CMP_HIDDEN = 256
SEL_LEN = 64
SEL_TOP = 16
WINDOW = 512
N_GATES = 3
ROPE_THETA = 500000.0
ROPE_DIM = 16

NEG_INF = -1e30
LOG2E = 1.4426950408889634
FORCE_BONUS = 1e4

LANE = 128
VMEM_LIMIT = 56 * 1024 * 1024

TM = 512
RET_CHUNK = 128
RET_GROUP = 8
TQ = 256
TK = 256
SEL_UNROLL = 4
LOOP_LOOKAHEAD = 6
ACC_ROWS = 80
FF_SPLIT = (1280, 1536)

SEG = dict(rq=(0, 256), rk=(256, 512), rv=(512, 1024), rg=(1024, 1536), nq=(1536, 2048),
           kvc=(2048, 2304), kvs=(2304, 2560), kvw=(2560, 2816), ng=(2816, 2944))
D_CAT = 2944
GATE_ROWS = 32


def _rms(x, w):
    return x * lax.rsqrt(jnp.mean(x * x, axis=-1, keepdims=True) + RMS_EPS) * w


def _params(sem):
    return pltpu.CompilerParams(dimension_semantics=sem, vmem_limit_bytes=VMEM_LIMIT)


def _const_spec(shape):
    nd = len(shape)
    return pl.BlockSpec(shape, lambda *_: (0,) * nd, pipeline_mode=pl.Buffered(1))


def _swiglu_half_step(h, nw, wg_ref, wu_ref, wd_ref):
    hn = _rms(h, nw).astype(BF16)
    acc = None
    off = 0
    for width in FF_SPLIT:
        g = jnp.dot(hn, wg_ref[:, off:off + width], preferred_element_type=F32)
        u = jnp.dot(hn, wu_ref[:, off:off + width], preferred_element_type=F32)
        a = (g * jax.nn.sigmoid(g) * u).astype(BF16)
        y = jnp.dot(a, wd_ref[off:off + width, :], preferred_element_type=F32)
        acc = y if acc is None else acc + y
        off += width
    return h + 0.5 * acc


def _ffn1_body(x_ref, nw_ref, wg_ref, wu_ref, wd_ref, o_ref):
    o_ref[...] = _swiglu_half_step(x_ref[...], nw_ref[...], wg_ref, wu_ref, wd_ref)


def _ffn1(x2, nw, wg, wu, wd):
    t = x2.shape[0]
    tok = pl.BlockSpec((TM, D_MODEL), lambda i: (i, 0))
    return pl.pallas_call(
        _ffn1_body,
        grid=(t // TM,),
        in_specs=[tok, _const_spec((1, D_MODEL)), _const_spec((D_MODEL, D_FF)),
                  _const_spec((D_MODEL, D_FF)), _const_spec((D_FF, D_MODEL))],
        out_specs=tok,
        out_shape=jax.ShapeDtypeStruct((t, D_MODEL), F32),
        compiler_params=_params(("parallel",)),
        name="ffn1",
    )(x2, nw, wg, wu, wd)


def _ffn2_body(h_ref, ret_ref, nsa_ref, wo_ref, nw_ref, wg_ref, wu_ref, wd_ref, fw_ref, o_ref):
    half = RET_HEADS * RET_DV
    mix = (jnp.dot(ret_ref[...], wo_ref[:half, :], preferred_element_type=F32)
           + jnp.dot(nsa_ref[...], wo_ref[half:, :], preferred_element_type=F32))
    h = h_ref[...] + mix
    h = _swiglu_half_step(h, nw_ref[...], wg_ref, wu_ref, wd_ref)
    o_ref[...] = _rms(h, fw_ref[...])


def _ffn2(h1, ret, nsa, wo, nw, wg, wu, wd, fw):
    t = h1.shape[0]
    tok = pl.BlockSpec((TM, D_MODEL), lambda i: (i, 0))
    half = pl.BlockSpec((TM, D_MODEL // 2), lambda i: (i, 0))
    return pl.pallas_call(
        _ffn2_body,
        grid=(t // TM,),
        in_specs=[tok, half, half, _const_spec((D_MODEL, D_MODEL)), _const_spec((1, D_MODEL)),
                  _const_spec((D_MODEL, D_FF)), _const_spec((D_MODEL, D_FF)),
                  _const_spec((D_FF, D_MODEL)), _const_spec((1, D_MODEL))],
        out_specs=tok,
        out_shape=jax.ShapeDtypeStruct((t, D_MODEL), F32),
        compiler_params=_params(("parallel",)),
        name="ffn2",
    )(h1, ret, nsa, wo, nw, wg, wu, wd, fw)


def _rope128(x, cos, sin_a, sin_b, half):
    return (x * cos + pltpu.roll(x, LANE - half, 1) * sin_a + pltpu.roll(x, half, 1) * sin_b)


def _proj_body(h_ref, nw_ref, w_ref, rc_ref, ra_ref, rb_ref, nc_ref, na_ref, nb_ref,
               rq_ref, rk_ref, rv_ref, rg_ref, qr_ref, qp_ref, kc_ref, vc_ref,
               ks_ref, vs_ref, kw_ref, vw_ref, ng_ref):
    u = _rms(h_ref[...], nw_ref[...]).astype(BF16)

    def seg(name):
        a, b = SEG[name]
        return jnp.dot(u, w_ref[:, a:b], preferred_element_type=F32)

    rc, ra, rb = rc_ref[...], ra_ref[...], rb_ref[...]
    nc, na, nb = nc_ref[...], na_ref[...], nb_ref[...]
    ret_half = RET_DK // 2
    nsa_half = ROPE_DIM // 2
    nsa_scale = NSA_DH ** -0.5 * LOG2E

    p = seg("rq")
    for j in range(2):
        sl = slice(j * LANE, (j + 1) * LANE)
        rq_ref[:, sl] = _rope128(p[:, sl], rc, ra, rb, ret_half)
    p = seg("rk")
    for j in range(2):
        sl = slice(j * LANE, (j + 1) * LANE)
        rk_ref[:, sl] = _rope128(p[:, sl], rc, ra, rb, ret_half) * (RET_DK ** -0.5)
    rv_ref[...] = seg("rv").astype(BF16)
    rg_ref[...] = seg("rg")
    p = seg("nq")
    for j in range(4):
        sl = slice(j * LANE, (j + 1) * LANE)
        qr_ref[sl, :] = (p[:, sl] * nsa_scale).T.astype(BF16)
        qp_ref[sl, :] = (_rope128(p[:, sl], nc, na, nb, nsa_half) * nsa_scale).T.astype(BF16)
    p = seg("kvc")
    kc_ref[...] = p[:, :LANE]
    vc_ref[...] = p[:, LANE:]
    p = seg("kvs")
    ks_ref[...] = _rope128(p[:, :LANE], nc, na, nb, nsa_half).astype(BF16)
    vs_ref[...] = p[:, LANE:].T.astype(BF16)
    p = seg("kvw")
    kw_ref[...] = _rope128(p[:, :LANE], nc, na, nb, nsa_half).astype(BF16)
    vw_ref[...] = p[:, LANE:].T.astype(BF16)
    ng_ref[...] = seg("ng").T[:GATE_ROWS, :]


def _proj(h1, nw, wcat, tabs, seq):
    t = h1.shape[0]
    bsz = t // seq
    per_seq = seq // TM

    def tok(width, dtype):
        return pl.BlockSpec((TM, width), lambda i: (i, 0)), jax.ShapeDtypeStruct((t, width), dtype)

    def tr(rows, dtype):
        return (pl.BlockSpec((None, rows, TM), lambda i: (i // per_seq, 0, i % per_seq)),
                jax.ShapeDtypeStruct((bsz, rows, seq), dtype))

    tab = pl.BlockSpec((TM, LANE), lambda i: (i % per_seq, 0))
    width = NSA_HEADS * NSA_DH
    outs = [tok(256, F32), tok(256, F32), tok(512, BF16), tok(512, F32),
            tr(width, BF16), tr(width, BF16),
            tok(LANE, F32), tok(LANE, F32),
            tok(LANE, BF16), tr(LANE, BF16), tok(LANE, BF16), tr(LANE, BF16),
            tr(GATE_ROWS, F32)]
    return pl.pallas_call(
        _proj_body,
        grid=(t // TM,),
        in_specs=[tok(D_MODEL, F32)[0], _const_spec((1, D_MODEL)), _const_spec((D_MODEL, D_CAT))] + [tab] * 6,
        out_specs=[o[0] for o in outs],
        out_shape=[o[1] for o in outs],
        compiler_params=_params(("parallel",)),
        name="proj",
    )(h1, nw, wcat, *tabs)


def _ret_body(q_ref, k_ref, v_ref, g_ref, dm_ref, xi_ref, ze_ref, gc_ref, avg_ref, nw_ref, o_ref, st_ref, *, seq):
    c = RET_CHUNK
    st_ref[...] = jnp.zeros_like(st_ref)

    def chunk_group(n, carry):
        heads = range(RET_HEADS)
        members = range(RET_GROUP)
        ks = [slice(h * RET_DK, (h + 1) * RET_DK) for h in heads]
        vs = [slice(h * RET_DV, (h + 1) * RET_DV) for h in heads]
        rows = [pl.ds(pl.multiple_of((n * RET_GROUP + ci) * c, c), c) for ci in members]
        q = [[q_ref[rows[ci], ks[h]] for h in heads] for ci in members]
        k = [[k_ref[rows[ci], ks[h]] for h in heads] for ci in members]
        v = [[v_ref[rows[ci], vs[h]] for h in heads] for ci in members]
        inner = [[lax.dot_general(q[ci][h].astype(BF16), k[ci][h].astype(BF16), (((1,), (1,)), ((), ())),
                                  preferred_element_type=F32) for h in heads] for ci in members]
        upd = [[lax.dot_general((k[ci][h] * ze_ref[h]).astype(BF16), v[ci][h], (((0,), (0,)), ((), ())),
                                preferred_element_type=F32) for h in heads] for ci in members]
        cross = [[None] * RET_HEADS for _ in members]
        for h in heads:
            state = st_ref[h]
            for ci in members:
                cross[ci][h] = jnp.dot((q[ci][h] * xi_ref[h]).astype(BF16), state.astype(BF16),
                                       preferred_element_type=F32)
                state = gc_ref[h] * state + upd[ci][h]
            st_ref[h] = state
        pairs = [(ci, h) for ci in members for h in heads]
        out = {u: jnp.dot((inner[u[0]][u[1]] * dm_ref[u[1]]).astype(BF16), v[u[0]][u[1]],
                          preferred_element_type=F32) + cross[u[0]][u[1]] for u in pairs}
        mu = {u: jnp.dot(out[u].astype(BF16), avg_ref[...], preferred_element_type=F32) for u in pairs}
        dev = {u: out[u] - mu[u] for u in pairs}
        var = {u: jnp.dot((dev[u] * dev[u]).astype(BF16), avg_ref[...], preferred_element_type=F32)
               for u in pairs}
        for ci, h in pairs:
            y = dev[(ci, h)] * lax.rsqrt(var[(ci, h)] + GN_EPS) * nw_ref[:, vs[h]]
            gate = g_ref[rows[ci], vs[h]]
            o_ref[rows[ci], vs[h]] = (y * (gate * jax.nn.sigmoid(gate))).astype(BF16)
        return carry

    lax.fori_loop(0, seq // (c * RET_GROUP), chunk_group, 0)


def _retention(rq, rk, rv, rg, tabs, nw, bsz, seq):
    t = bsz * seq
    c = RET_CHUNK
    assert seq % (c * RET_GROUP) == 0

    def seq_blk(width):
        return pl.BlockSpec((seq, width), lambda b: (b, 0))

    return pl.pallas_call(
        functools.partial(_ret_body, seq=seq),
        grid=(bsz,),
        in_specs=[seq_blk(256), seq_blk(256), seq_blk(512), seq_blk(512),
                  _const_spec((RET_HEADS, c, c)), _const_spec((RET_HEADS, c, RET_DK)),
                  _const_spec((RET_HEADS, c, RET_DK)), _const_spec((RET_HEADS, 1, 1)),
                  _const_spec((RET_DV, RET_DV)), _const_spec((1, RET_HEADS * RET_DV))],
        out_specs=seq_blk(512),
        out_shape=jax.ShapeDtypeStruct((t, RET_HEADS * RET_DV), BF16),
        scratch_shapes=[pltpu.VMEM((RET_HEADS, RET_DK, RET_DV), F32)],
        compiler_params=_params(("parallel",)),
        name="retention",
    )(rq, rk, rv, rg, *tabs, nw)


def _gelu_tanh(x):
    return 0.5 * x * (1.0 + jnp.tanh(np.sqrt(2.0 / np.pi).astype(np.float32) * (x + 0.044715 * (x * x * x))))


def _cmp_body(kc_ref, vc_ref, pek_ref, w1k_ref, b1k_ref, w2k_ref, pev_ref, w1v_ref, b1v_ref, w2v_ref,
              ko_ref, vo_ref, *, seq):
    n_grp = seq // CMP_STRIDE

    def one(x_ref, pe_ref, w1_ref, b1_ref, w2_ref, o_ref, transposed):
        first = None
        second = None
        for l in range(CMP_STRIDE):
            x = x_ref[pl.ds(l, n_grp, stride=CMP_STRIDE), :]
            a = jnp.dot((x + pe_ref[l:l + 1, :]).astype(BF16), w1_ref[l], preferred_element_type=F32)
            b = jnp.dot((x + pe_ref[CMP_STRIDE + l:CMP_STRIDE + l + 1, :]).astype(BF16),
                        w1_ref[CMP_STRIDE + l], preferred_element_type=F32)
            first = a if first is None else first + a
            second = b if second is None else second + b
        pre = first + pltpu.roll(second, n_grp - 1, 0) + b1_ref[...]
        hid = _gelu_tanh(pre).astype(BF16)
        res = jnp.dot(hid, w2_ref[...], preferred_element_type=F32)
        row = lax.broadcasted_iota(jnp.int32, res.shape, 0)
        res = jnp.where(row < n_grp - 1, res, 0.0)
        if transposed:
            o_ref[...] = res.T.astype(BF16)
        else:
            o_ref[...] = res.astype(BF16)

    one(kc_ref, pek_ref, w1k_ref, b1k_ref, w2k_ref, ko_ref, False)
    one(vc_ref, pev_ref, w1v_ref, b1v_ref, w2v_ref, vo_ref, True)


def _compress(kc, vc, wk, wv, bsz, seq):
    n_grp = seq // CMP_STRIDE
    blk = pl.BlockSpec((seq, LANE), lambda b: (b, 0))
    wspecs = [_const_spec((CMP_LEN, LANE)), _const_spec((CMP_LEN, LANE, 2 * CMP_HIDDEN)),
              _const_spec((1, 2 * CMP_HIDDEN)), _const_spec((2 * CMP_HIDDEN, LANE))]
    return pl.pallas_call(
        functools.partial(_cmp_body, seq=seq),
        grid=(bsz,),
        in_specs=[blk, blk] + wspecs + wspecs,
        out_specs=[pl.BlockSpec((None, n_grp, LANE), lambda b: (b, 0, 0)),
                   pl.BlockSpec((None, LANE, n_grp), lambda b: (b, 0, 0))],
        out_shape=[jax.ShapeDtypeStruct((bsz, n_grp, LANE), BF16),
                   jax.ShapeDtypeStruct((bsz, LANE, n_grp), BF16)],
        compiler_params=_params(("parallel",)),
        name="compress",
    )(kc, vc, *wk, *wv)


def _nsa_body(qr_ref, qp_ref, kc_ref, vc_ref, ks_ref, vs_ref, kw_ref, vw_ref, ng_ref, mcs_ref, oh_ref,
              o_ref, imp_ref, cnt_ref, qa_ref, m_ref, acc_ref, res_ref, *, n_top):
    i = pl.program_id(1)
    n_grp = kc_ref.shape[0]
    n_sel = mcs_ref.shape[0]
    t0 = i * TQ
    gates = jax.nn.sigmoid(ng_ref[...])
    lane_head = lax.broadcasted_iota(jnp.int32, (TK, LANE), 1) // NSA_DH
    rel0 = (lax.broadcasted_iota(jnp.int32, (TK, TQ), 0)
            - lax.broadcasted_iota(jnp.int32, (TK, TQ), 1))
    groups = range(NSA_KV_HEADS)

    def gate(g, kq, idx):
        r = (g * NSA_GROUP + kq) * N_GATES + idx
        return gates[r:r + 1, :]

    def load_keys(task):
        g, br, k_ref, v_ref, other, j, kind, arg = task
        rows = pl.ds(pl.multiple_of(j * TK, TK), TK)
        k_aug = jnp.where(lane_head == g, k_ref[rows, :], other(rows))
        v_t = jnp.concatenate([v_ref[g * NSA_DH:(g + 1) * NSA_DH, rows],
                               jnp.ones((ACC_ROWS - NSA_DH, TK), BF16)], axis=0)
        return k_aug, v_t

    def softmax(task, kq, s):
        g, br, k_ref, v_ref, other, j, kind, arg = task
        if kind == "causal":
            s = jnp.where(rel0 <= 0, s, NEG_INF)
        elif kind == "lower":
            s = jnp.where(rel0 > arg, s, NEG_INF)
        elif kind == "bias":
            s = s + arg
        m_prev = m_ref[g, br, kq]
        m_next = jnp.maximum(m_prev, jnp.max(s, axis=0, keepdims=True))
        m_ref[g, br, kq] = m_next
        return jnp.exp2(s - m_next).astype(BF16), jnp.exp2(m_prev - m_next)

    def run_tasks(tasks, lookahead, pv_group):
        operands = [load_keys(task) for task in tasks]
        units = [(n, kq) for n in range(len(tasks)) for kq in range(NSA_GROUP)]
        score = lambda u: jnp.dot(operands[u[0]][0], qa_ref[tasks[u[0]][0], u[1]], preferred_element_type=F32)
        pending = [score(u) for u in units[:lookahead]]
        done = []
        for pos, (n, kq) in enumerate(units):
            if pos + lookahead < len(units):
                pending.append(score(units[pos + lookahead]))
            done.append((n, kq) + softmax(tasks[n], kq, pending.pop(0)))
            if len(done) == pv_group:
                for n2, kq2, p, alpha in done:
                    g, br = tasks[n2][0], tasks[n2][1]
                    acc_ref[g, br, kq2] = alpha * acc_ref[g, br, kq2] + jnp.dot(
                        operands[n2][1], p, preferred_element_type=F32)
                done = []

    def compressed_branch(g, n_rows):
        head_rows = slice(g * NSA_DH, (g + 1) * NSA_DH)
        kc = kc_ref[0:n_rows, head_rows]
        vc_t = vc_ref[head_rows, 0:n_rows]
        c_end = (lax.broadcasted_iota(jnp.int32, (n_rows, TQ), 0) * CMP_STRIDE + (CMP_LEN - 1))
        mask = c_end <= t0 + lax.broadcasted_iota(jnp.int32, (n_rows, TQ), 1)
        heads = [g * NSA_GROUP + kq for kq in range(NSA_GROUP)]
        scores = [jnp.dot(kc, qr_ref[hh * NSA_DH:(hh + 1) * NSA_DH, :], preferred_element_type=F32)
                  for hh in heads]
        psum = jnp.zeros((n_rows, TQ), F32)
        probs = []
        for s in scores:
            s = jnp.where(mask, s, NEG_INF)
            m = jnp.max(s, axis=0, keepdims=True)
            e = jnp.exp2(s - m)
            l = jnp.sum(e, axis=0, keepdims=True)
            p = e * jnp.where(m > 0.5 * NEG_INF, 1.0 / l, 0.0)
            psum = psum + p
            probs.append(p.astype(BF16))
        for kq, hh in enumerate(heads):
            o = jnp.dot(vc_t, probs[kq], preferred_element_type=F32)
            res_ref[hh * NSA_DH:(hh + 1) * NSA_DH, :] = o * gate(g, kq, 0)
        hi = psum.astype(BF16)
        lo = (psum - hi.astype(F32)).astype(BF16)
        mcs = mcs_ref[:, 0:n_rows]
        imp = jnp.dot(mcs, hi, preferred_element_type=F32) + jnp.dot(mcs, lo, preferred_element_type=F32)
        blk = lax.broadcasted_iota(jnp.int32, (n_sel, TQ), 0)
        cur = jnp.right_shift(t0 + lax.broadcasted_iota(jnp.int32, (n_sel, TQ), 1), 6)
        valid = blk <= cur
        forced = (blk == 0) | (blk == cur) | (blk == cur - 1)
        imp_ref[...] = jnp.where(valid, imp + jnp.where(forced, FORCE_BONUS, 0.0), NEG_INF)

    for g in groups:
        first_half = i < pl.num_programs(1) // 2

        @pl.when(first_half)
        def _(g=g):
            compressed_branch(g, n_grp // 2)

        @pl.when(jnp.logical_not(first_half))
        def _(g=g):
            compressed_branch(g, n_grp)

        cnt_ref[...] = jnp.zeros_like(cnt_ref)
        per_tile = TQ // SEL_LEN
        for rnd in range(n_sel // per_tile):
            @pl.when(rnd <= i)
            def _(rnd=rnd):
                sub = 8
                contenders = [imp_ref[kk:kk + 1, :] for kk in range(rnd * per_tile, (rnd + 1) * per_tile)]
                for jg in range(n_sel // sub):
                    rows_j = slice(jg * sub, (jg + 1) * sub)
                    grp = imp_ref[rows_j, :]
                    cnt = cnt_ref[rows_j, :]
                    for off, row in enumerate(contenders):
                        kk = rnd * per_tile + off
                        if kk < jg * sub:
                            beats = row >= grp
                        elif kk >= (jg + 1) * sub:
                            beats = row > grp
                        else:
                            idx = jg * sub + lax.broadcasted_iota(jnp.int32, (sub, TQ), 0)
                            beats = (row > grp) | ((row == grp) & (idx > kk))
                        cnt = cnt + jnp.where(beats, 1.0, 0.0)
                    cnt_ref[rows_j, :] = cnt

        bias = jnp.where(cnt_ref[...] < n_top, 0.0, NEG_INF).astype(BF16)
        pad = jnp.zeros((NSA_DH - n_sel, TQ), BF16)
        for kq in range(NSA_GROUP):
            hh = g * NSA_GROUP + kq
            q_t = qp_ref[hh * NSA_DH:(hh + 1) * NSA_DH, :]
            parts = [q_t, bias, pad] if g == 0 else [bias, pad, q_t]
            qa_ref[g, kq] = jnp.concatenate([x for x in parts if x.shape[0] > 0], axis=0)

    m_ref[...] = jnp.full_like(m_ref, NEG_INF)
    acc_ref[...] = jnp.zeros_like(acc_ref)
    one_hot = lambda rows: oh_ref[rows, :]
    zeros = lambda rows: jnp.zeros((TK, LANE), BF16)

    for g in groups:
        def full_tiles(first, count, g=g):
            return [(g, 0, ks_ref, vs_ref, one_hot, first + n, None, None) for n in range(count)]

        def sel_quad(jj, carry, full_tiles=full_tiles):
            run_tasks(full_tiles(SEL_UNROLL * jj, SEL_UNROLL), LOOP_LOOKAHEAD, 1)
            return carry

        lax.fori_loop(0, i // SEL_UNROLL, sel_quad, 0)

        for left in range(1, SEL_UNROLL):
            @pl.when(i % SEL_UNROLL == left)
            def _(left=left, full_tiles=full_tiles):
                run_tasks(full_tiles((i // SEL_UNROLL) * SEL_UNROLL, left), LOOP_LOOKAHEAD, 1)

    tail = []
    for g in groups:
        tail += [
            (g, 0, ks_ref, vs_ref, one_hot, i, "causal", None),
            (g, 1, kw_ref, vw_ref, zeros, jnp.maximum(i - 2, 0), "lower", jnp.where(i >= 2, 0, 2 ** 30)),
            (g, 1, kw_ref, vw_ref, zeros, jnp.maximum(i - 1, 0), "bias", jnp.where(i >= 1, 0.0, NEG_INF)),
            (g, 1, kw_ref, vw_ref, zeros, i, "causal", None),
        ]
    run_tasks(tail, NSA_GROUP, NSA_GROUP)

    for g in groups:
        for br, idx in ((0, 1), (1, 2)):
            for kq in range(NSA_GROUP):
                hh = g * NSA_GROUP + kq
                dst = slice(hh * NSA_DH, (hh + 1) * NSA_DH)
                o = acc_ref[g, br, kq, 0:NSA_DH, :] / acc_ref[g, br, kq, NSA_DH:NSA_DH + 1, :]
                res_ref[dst, :] += o * gate(g, kq, idx)

    o_ref[...] = res_ref[...].T.astype(BF16)


def _nsa(qr_t, qp_t, kcmp, vcmp_t, ks, vs_t, kw, vw_t, ng_t, mcs_t, onehot, bsz, seq):
    assert WINDOW == 2 * TK and TQ == TK and TQ % SEL_LEN == 0 and SEL_UNROLL == 4
    t = bsz * seq
    nq = seq // TQ
    n_grp = seq // CMP_STRIDE
    n_sel = seq // SEL_LEN
    assert n_sel <= NSA_DH and n_sel % 16 == 0
    width = NSA_HEADS * NSA_DH
    q_t = pl.BlockSpec((None, width, TQ), lambda b, i: (b, 0, i))
    kv = pl.BlockSpec((seq, LANE), lambda b, i: (b, 0))
    kv_t = pl.BlockSpec((None, LANE, seq), lambda b, i: (b, 0, 0))
    return pl.pallas_call(
        functools.partial(_nsa_body, n_top=min(SEL_TOP, n_sel)),
        grid=(bsz, nq),
        in_specs=[q_t, q_t,
                  pl.BlockSpec((None, n_grp, LANE), lambda b, i: (b, 0, 0)),
                  pl.BlockSpec((None, LANE, n_grp), lambda b, i: (b, 0, 0)),
                  kv, kv_t, kv, kv_t,
                  pl.BlockSpec((None, GATE_ROWS, TQ), lambda b, i: (b, 0, i)),
                  _const_spec((n_sel, n_grp)), _const_spec((seq, LANE))],
        out_specs=pl.BlockSpec((TQ, width), lambda b, i: (b * nq + i, 0)),
        out_shape=jax.ShapeDtypeStruct((t, width), BF16),
        scratch_shapes=[pltpu.VMEM((n_sel, TQ), F32), pltpu.VMEM((n_sel, TQ), F32),
                        pltpu.VMEM((NSA_KV_HEADS, NSA_GROUP, LANE, TQ), BF16),
                        pltpu.VMEM((NSA_KV_HEADS, 2, NSA_GROUP, 1, TQ), F32),
                        pltpu.VMEM((NSA_KV_HEADS, 2, NSA_GROUP, ACC_ROWS, TQ), F32),
                        pltpu.VMEM((width, TQ), F32)],
        compiler_params=_params(("parallel", "parallel")),
        name="nsa",
    )(qr_t, qp_t, kcmp, vcmp_t, ks, vs_t, kw, vw_t, ng_t, mcs_t, onehot)


def _rope_tables(seq, rot_dim, theta, head_dim):
    half = rot_dim // 2
    inv_freq = theta ** (-2.0 * jnp.arange(half, dtype=F32) / rot_dim)
    ang = jnp.arange(seq, dtype=F32)[:, None] * inv_freq[None, :]
    cos, sin = jnp.cos(ang), jnp.sin(ang)
    rest = head_dim - rot_dim
    ones = jnp.ones((seq, rest), F32)
    zr = jnp.zeros((seq, rest), F32)
    zh = jnp.zeros((seq, half), F32)
    cos_t = jnp.concatenate([cos, cos, ones], axis=1)
    sin_a = jnp.concatenate([-sin, zh, zr], axis=1)
    sin_b = jnp.concatenate([zh, sin, zr], axis=1)
    rep = LANE // head_dim
    return [jnp.tile(a, (1, rep)) for a in (cos_t, sin_a, sin_b)]


def _retention_tables():
    c = RET_CHUNK
    gamma = 1.0 - 2.0 ** (-5.0 - jnp.arange(RET_HEADS, dtype=F32))
    log_g = jnp.log(gamma)
    j = jnp.arange(c, dtype=F32)
    diff = j[:, None] - j[None, :]
    dmat = jnp.where(diff >= 0, jnp.exp(log_g[:, None, None] * jnp.maximum(diff, 0.0)), 0.0)
    xi = jnp.exp(log_g[:, None] * (j[None, :] + 1.0))[:, :, None]
    zeta = jnp.exp(log_g[:, None] * (c - 1.0 - j[None, :]))[:, :, None]
    xi = jnp.broadcast_to(xi, (RET_HEADS, c, RET_DK))
    zeta = jnp.broadcast_to(zeta, (RET_HEADS, c, RET_DK))
    g_chunk = jnp.exp(log_g * c)[:, None, None]
    avg = jnp.full((RET_DV, RET_DV), 1.0 / RET_DV, BF16)
    return [dmat, xi, zeta, g_chunk, avg]


def _cmp_to_sel_t(seq):
    n_grp = seq // CMP_STRIDE
    n_sel = seq // SEL_LEN
    c_start = np.arange(n_grp) * CMP_STRIDE
    s_start = np.arange(n_sel) * SEL_LEN
    overlap = (np.minimum(c_start[None, :] + CMP_LEN, s_start[:, None] + SEL_LEN)
               - np.maximum(c_start[None, :], s_start[:, None]))
    m = np.clip(overlap, 0, None) / CMP_LEN
    m[:, n_grp - 1] = 0.0
    return jnp.asarray(m, dtype=BF16)


def _block_onehot(seq):
    n_sel = seq // SEL_LEN
    e = (np.arange(seq) // SEL_LEN)[:, None] == np.arange(NSA_DH)[None, :]
    return jnp.asarray(np.concatenate([e, e], axis=1), dtype=BF16)


def _pack_w_in(w_in):
    pad = jnp.zeros((D_MODEL, D_CAT - w_in.shape[1]), w_in.dtype)
    return jnp.concatenate([w_in, pad], axis=1).astype(BF16)


def _pack_cmp(pe, w1, b1, w2):
    pe2 = pe.reshape(CMP_LEN, NSA_KV_HEADS * NSA_DH)
    w1l = w1.reshape(CMP_LEN, NSA_DH, CMP_HIDDEN)
    z1 = jnp.zeros_like(w1l)
    w1bd = jnp.concatenate([jnp.concatenate([w1l, z1], axis=2),
                            jnp.concatenate([z1, w1l], axis=2)], axis=1).astype(BF16)
    b1bd = jnp.concatenate([b1, b1])[None, :]
    z2 = jnp.zeros_like(w2)
    w2bd = jnp.concatenate([jnp.concatenate([w2, z2], axis=1),
                            jnp.concatenate([z2, w2], axis=1)], axis=0).astype(BF16)
    return [pe2, w1bd, b1bd, w2bd]


def kernel(x, ffn1_norm_w, ffn1_w_gate, ffn1_w_up, ffn1_w_down, mix_norm_w, w_in, ret_norm_w, cmp_pe_k, cmp_k_w1, cmp_k_b1, cmp_k_w2, cmp_pe_v, cmp_v_w1, cmp_v_b1, cmp_v_w2, w_out, ffn2_norm_w, ffn2_w_gate, ffn2_w_up, ffn2_w_down, final_norm_w):
    bsz, seq, d = x.shape
    assert d == D_MODEL and seq % TM == 0 and seq % TQ == 0 and seq >= WINDOW
    assert ffn1_norm_w.shape[0] == 1, "the final norm is fused into the (single) layer's second FFN"
    t = bsz * seq
    h = x.reshape(t, d)
    n_layers = ffn1_norm_w.shape[0]
    ret_tabs = _retention_tables()
    ret_rope = _rope_tables(seq, RET_DK, RET_THETA, RET_DK)
    nsa_rope = _rope_tables(seq, ROPE_DIM, ROPE_THETA, NSA_DH)
    mcs_t = _cmp_to_sel_t(seq)
    onehot = _block_onehot(seq)
    for layer in range(n_layers):
        h1 = _ffn1(h, ffn1_norm_w[layer][None, :], ffn1_w_gate[layer].astype(BF16),
                   ffn1_w_up[layer].astype(BF16), ffn1_w_down[layer].astype(BF16))
        (rq, rk, rv, rg, qr_t, qp_t, kc, vc, ks, vs_t, kw, vw_t, ng_t) = _proj(
            h1, mix_norm_w[layer][None, :], _pack_w_in(w_in[layer]), ret_rope + nsa_rope, seq)
        ret = _retention(rq, rk, rv, rg, ret_tabs, ret_norm_w[layer][None, :], bsz, seq)
        kcmp, vcmp_t = _compress(
            kc, vc,
            _pack_cmp(cmp_pe_k[layer], cmp_k_w1[layer], cmp_k_b1[layer], cmp_k_w2[layer]),
            _pack_cmp(cmp_pe_v[layer], cmp_v_w1[layer], cmp_v_b1[layer], cmp_v_w2[layer]),
            bsz, seq)
        nsa = _nsa(qr_t, qp_t, kcmp, vcmp_t, ks, vs_t, kw, vw_t, ng_t, mcs_t, onehot, bsz, seq)
        h = _ffn2(h1, ret, nsa, w_out[layer].astype(BF16), ffn2_norm_w[layer][None, :],
                  ffn2_w_gate[layer].astype(BF16), ffn2_w_up[layer].astype(BF16),
                  ffn2_w_down[layer].astype(BF16), final_norm_w[None, :])
    return h.reshape(bsz, seq, d)
```

```python
import functools

import numpy as np
import jax
import jax.numpy as jnp
from jax import lax
from jax.experimental import pallas as pl
from jax.experimental.pallas import tpu as pltpu

F32 = jnp.float32
BF16 = jnp.bfloat16

D_MODEL = 1024
D_FF = 2816
RMS_EPS = 1e-6
GN_EPS = 1e-5

RET_HEADS = 4
RET_DK = 64
RET_DV = 128
RET_THETA = 10000.0

NSA_HEADS = 8
NSA_KV_HEADS = 2
NSA_GROUP = 4
NSA_DH = 64
CMP_LEN = 32
CMP_STRIDE = 16
CMP_HIDDEN = 256
SEL_LEN = 64
SEL_TOP = 16
WINDOW = 512
N_GATES = 3
ROPE_THETA = 500000.0
ROPE_DIM = 16

NEG_INF = -1e30
LOG2E = 1.4426950408889634
FORCE_BONUS = 1e4

LANE = 128
VMEM_LIMIT = 56 * 1024 * 1024

TM = 512
RET_CHUNK = 128
RET_GROUP = 8
TQ = 256
TK = 256
SEL_UNROLL = 4
LOOP_LOOKAHEAD = 6
SCORE_SLOTS = 16
ACC_ROWS = 80
FF_SPLIT = (1280, 1536)

SEG = dict(rq=(0, 256), rk=(256, 512), rv=(512, 1024), rg=(1024, 1536), nq=(1536, 2048),
           kvc=(2048, 2304), kvs=(2304, 2560), kvw=(2560, 2816), ng=(2816, 2944))
D_CAT = 2944
GATE_ROWS = 32


def _rms(x, w):
    return x * lax.rsqrt(jnp.mean(x * x, axis=-1, keepdims=True) + RMS_EPS) * w


def _params(sem):
    return pltpu.CompilerParams(dimension_semantics=sem, vmem_limit_bytes=VMEM_LIMIT)


def _const_spec(shape):
    nd = len(shape)
    return pl.BlockSpec(shape, lambda *_: (0,) * nd, pipeline_mode=pl.Buffered(1))


def _swiglu_half_step(h, nw, wg_ref, wu_ref, wd_ref):
    hn = _rms(h, nw).astype(BF16)
    acc = None
    off = 0
    for width in FF_SPLIT:
        g = jnp.dot(hn, wg_ref[:, off:off + width], preferred_element_type=F32)
        u = jnp.dot(hn, wu_ref[:, off:off + width], preferred_element_type=F32)
        a = (g * jax.nn.sigmoid(g) * u).astype(BF16)
        y = jnp.dot(a, wd_ref[off:off + width, :], preferred_element_type=F32)
        acc = y if acc is None else acc + y
        off += width
    return h + 0.5 * acc


def _ffn1_body(x_ref, nw_ref, wg_ref, wu_ref, wd_ref, o_ref):
    o_ref[...] = _swiglu_half_step(x_ref[...], nw_ref[...], wg_ref, wu_ref, wd_ref)


def _ffn1(x2, nw, wg, wu, wd):
    t = x2.shape[0]
    tok = pl.BlockSpec((TM, D_MODEL), lambda i: (i, 0))
    return pl.pallas_call(
        _ffn1_body,
        grid=(t // TM,),
        in_specs=[tok, _const_spec((1, D_MODEL)), _const_spec((D_MODEL, D_FF)),
                  _const_spec((D_MODEL, D_FF)), _const_spec((D_FF, D_MODEL))],
        out_specs=tok,
        out_shape=jax.ShapeDtypeStruct((t, D_MODEL), F32),
        compiler_params=_params(("parallel",)),
        name="ffn1",
    )(x2, nw, wg, wu, wd)


def _ffn2_body(h_ref, ret_ref, nsa_ref, wo_ref, nw_ref, wg_ref, wu_ref, wd_ref, fw_ref, o_ref):
    half = RET_HEADS * RET_DV
    mix = (jnp.dot(ret_ref[...], wo_ref[:half, :], preferred_element_type=F32)
           + jnp.dot(nsa_ref[...], wo_ref[half:, :], preferred_element_type=F32))
    h = h_ref[...] + mix
    h = _swiglu_half_step(h, nw_ref[...], wg_ref, wu_ref, wd_ref)
    o_ref[...] = _rms(h, fw_ref[...])


def _ffn2(h1, ret, nsa, wo, nw, wg, wu, wd, fw):
    t = h1.shape[0]
    tok = pl.BlockSpec((TM, D_MODEL), lambda i: (i, 0))
    half = pl.BlockSpec((TM, D_MODEL // 2), lambda i: (i, 0))
    return pl.pallas_call(
        _ffn2_body,
        grid=(t // TM,),
        in_specs=[tok, half, half, _const_spec((D_MODEL, D_MODEL)), _const_spec((1, D_MODEL)),
                  _const_spec((D_MODEL, D_FF)), _const_spec((D_MODEL, D_FF)),
                  _const_spec((D_FF, D_MODEL)), _const_spec((1, D_MODEL))],
        out_specs=tok,
        out_shape=jax.ShapeDtypeStruct((t, D_MODEL), F32),
        compiler_params=_params(("parallel",)),
        name="ffn2",
    )(h1, ret, nsa, wo, nw, wg, wu, wd, fw)


def _rope128(x, cos, sin_a, sin_b, half):
    return (x * cos + pltpu.roll(x, LANE - half, 1) * sin_a + pltpu.roll(x, half, 1) * sin_b)


def _proj_body(h_ref, nw_ref, w_ref, rc_ref, ra_ref, rb_ref, nc_ref, na_ref, nb_ref,
               rq_ref, rk_ref, rv_ref, rg_ref, qr_ref, qp_ref, kc_ref, vc_ref,
               ks_ref, vs_ref, kw_ref, vw_ref, ng_ref):
    u = _rms(h_ref[...], nw_ref[...]).astype(BF16)

    def seg(name):
        a, b = SEG[name]
        return jnp.dot(u, w_ref[:, a:b], preferred_element_type=F32)

    rc, ra, rb = rc_ref[...], ra_ref[...], rb_ref[...]
    nc, na, nb = nc_ref[...], na_ref[...], nb_ref[...]
    ret_half = RET_DK // 2
    nsa_half = ROPE_DIM // 2
    nsa_scale = NSA_DH ** -0.5 * LOG2E

    p = seg("rq")
    for j in range(2):
        sl = slice(j * LANE, (j + 1) * LANE)
        rq_ref[:, sl] = _rope128(p[:, sl], rc, ra, rb, ret_half)
    p = seg("rk")
    for j in range(2):
        sl = slice(j * LANE, (j + 1) * LANE)
        rk_ref[:, sl] = _rope128(p[:, sl], rc, ra, rb, ret_half) * (RET_DK ** -0.5)
    rv_ref[...] = seg("rv").astype(BF16)
    rg_ref[...] = seg("rg")
    p = seg("nq")
    for j in range(4):
        sl = slice(j * LANE, (j + 1) * LANE)
        qr_ref[sl, :] = (p[:, sl] * nsa_scale).T.astype(BF16)
        qp_ref[sl, :] = (_rope128(p[:, sl], nc, na, nb, nsa_half) * nsa_scale).T.astype(BF16)
    p = seg("kvc")
    kc_ref[...] = p[:, :LANE]
    vc_ref[...] = p[:, LANE:]
    p = seg("kvs")
    ks_ref[...] = _rope128(p[:, :LANE], nc, na, nb, nsa_half).astype(BF16)
    vs_ref[...] = p[:, LANE:].T.astype(BF16)
    p = seg("kvw")
    kw_ref[...] = _rope128(p[:, :LANE], nc, na, nb, nsa_half).astype(BF16)
    vw_ref[...] = p[:, LANE:].T.astype(BF16)
    ng_ref[...] = seg("ng").T[:GATE_ROWS, :]


def _proj(h1, nw, wcat, tabs, seq):
    t = h1.shape[0]
    bsz = t // seq
    per_seq = seq // TM

    def tok(width, dtype):
        return pl.BlockSpec((TM, width), lambda i: (i, 0)), jax.ShapeDtypeStruct((t, width), dtype)

    def tr(rows, dtype):
        return (pl.BlockSpec((None, rows, TM), lambda i: (i // per_seq, 0, i % per_seq)),
                jax.ShapeDtypeStruct((bsz, rows, seq), dtype))

    tab = pl.BlockSpec((TM, LANE), lambda i: (i % per_seq, 0))
    width = NSA_HEADS * NSA_DH
    outs = [tok(256, F32), tok(256, F32), tok(512, BF16), tok(512, F32),
            tr(width, BF16), tr(width, BF16),
            tok(LANE, F32), tok(LANE, F32),
            tok(LANE, BF16), tr(LANE, BF16), tok(LANE, BF16), tr(LANE, BF16),
            tr(GATE_ROWS, F32)]
    return pl.pallas_call(
        _proj_body,
        grid=(t // TM,),
        in_specs=[tok(D_MODEL, F32)[0], _const_spec((1, D_MODEL)), _const_spec((D_MODEL, D_CAT))] + [tab] * 6,
        out_specs=[o[0] for o in outs],
        out_shape=[o[1] for o in outs],
        compiler_params=_params(("parallel",)),
        name="proj",
    )(h1, nw, wcat, *tabs)


def _ret_body(q_ref, k_ref, v_ref, g_ref, dm_ref, xi_ref, ze_ref, gc_ref, avg_ref, nw_ref, o_ref, st_ref, *, seq):
    c = RET_CHUNK
    st_ref[...] = jnp.zeros_like(st_ref)

    def chunk_group(n, carry):
        heads = range(RET_HEADS)
        members = range(RET_GROUP)
        ks = [slice(h * RET_DK, (h + 1) * RET_DK) for h in heads]
        vs = [slice(h * RET_DV, (h + 1) * RET_DV) for h in heads]
        rows = [pl.ds(pl.multiple_of((n * RET_GROUP + ci) * c, c), c) for ci in members]
        q = [[q_ref[rows[ci], ks[h]] for h in heads] for ci in members]
        k = [[k_ref[rows[ci], ks[h]] for h in heads] for ci in members]
        v = [[v_ref[rows[ci], vs[h]] for h in heads] for ci in members]
        inner = [[lax.dot_general(q[ci][h].astype(BF16), k[ci][h].astype(BF16), (((1,), (1,)), ((), ())),
                                  preferred_element_type=F32) for h in heads] for ci in members]
        upd = [[lax.dot_general((k[ci][h] * ze_ref[h]).astype(BF16), v[ci][h], (((0,), (0,)), ((), ())),
                                preferred_element_type=F32) for h in heads] for ci in members]
        cross = [[None] * RET_HEADS for _ in members]
        for h in heads:
            state = st_ref[h]
            for ci in members:
                cross[ci][h] = jnp.dot((q[ci][h] * xi_ref[h]).astype(BF16), state.astype(BF16),
                                       preferred_element_type=F32)
                state = gc_ref[h] * state + upd[ci][h]
            st_ref[h] = state
        pairs = [(ci, h) for ci in members for h in heads]
        out = {u: jnp.dot((inner[u[0]][u[1]] * dm_ref[u[1]]).astype(BF16), v[u[0]][u[1]],
                          preferred_element_type=F32) + cross[u[0]][u[1]] for u in pairs}
        mu = {u: jnp.dot(out[u].astype(BF16), avg_ref[...], preferred_element_type=F32) for u in pairs}
        dev = {u: out[u] - mu[u] for u in pairs}
        var = {u: jnp.dot((dev[u] * dev[u]).astype(BF16), avg_ref[...], preferred_element_type=F32)
               for u in pairs}
        for ci, h in pairs:
            y = dev[(ci, h)] * lax.rsqrt(var[(ci, h)] + GN_EPS) * nw_ref[:, vs[h]]
            gate = g_ref[rows[ci], vs[h]]
            o_ref[rows[ci], vs[h]] = (y * (gate * jax.nn.sigmoid(gate))).astype(BF16)
        return carry

    lax.fori_loop(0, seq // (c * RET_GROUP), chunk_group, 0)


def _retention(rq, rk, rv, rg, tabs, nw, bsz, seq):
    t = bsz * seq
    c = RET_CHUNK
    assert seq % (c * RET_GROUP) == 0

    def seq_blk(width):
        return pl.BlockSpec((seq, width), lambda b: (b, 0))

    return pl.pallas_call(
        functools.partial(_ret_body, seq=seq),
        grid=(bsz,),
        in_specs=[seq_blk(256), seq_blk(256), seq_blk(512), seq_blk(512),
                  _const_spec((RET_HEADS, c, c)), _const_spec((RET_HEADS, c, RET_DK)),
                  _const_spec((RET_HEADS, c, RET_DK)), _const_spec((RET_HEADS, 1, 1)),
                  _const_spec((RET_DV, RET_DV)), _const_spec((1, RET_HEADS * RET_DV))],
        out_specs=seq_blk(512),
        out_shape=jax.ShapeDtypeStruct((t, RET_HEADS * RET_DV), BF16),
        scratch_shapes=[pltpu.VMEM((RET_HEADS, RET_DK, RET_DV), F32)],
        compiler_params=_params(("parallel",)),
        name="retention",
    )(rq, rk, rv, rg, *tabs, nw)


def _gelu_tanh(x):
    return 0.5 * x * (1.0 + jnp.tanh(np.sqrt(2.0 / np.pi).astype(np.float32) * (x + 0.044715 * (x * x * x))))


def _cmp_body(kc_ref, vc_ref, pek_ref, w1k_ref, b1k_ref, w2k_ref, pev_ref, w1v_ref, b1v_ref, w2v_ref,
              ko_ref, vo_ref, *, seq):
    n_grp = seq // CMP_STRIDE

    def one(x_ref, pe_ref, w1_ref, b1_ref, w2_ref, o_ref, transposed):
        first = None
        second = None
        for l in range(CMP_STRIDE):
            x = x_ref[pl.ds(l, n_grp, stride=CMP_STRIDE), :]
            a = jnp.dot((x + pe_ref[l:l + 1, :]).astype(BF16), w1_ref[l], preferred_element_type=F32)
            b = jnp.dot((x + pe_ref[CMP_STRIDE + l:CMP_STRIDE + l + 1, :]).astype(BF16),
                        w1_ref[CMP_STRIDE + l], preferred_element_type=F32)
            first = a if first is None else first + a
            second = b if second is None else second + b
        pre = first + pltpu.roll(second, n_grp - 1, 0) + b1_ref[...]
        hid = _gelu_tanh(pre).astype(BF16)
        res = jnp.dot(hid, w2_ref[...], preferred_element_type=F32)
        row = lax.broadcasted_iota(jnp.int32, res.shape, 0)
        res = jnp.where(row < n_grp - 1, res, 0.0)
        if transposed:
            o_ref[...] = res.T.astype(BF16)
        else:
            o_ref[...] = res.astype(BF16)

    one(kc_ref, pek_ref, w1k_ref, b1k_ref, w2k_ref, ko_ref, False)
    one(vc_ref, pev_ref, w1v_ref, b1v_ref, w2v_ref, vo_ref, True)


def _compress(kc, vc, wk, wv, bsz, seq):
    n_grp = seq // CMP_STRIDE
    blk = pl.BlockSpec((seq, LANE), lambda b: (b, 0))
    wspecs = [_const_spec((CMP_LEN, LANE)), _const_spec((CMP_LEN, LANE, 2 * CMP_HIDDEN)),
              _const_spec((1, 2 * CMP_HIDDEN)), _const_spec((2 * CMP_HIDDEN, LANE))]
    return pl.pallas_call(
        functools.partial(_cmp_body, seq=seq),
        grid=(bsz,),
        in_specs=[blk, blk] + wspecs + wspecs,
        out_specs=[pl.BlockSpec((None, n_grp, LANE), lambda b: (b, 0, 0)),
                   pl.BlockSpec((None, LANE, n_grp), lambda b: (b, 0, 0))],
        out_shape=[jax.ShapeDtypeStruct((bsz, n_grp, LANE), BF16),
                   jax.ShapeDtypeStruct((bsz, LANE, n_grp), BF16)],
        compiler_params=_params(("parallel",)),
        name="compress",
    )(kc, vc, *wk, *wv)


def _nsa_body(qr_ref, qp_ref, kc_ref, vc_ref, ks_ref, vs_ref, kw_ref, vw_ref, ng_ref, mcs_ref, oh_ref,
              o_ref, imp_ref, cnt_ref, qa_ref, m_ref, acc_ref, res_ref, s_ref, *, n_top):
    i = pl.program_id(1)
    n_grp = kc_ref.shape[0]
    n_sel = mcs_ref.shape[0]
    t0 = i * TQ
    gates = jax.nn.sigmoid(ng_ref[...])
    lane_head = lax.broadcasted_iota(jnp.int32, (TK, LANE), 1) // NSA_DH
    rel0 = (lax.broadcasted_iota(jnp.int32, (TK, TQ), 0)
            - lax.broadcasted_iota(jnp.int32, (TK, TQ), 1))
    groups = range(NSA_KV_HEADS)

    def gate(g, kq, idx):
        r = (g * NSA_GROUP + kq) * N_GATES + idx
        return gates[r:r + 1, :]

    def load_keys(task):
        g, br, k_ref, v_ref, other, j, kind, arg = task
        rows = pl.ds(pl.multiple_of(j * TK, TK), TK)
        k_aug = jnp.where(lane_head == g, k_ref[rows, :], other(rows))
        v_t = jnp.concatenate([v_ref[g * NSA_DH:(g + 1) * NSA_DH, rows],
                               jnp.ones((ACC_ROWS - NSA_DH, TK), BF16)], axis=0)
        return k_aug, v_t

    def softmax(task, kq, s):
        g, br, k_ref, v_ref, other, j, kind, arg = task
        m_prev = m_ref[g, br, kq]
        if kind in ("causal", "lower"):
            h = TK // 2
            lo, hi = slice(0, h), slice(h, TK)
            if kind == "causal":
                tri = rel0[lo, lo] <= 0
                quads = {(0, 0): jnp.where(tri, s[lo, lo], NEG_INF), (0, 1): s[lo, hi],
                         (1, 1): jnp.where(tri, s[hi, hi], NEG_INF)}
            else:
                tri = rel0[lo, lo] > arg
                quads = {(0, 0): jnp.where(tri, s[lo, lo], NEG_INF),
                         (1, 0): s[hi, lo] + jnp.where(arg == 0, 0.0, NEG_INF),
                         (1, 1): jnp.where(tri, s[hi, hi], NEG_INF)}
            col_max = []
            for c in (0, 1):
                parts = [jnp.max(x, axis=0, keepdims=True) for (r, cc), x in quads.items() if cc == c]
                col_max.append(functools.reduce(jnp.maximum, parts))
            m_next = jnp.maximum(m_prev, jnp.concatenate(col_max, axis=1))
            m_cols = (m_next[:, lo], m_next[:, hi])
            zero = jnp.zeros((h, h), BF16)
            blocks = [[jnp.exp2(quads[(r, c)] - m_cols[c]).astype(BF16) if (r, c) in quads else zero
                       for c in (0, 1)] for r in (0, 1)]
            p = jnp.concatenate([jnp.concatenate(row, axis=1) for row in blocks], axis=0)
        else:
            if kind == "bias":
                s = s + arg
            m_next = jnp.maximum(m_prev, jnp.max(s, axis=0, keepdims=True))
            p = jnp.exp2(s - m_next).astype(BF16)
        m_ref[g, br, kq] = m_next
        return p, jnp.exp2(m_prev - m_next)

    def run_tasks(tasks, lookahead, pv_group):
        operands = [load_keys(task) for task in tasks]
        units = [(n, kq) for n in range(len(tasks)) for kq in range(NSA_GROUP)]
        def score(pos):
            n, kq = units[pos]
            s_ref[pos % SCORE_SLOTS] = jnp.dot(operands[n][0], qa_ref[tasks[n][0], kq],
                                               preferred_element_type=F32)

        for pos in range(min(lookahead, len(units))):
            score(pos)
        done = []
        for pos, (n, kq) in enumerate(units):
            if pos + lookahead < len(units):
                score(pos + lookahead)
            done.append((n, kq) + softmax(tasks[n], kq, s_ref[pos % SCORE_SLOTS]))
            if len(done) == pv_group:
                for n2, kq2, p, alpha in done:
                    g, br = tasks[n2][0], tasks[n2][1]
                    acc_ref[g, br, kq2] = alpha * acc_ref[g, br, kq2] + jnp.dot(
                        operands[n2][1], p, preferred_element_type=F32)
                done = []

    def compressed_branch(g, n_rows):
        head_rows = slice(g * NSA_DH, (g + 1) * NSA_DH)
        kc = kc_ref[0:n_rows, head_rows]
        vc_t = vc_ref[head_rows, 0:n_rows]
        c_end = (lax.broadcasted_iota(jnp.int32, (n_rows, TQ), 0) * CMP_STRIDE + (CMP_LEN - 1))
        mask = c_end <= t0 + lax.broadcasted_iota(jnp.int32, (n_rows, TQ), 1)
        heads = [g * NSA_GROUP + kq for kq in range(NSA_GROUP)]
        scores = [jnp.dot(kc, qr_ref[hh * NSA_DH:(hh + 1) * NSA_DH, :], preferred_element_type=F32)
                  for hh in heads]
        psum = jnp.zeros((n_rows, TQ), F32)
        probs = []
        for s in scores:
            s = jnp.where(mask, s, NEG_INF)
            m = jnp.max(s, axis=0, keepdims=True)
            e = jnp.exp2(s - m)
            l = jnp.sum(e, axis=0, keepdims=True)
            p = e * jnp.where(m > 0.5 * NEG_INF, 1.0 / l, 0.0)
            psum = psum + p
            probs.append(p.astype(BF16))
        for kq, hh in enumerate(heads):
            o = jnp.dot(vc_t, probs[kq], preferred_element_type=F32)
            res_ref[hh * NSA_DH:(hh + 1) * NSA_DH, :] = o * gate(g, kq, 0)
        hi = psum.astype(BF16)
        lo = (psum - hi.astype(F32)).astype(BF16)
        mcs = mcs_ref[:, 0:n_rows]
        imp = jnp.dot(mcs, hi, preferred_element_type=F32) + jnp.dot(mcs, lo, preferred_element_type=F32)
        blk = lax.broadcasted_iota(jnp.int32, (n_sel, TQ), 0)
        cur = jnp.right_shift(t0 + lax.broadcasted_iota(jnp.int32, (n_sel, TQ), 1), 6)
        valid = blk <= cur
        forced = (blk == 0) | (blk == cur) | (blk == cur - 1)
        imp_ref[...] = jnp.where(valid, imp + jnp.where(forced, FORCE_BONUS, 0.0), NEG_INF)

    for g in groups:
        first_half = i < pl.num_programs(1) // 2

        @pl.when(first_half)
        def _(g=g):
            compressed_branch(g, n_grp // 2)

        @pl.when(jnp.logical_not(first_half))
        def _(g=g):
            compressed_branch(g, n_grp)

        cnt_ref[...] = jnp.zeros_like(cnt_ref)
        per_tile = TQ // SEL_LEN
        for rnd in range(n_sel // per_tile):
            @pl.when(rnd <= i)
            def _(rnd=rnd):
                sub = 8
                contenders = [imp_ref[kk:kk + 1, :] for kk in range(rnd * per_tile, (rnd + 1) * per_tile)]
                for jg in range(n_sel // sub):
                    rows_j = slice(jg * sub, (jg + 1) * sub)
                    grp = imp_ref[rows_j, :]
                    cnt = cnt_ref[rows_j, :]
                    for off, row in enumerate(contenders):
                        kk = rnd * per_tile + off
                        if kk < jg * sub:
                            beats = row >= grp
                        elif kk >= (jg + 1) * sub:
                            beats = row > grp
                        else:
                            idx = jg * sub + lax.broadcasted_iota(jnp.int32, (sub, TQ), 0)
                            beats = (row > grp) | ((row == grp) & (idx > kk))
                        cnt = cnt + jnp.where(beats, 1.0, 0.0)
                    cnt_ref[rows_j, :] = cnt

        bias = jnp.where(cnt_ref[...] < n_top, 0.0, NEG_INF).astype(BF16)
        pad = jnp.zeros((NSA_DH - n_sel, TQ), BF16)
        for kq in range(NSA_GROUP):
            hh = g * NSA_GROUP + kq
            q_t = qp_ref[hh * NSA_DH:(hh + 1) * NSA_DH, :]
            parts = [q_t, bias, pad] if g == 0 else [bias, pad, q_t]
            qa_ref[g, kq] = jnp.concatenate([x for x in parts if x.shape[0] > 0], axis=0)

    m_ref[...] = jnp.full_like(m_ref, NEG_INF)
    acc_ref[...] = jnp.zeros_like(acc_ref)
    one_hot = lambda rows: oh_ref[rows, :]
    zeros = lambda rows: jnp.zeros((TK, LANE), BF16)

    for g in groups:
        def full_tiles(first, count, g=g):
            return [(g, 0, ks_ref, vs_ref, one_hot, first + n, None, None) for n in range(count)]

        def sel_quad(jj, carry, full_tiles=full_tiles):
            run_tasks(full_tiles(SEL_UNROLL * jj, SEL_UNROLL), LOOP_LOOKAHEAD, 1)
            return carry

        lax.fori_loop(0, i // SEL_UNROLL, sel_quad, 0)

        for left in range(1, SEL_UNROLL):
            @pl.when(i % SEL_UNROLL == left)
            def _(left=left, full_tiles=full_tiles):
                run_tasks(full_tiles((i // SEL_UNROLL) * SEL_UNROLL, left), LOOP_LOOKAHEAD, 1)

    tail = []
    for g in groups:
        tail += [
            (g, 0, ks_ref, vs_ref, one_hot, i, "causal", None),
            (g, 1, kw_ref, vw_ref, zeros, jnp.maximum(i - 2, 0), "lower", jnp.where(i >= 2, 0, 2 ** 30)),
            (g, 1, kw_ref, vw_ref, zeros, jnp.maximum(i - 1, 0), "bias", jnp.where(i >= 1, 0.0, NEG_INF)),
            (g, 1, kw_ref, vw_ref, zeros, i, "causal", None),
        ]
    run_tasks(tail, LOOP_LOOKAHEAD, 1)

    for g in groups:
        for br, idx in ((0, 1), (1, 2)):
            for kq in range(NSA_GROUP):
                hh = g * NSA_GROUP + kq
                dst = slice(hh * NSA_DH, (hh + 1) * NSA_DH)
                o = acc_ref[g, br, kq, 0:NSA_DH, :] / acc_ref[g, br, kq, NSA_DH:NSA_DH + 1, :]
                res_ref[dst, :] += o * gate(g, kq, idx)

    o_ref[...] = res_ref[...].T.astype(BF16)


def _nsa(qr_t, qp_t, kcmp, vcmp_t, ks, vs_t, kw, vw_t, ng_t, mcs_t, onehot, bsz, seq):
    assert WINDOW == 2 * TK and TQ == TK and TQ % SEL_LEN == 0 and SEL_UNROLL == 4
    t = bsz * seq
    nq = seq // TQ
    n_grp = seq // CMP_STRIDE
    n_sel = seq // SEL_LEN
    assert n_sel <= NSA_DH and n_sel % 16 == 0
    width = NSA_HEADS * NSA_DH
    q_t = pl.BlockSpec((None, width, TQ), lambda b, i: (b, 0, i))
    kv = pl.BlockSpec((seq, LANE), lambda b, i: (b, 0))
    kv_t = pl.BlockSpec((None, LANE, seq), lambda b, i: (b, 0, 0))
    return pl.pallas_call(
        functools.partial(_nsa_body, n_top=min(SEL_TOP, n_sel)),
        grid=(bsz, nq),
        in_specs=[q_t, q_t,
                  pl.BlockSpec((None, n_grp, LANE), lambda b, i: (b, 0, 0)),
                  pl.BlockSpec((None, LANE, n_grp), lambda b, i: (b, 0, 0)),
                  kv, kv_t, kv, kv_t,
                  pl.BlockSpec((None, GATE_ROWS, TQ), lambda b, i: (b, 0, i)),
                  _const_spec((n_sel, n_grp)), _const_spec((seq, LANE))],
        out_specs=pl.BlockSpec((TQ, width), lambda b, i: (b * nq + i, 0)),
        out_shape=jax.ShapeDtypeStruct((t, width), BF16),
        scratch_shapes=[pltpu.VMEM((n_sel, TQ), F32), pltpu.VMEM((n_sel, TQ), F32),
                        pltpu.VMEM((NSA_KV_HEADS, NSA_GROUP, LANE, TQ), BF16),
                        pltpu.VMEM((NSA_KV_HEADS, 2, NSA_GROUP, 1, TQ), F32),
                        pltpu.VMEM((NSA_KV_HEADS, 2, NSA_GROUP, ACC_ROWS, TQ), F32),
                        pltpu.VMEM((width, TQ), F32), pltpu.VMEM((SCORE_SLOTS, TK, TQ), F32)],
        compiler_params=_params(("parallel", "parallel")),
        name="nsa",
    )(qr_t, qp_t, kcmp, vcmp_t, ks, vs_t, kw, vw_t, ng_t, mcs_t, onehot)


def _rope_tables(seq, rot_dim, theta, head_dim):
    half = rot_dim // 2
    inv_freq = theta ** (-2.0 * jnp.arange(half, dtype=F32) / rot_dim)
    ang = jnp.arange(seq, dtype=F32)[:, None] * inv_freq[None, :]
    cos, sin = jnp.cos(ang), jnp.sin(ang)
    rest = head_dim - rot_dim
    ones = jnp.ones((seq, rest), F32)
    zr = jnp.zeros((seq, rest), F32)
    zh = jnp.zeros((seq, half), F32)
    cos_t = jnp.concatenate([cos, cos, ones], axis=1)
    sin_a = jnp.concatenate([-sin, zh, zr], axis=1)
    sin_b = jnp.concatenate([zh, sin, zr], axis=1)
    rep = LANE // head_dim
    return [jnp.tile(a, (1, rep)) for a in (cos_t, sin_a, sin_b)]


def _retention_tables():
    c = RET_CHUNK
    gamma = 1.0 - 2.0 ** (-5.0 - jnp.arange(RET_HEADS, dtype=F32))
    log_g = jnp.log(gamma)
    j = jnp.arange(c, dtype=F32)
    diff = j[:, None] - j[None, :]
    dmat = jnp.where(diff >= 0, jnp.exp(log_g[:, None, None] * jnp.maximum(diff, 0.0)), 0.0)
    xi = jnp.exp(log_g[:, None] * (j[None, :] + 1.0))[:, :, None]
    zeta = jnp.exp(log_g[:, None] * (c - 1.0 - j[None, :]))[:, :, None]
    xi = jnp.broadcast_to(xi, (RET_HEADS, c, RET_DK))
    zeta = jnp.broadcast_to(zeta, (RET_HEADS, c, RET_DK))
    g_chunk = jnp.exp(log_g * c)[:, None, None]
    avg = jnp.full((RET_DV, RET_DV), 1.0 / RET_DV, BF16)
    return [dmat, xi, zeta, g_chunk, avg]


def _cmp_to_sel_t(seq):
    n_grp = seq // CMP_STRIDE
    n_sel = seq // SEL_LEN
    c_start = np.arange(n_grp) * CMP_STRIDE
    s_start = np.arange(n_sel) * SEL_LEN
    overlap = (np.minimum(c_start[None, :] + CMP_LEN, s_start[:, None] + SEL_LEN)
               - np.maximum(c_start[None, :], s_start[:, None]))
    m = np.clip(overlap, 0, None) / CMP_LEN
    m[:, n_grp - 1] = 0.0
    return jnp.asarray(m, dtype=BF16)


def _block_onehot(seq):
    n_sel = seq // SEL_LEN
    e = (np.arange(seq) // SEL_LEN)[:, None] == np.arange(NSA_DH)[None, :]
    return jnp.asarray(np.concatenate([e, e], axis=1), dtype=BF16)


def _pack_w_in(w_in):
    pad = jnp.zeros((D_MODEL, D_CAT - w_in.shape[1]), w_in.dtype)
    return jnp.concatenate([w_in, pad], axis=1).astype(BF16)


def _pack_cmp(pe, w1, b1, w2):
    pe2 = pe.reshape(CMP_LEN, NSA_KV_HEADS * NSA_DH)
    w1l = w1.reshape(CMP_LEN, NSA_DH, CMP_HIDDEN)
    z1 = jnp.zeros_like(w1l)
    w1bd = jnp.concatenate([jnp.concatenate([w1l, z1], axis=2),
                            jnp.concatenate([z1, w1l], axis=2)], axis=1).astype(BF16)
    b1bd = jnp.concatenate([b1, b1])[None, :]
    z2 = jnp.zeros_like(w2)
    w2bd = jnp.concatenate([jnp.concatenate([w2, z2], axis=1),
                            jnp.concatenate([z2, w2], axis=1)], axis=0).astype(BF16)
    return [pe2, w1bd, b1bd, w2bd]


def kernel(x, ffn1_norm_w, ffn1_w_gate, ffn1_w_up, ffn1_w_down, mix_norm_w, w_in, ret_norm_w, cmp_pe_k, cmp_k_w1, cmp_k_b1, cmp_k_w2, cmp_pe_v, cmp_v_w1, cmp_v_b1, cmp_v_w2, w_out, ffn2_norm_w, ffn2_w_gate, ffn2_w_up, ffn2_w_down, final_norm_w):
    bsz, seq, d = x.shape
    assert d == D_MODEL and seq % TM == 0 and seq % TQ == 0 and seq >= WINDOW
    assert ffn1_norm_w.shape[0] == 1, "the final norm is fused into the (single) layer's second FFN"
    t = bsz * seq
    h = x.reshape(t, d)
    n_layers = ffn1_norm_w.shape[0]
    ret_tabs = _retention_tables()
    ret_rope = _rope_tables(seq, RET_DK, RET_THETA, RET_DK)
    nsa_rope = _rope_tables(seq, ROPE_DIM, ROPE_THETA, NSA_DH)
    mcs_t = _cmp_to_sel_t(seq)
    onehot = _block_onehot(seq)
    for layer in range(n_layers):
        h1 = _ffn1(h, ffn1_norm_w[layer][None, :], ffn1_w_gate[layer].astype(BF16),
                   ffn1_w_up[layer].astype(BF16), ffn1_w_down[layer].astype(BF16))
        (rq, rk, rv, rg, qr_t, qp_t, kc, vc, ks, vs_t, kw, vw_t, ng_t) = _proj(
            h1, mix_norm_w[layer][None, :], _pack_w_in(w_in[layer]), ret_rope + nsa_rope, seq)
        ret = _retention(rq, rk, rv, rg, ret_tabs, ret_norm_w[layer][None, :], bsz, seq)
        kcmp, vcmp_t = _compress(
            kc, vc,
            _pack_cmp(cmp_pe_k[layer], cmp_k_w1[layer], cmp_k_b1[layer], cmp_k_w2[layer]),
            _pack_cmp(cmp_pe_v[layer], cmp_v_w1[layer], cmp_v_b1[layer], cmp_v_w2[layer]),
            bsz, seq)
        nsa = _nsa(qr_t, qp_t, kcmp, vcmp_t, ks, vs_t, kw, vw_t, ng_t, mcs_t, onehot, bsz, seq)
        h = _ffn2(h1, ret, nsa, w_out[layer].astype(BF16), ffn2_norm_w[layer][None, :],
                  ffn2_w_gate[layer].astype(BF16), ffn2_w_up[layer].astype(BF16),
                  ffn2_w_down[layer].astype(BF16), final_norm_w[None, :])
    return h.reshape(bsz, seq, d)
```

```python
import functools

import numpy as np
import jax
import jax.numpy as jnp
from jax import lax
from jax.experimental import pallas as pl
from jax.experimental.pallas import tpu as pltpu

F32 = jnp.float32
BF16 = jnp.bfloat16

D_MODEL = 1024
D_FF = 2816
RMS_EPS = 1e-6
GN_EPS = 1e-5

RET_HEADS = 4
RET_DK = 64
RET_DV = 128
RET_THETA = 10000.0

NSA_HEADS = 8
NSA_KV_HEADS = 2
NSA_GROUP = 4
NSA_DH = 64
CMP_LEN = 32
CMP_STRIDE = 16
CMP_HIDDEN = 256
SEL_LEN = 64
SEL_TOP = 16
WINDOW = 512
N_GATES = 3
ROPE_THETA = 500000.0
ROPE_DIM = 16

NEG_INF = -1e30
LOG2E = 1.4426950408889634
FORCE_BONUS = 1e4

LANE = 128
VMEM_LIMIT = 56 * 1024 * 1024

TM = 512
RET_CHUNK = 128
RET_GROUP = 8
TQ = 256
TK = 256
SEL_UNROLL = 8
SEL_TAIL = 4
LOOP_LOOKAHEAD = 6
SCORE_SLOTS = 16
ACC_ROWS = 80
FF_SPLIT = (1280, 1536)
W_CHUNKS = 8

SEG = dict(rq=(0, 256), rk=(256, 512), rv=(512, 1024), rg=(1024, 1536), nq=(1536, 2048),
           kvc=(2048, 2304), kvs=(2304, 2560), kvw=(2560, 2816), ng=(2816, 2944))
D_CAT = 2944
GATE_ROWS = 32


def _rms(x, w):
    return x * lax.rsqrt(jnp.mean(x * x, axis=-1, keepdims=True) + RMS_EPS) * w


def _params(sem):
    return pltpu.CompilerParams(dimension_semantics=sem, vmem_limit_bytes=VMEM_LIMIT)


def _const_spec(shape):
    nd = len(shape)
    return pl.BlockSpec(shape, lambda *_: (0,) * nd, pipeline_mode=pl.Buffered(1))


def _swiglu_half_step(h, nw, wg_ref, wu_ref, wd_ref):
    hn = _rms(h, nw).astype(BF16)
    acc = None
    off = 0
    for width in FF_SPLIT:
        g = jnp.dot(hn, wg_ref[:, off:off + width], preferred_element_type=F32)
        u = jnp.dot(hn, wu_ref[:, off:off + width], preferred_element_type=F32)
        a = (g * jax.nn.sigmoid(g) * u).astype(BF16)
        y = jnp.dot(a, wd_ref[off:off + width, :], preferred_element_type=F32)
        acc = y if acc is None else acc + y
        off += width
    return h + 0.5 * acc


def _convert_weight_chunk(step, pairs):
    for src_ref, dst_ref in pairs:
        rows = src_ref.shape[0]
        dst_ref[pl.ds(pl.multiple_of(step * rows, 16), rows), :] = src_ref[...].astype(BF16)


def _ffn1_body(x_ref, nw_ref, wg_ref, wu_ref, wd_ref, o_ref, wg_bf, wu_bf, wd_bf):
    step = pl.program_id(0)

    @pl.when(step < W_CHUNKS)
    def _():
        _convert_weight_chunk(step, [(wg_ref, wg_bf), (wu_ref, wu_bf), (wd_ref, wd_bf)])

    @pl.when(step >= W_CHUNKS)
    def _():
        o_ref[...] = _swiglu_half_step(x_ref[...], nw_ref[...], wg_bf, wu_bf, wd_bf)


def _token_tile(width):
    return pl.BlockSpec((TM, width), lambda i: (jnp.maximum(i - W_CHUNKS, 0), 0))


def _weight_chunk(rows, cols):
    return pl.BlockSpec((rows // W_CHUNKS, cols), lambda i: (jnp.minimum(i, W_CHUNKS - 1), 0))


def _ffn1(x2, nw, wg, wu, wd):
    t = x2.shape[0]
    return pl.pallas_call(
        _ffn1_body,
        grid=(W_CHUNKS + t // TM,),
        in_specs=[_token_tile(D_MODEL), _const_spec((1, D_MODEL)), _weight_chunk(D_MODEL, D_FF),
                  _weight_chunk(D_MODEL, D_FF), _weight_chunk(D_FF, D_MODEL)],
        out_specs=_token_tile(D_MODEL),
        out_shape=jax.ShapeDtypeStruct((t, D_MODEL), F32),
        scratch_shapes=[pltpu.VMEM((D_MODEL, D_FF), BF16), pltpu.VMEM((D_MODEL, D_FF), BF16),
                        pltpu.VMEM((D_FF, D_MODEL), BF16)],
        compiler_params=_params(("arbitrary",)),
        name="ffn1",
    )(x2, nw, wg, wu, wd)


def _ffn2_body(h_ref, ret_ref, nsa_ref, wo_ref, nw_ref, wg_ref, wu_ref, wd_ref, fw_ref, o_ref,
               wo_bf, wg_bf, wu_bf, wd_bf):
    step = pl.program_id(0)

    @pl.when(step < W_CHUNKS)
    def _():
        _convert_weight_chunk(step, [(wo_ref, wo_bf), (wg_ref, wg_bf), (wu_ref, wu_bf), (wd_ref, wd_bf)])

    @pl.when(step >= W_CHUNKS)
    def _():
        half = RET_HEADS * RET_DV
        mix = (jnp.dot(ret_ref[...], wo_bf[:half, :], preferred_element_type=F32)
               + jnp.dot(nsa_ref[...], wo_bf[half:, :], preferred_element_type=F32))
        h = h_ref[...] + mix
        h = _swiglu_half_step(h, nw_ref[...], wg_bf, wu_bf, wd_bf)
        o_ref[...] = _rms(h, fw_ref[...])


def _ffn2(h1, ret, nsa, wo, nw, wg, wu, wd, fw):
    t = h1.shape[0]
    return pl.pallas_call(
        _ffn2_body,
        grid=(W_CHUNKS + t // TM,),
        in_specs=[_token_tile(D_MODEL), _token_tile(D_MODEL // 2), _token_tile(D_MODEL // 2),
                  _weight_chunk(D_MODEL, D_MODEL), _const_spec((1, D_MODEL)),
                  _weight_chunk(D_MODEL, D_FF), _weight_chunk(D_MODEL, D_FF),
                  _weight_chunk(D_FF, D_MODEL), _const_spec((1, D_MODEL))],
        out_specs=_token_tile(D_MODEL),
        out_shape=jax.ShapeDtypeStruct((t, D_MODEL), F32),
        scratch_shapes=[pltpu.VMEM((D_MODEL, D_MODEL), BF16), pltpu.VMEM((D_MODEL, D_FF), BF16),
                        pltpu.VMEM((D_MODEL, D_FF), BF16), pltpu.VMEM((D_FF, D_MODEL), BF16)],
        compiler_params=_params(("arbitrary",)),
        name="ffn2",
    )(h1, ret, nsa, wo, nw, wg, wu, wd, fw)


def _rope128(x, cos, sin_a, sin_b, half):
    return (x * cos + pltpu.roll(x, LANE - half, 1) * sin_a + pltpu.roll(x, half, 1) * sin_b)


def _proj_body(h_ref, nw_ref, w_ref, rc_ref, ra_ref, rb_ref, nc_ref, na_ref, nb_ref,
               rq_ref, rk_ref, rv_ref, rg_ref, qr_ref, qp_ref, kc_ref, vc_ref,
               ks_ref, vs_ref, kw_ref, vw_ref, ng_ref):
    u = _rms(h_ref[...], nw_ref[...]).astype(BF16)

    def seg(name):
        a, b = SEG[name]
        return jnp.dot(u, w_ref[:, a:b], preferred_element_type=F32)

    rc, ra, rb = rc_ref[...], ra_ref[...], rb_ref[...]
    nc, na, nb = nc_ref[...], na_ref[...], nb_ref[...]
    ret_half = RET_DK // 2
    nsa_half = ROPE_DIM // 2
    nsa_scale = NSA_DH ** -0.5 * LOG2E

    p = seg("rq")
    for j in range(2):
        sl = slice(j * LANE, (j + 1) * LANE)
        rq_ref[:, sl] = _rope128(p[:, sl], rc, ra, rb, ret_half)
    p = seg("rk")
    for j in range(2):
        sl = slice(j * LANE, (j + 1) * LANE)
        rk_ref[:, sl] = _rope128(p[:, sl], rc, ra, rb, ret_half) * (RET_DK ** -0.5)
    rv_ref[...] = seg("rv").astype(BF16)
    rg_ref[...] = seg("rg")
    p = seg("nq")
    for j in range(4):
        sl = slice(j * LANE, (j + 1) * LANE)
        qr_ref[sl, :] = (p[:, sl] * nsa_scale).T.astype(BF16)
        qp_ref[sl, :] = (_rope128(p[:, sl], nc, na, nb, nsa_half) * nsa_scale).T.astype(BF16)
    p = seg("kvc")
    kc_ref[...] = p[:, :LANE]
    vc_ref[...] = p[:, LANE:]
    p = seg("kvs")
    ks_ref[...] = _rope128(p[:, :LANE], nc, na, nb, nsa_half).astype(BF16)
    vs_ref[...] = p[:, LANE:].T.astype(BF16)
    p = seg("kvw")
    kw_ref[...] = _rope128(p[:, :LANE], nc, na, nb, nsa_half).astype(BF16)
    vw_ref[...] = p[:, LANE:].T.astype(BF16)
    ng_ref[...] = seg("ng").T[:GATE_ROWS, :]


def _proj(h1, nw, wcat, tabs, seq):
    t = h1.shape[0]
    bsz = t // seq
    per_seq = seq // TM

    def tok(width, dtype):
        return pl.BlockSpec((TM, width), lambda i: (i, 0)), jax.ShapeDtypeStruct((t, width), dtype)

    def tr(rows, dtype):
        return (pl.BlockSpec((None, rows, TM), lambda i: (i // per_seq, 0, i % per_seq)),
                jax.ShapeDtypeStruct((bsz, rows, seq), dtype))

    tab = pl.BlockSpec((TM, LANE), lambda i: (i % per_seq, 0))
    width = NSA_HEADS * NSA_DH
    outs = [tok(256, F32), tok(256, F32), tok(512, BF16), tok(512, F32),
            tr(width, BF16), tr(width, BF16),
            tok(LANE, F32), tok(LANE, F32),
            tok(LANE, BF16), tr(LANE, BF16), tok(LANE, BF16), tr(LANE, BF16),
            tr(GATE_ROWS, F32)]
    return pl.pallas_call(
        _proj_body,
        grid=(t // TM,),
        in_specs=[tok(D_MODEL, F32)[0], _const_spec((1, D_MODEL)), _const_spec((D_MODEL, D_CAT))] + [tab] * 6,
        out_specs=[o[0] for o in outs],
        out_shape=[o[1] for o in outs],
        compiler_params=_params(("parallel",)),
        name="proj",
    )(h1, nw, wcat, *tabs)


def _ret_body(q_ref, k_ref, v_ref, g_ref, dm_ref, xi_ref, ze_ref, gc_ref, avg_ref, nw_ref, o_ref, st_ref, *, seq):
    c = RET_CHUNK
    st_ref[...] = jnp.zeros_like(st_ref)

    def chunk_group(n, carry):
        heads = range(RET_HEADS)
        members = range(RET_GROUP)
        ks = [slice(h * RET_DK, (h + 1) * RET_DK) for h in heads]
        vs = [slice(h * RET_DV, (h + 1) * RET_DV) for h in heads]
        rows = [pl.ds(pl.multiple_of((n * RET_GROUP + ci) * c, c), c) for ci in members]
        q = [[q_ref[rows[ci], ks[h]] for h in heads] for ci in members]
        k = [[k_ref[rows[ci], ks[h]] for h in heads] for ci in members]
        v = [[v_ref[rows[ci], vs[h]] for h in heads] for ci in members]
        inner = [[lax.dot_general(q[ci][h].astype(BF16), k[ci][h].astype(BF16), (((1,), (1,)), ((), ())),
                                  preferred_element_type=F32) for h in heads] for ci in members]
        upd = [[lax.dot_general((k[ci][h] * ze_ref[h]).astype(BF16), v[ci][h], (((0,), (0,)), ((), ())),
                                preferred_element_type=F32) for h in heads] for ci in members]
        cross = [[None] * RET_HEADS for _ in members]
        for h in heads:
            state = st_ref[h]
            for ci in members:
                cross[ci][h] = jnp.dot((q[ci][h] * xi_ref[h]).astype(BF16), state.astype(BF16),
                                       preferred_element_type=F32)
                state = gc_ref[h] * state + upd[ci][h]
            st_ref[h] = state
        pairs = [(ci, h) for ci in members for h in heads]
        out = {u: jnp.dot((inner[u[0]][u[1]] * dm_ref[u[1]]).astype(BF16), v[u[0]][u[1]],
                          preferred_element_type=F32) + cross[u[0]][u[1]] for u in pairs}
        mu = {u: jnp.dot(out[u].astype(BF16), avg_ref[...], preferred_element_type=F32) for u in pairs}
        dev = {u: out[u] - mu[u] for u in pairs}
        var = {u: jnp.dot((dev[u] * dev[u]).astype(BF16), avg_ref[...], preferred_element_type=F32)
               for u in pairs}
        for ci, h in pairs:
            y = dev[(ci, h)] * lax.rsqrt(var[(ci, h)] + GN_EPS) * nw_ref[:, vs[h]]
            gate = g_ref[rows[ci], vs[h]]
            o_ref[rows[ci], vs[h]] = (y * (gate * jax.nn.sigmoid(gate))).astype(BF16)
        return carry

    lax.fori_loop(0, seq // (c * RET_GROUP), chunk_group, 0)


def _retention(rq, rk, rv, rg, tabs, nw, bsz, seq):
    t = bsz * seq
    c = RET_CHUNK
    assert seq % (c * RET_GROUP) == 0

    def seq_blk(width):
        return pl.BlockSpec((seq, width), lambda b: (b, 0))

    return pl.pallas_call(
        functools.partial(_ret_body, seq=seq),
        grid=(bsz,),
        in_specs=[seq_blk(256), seq_blk(256), seq_blk(512), seq_blk(512),
                  _const_spec((RET_HEADS, c, c)), _const_spec((RET_HEADS, c, RET_DK)),
                  _const_spec((RET_HEADS, c, RET_DK)), _const_spec((RET_HEADS, 1, 1)),
                  _const_spec((RET_DV, RET_DV)), _const_spec((1, RET_HEADS * RET_DV))],
        out_specs=seq_blk(512),
        out_shape=jax.ShapeDtypeStruct((t, RET_HEADS * RET_DV), BF16),
        scratch_shapes=[pltpu.VMEM((RET_HEADS, RET_DK, RET_DV), F32)],
        compiler_params=_params(("parallel",)),
        name="retention",
    )(rq, rk, rv, rg, *tabs, nw)


def _gelu_tanh(x):
    return 0.5 * x * (1.0 + jnp.tanh(np.sqrt(2.0 / np.pi).astype(np.float32) * (x + 0.044715 * (x * x * x))))


def _cmp_body(kc_ref, vc_ref, pek_ref, w1k_ref, b1k_ref, w2k_ref, pev_ref, w1v_ref, b1v_ref, w2v_ref,
              ko_ref, vo_ref, *, seq):
    n_grp = seq // CMP_STRIDE

    def one(x_ref, pe_ref, w1_ref, b1_ref, w2_ref, o_ref, transposed):
        first = None
        second = None
        for l in range(CMP_STRIDE):
            x = x_ref[pl.ds(l, n_grp, stride=CMP_STRIDE), :]
            a = jnp.dot((x + pe_ref[l:l + 1, :]).astype(BF16), w1_ref[l], preferred_element_type=F32)
            b = jnp.dot((x + pe_ref[CMP_STRIDE + l:CMP_STRIDE + l + 1, :]).astype(BF16),
                        w1_ref[CMP_STRIDE + l], preferred_element_type=F32)
            first = a if first is None else first + a
            second = b if second is None else second + b
        pre = first + pltpu.roll(second, n_grp - 1, 0) + b1_ref[...]
        hid = _gelu_tanh(pre).astype(BF16)
        res = jnp.dot(hid, w2_ref[...], preferred_element_type=F32)
        row = lax.broadcasted_iota(jnp.int32, res.shape, 0)
        res = jnp.where(row < n_grp - 1, res, 0.0)
        if transposed:
            o_ref[...] = res.T.astype(BF16)
        else:
            o_ref[...] = res.astype(BF16)

    one(kc_ref, pek_ref, w1k_ref, b1k_ref, w2k_ref, ko_ref, False)
    one(vc_ref, pev_ref, w1v_ref, b1v_ref, w2v_ref, vo_ref, True)


def _compress(kc, vc, wk, wv, bsz, seq):
    n_grp = seq // CMP_STRIDE
    blk = pl.BlockSpec((seq, LANE), lambda b: (b, 0))
    wspecs = [_const_spec((CMP_LEN, LANE)), _const_spec((CMP_LEN, LANE, 2 * CMP_HIDDEN)),
              _const_spec((1, 2 * CMP_HIDDEN)), _const_spec((2 * CMP_HIDDEN, LANE))]
    return pl.pallas_call(
        functools.partial(_cmp_body, seq=seq),
        grid=(bsz,),
        in_specs=[blk, blk] + wspecs + wspecs,
        out_specs=[pl.BlockSpec((None, n_grp, LANE), lambda b: (b, 0, 0)),
                   pl.BlockSpec((None, LANE, n_grp), lambda b: (b, 0, 0))],
        out_shape=[jax.ShapeDtypeStruct((bsz, n_grp, LANE), BF16),
                   jax.ShapeDtypeStruct((bsz, LANE, n_grp), BF16)],
        compiler_params=_params(("parallel",)),
        name="compress",
    )(kc, vc, *wk, *wv)


def _nsa_body(qr_ref, qp_ref, kc_ref, vc_ref, ks_ref, vs_ref, kw_ref, vw_ref, ng_ref, mcs_ref, oh_ref,
              o_ref, imp_ref, cnt_ref, qa_ref, m_ref, acc_ref, res_ref, s_ref, *, n_top):
    i = pl.program_id(1)
    n_grp = kc_ref.shape[0]
    n_sel = mcs_ref.shape[0]
    t0 = i * TQ
    gates = jax.nn.sigmoid(ng_ref[...])
    lane_head = lax.broadcasted_iota(jnp.int32, (TK, LANE), 1) // NSA_DH
    rel0 = (lax.broadcasted_iota(jnp.int32, (TK, TQ), 0)
            - lax.broadcasted_iota(jnp.int32, (TK, TQ), 1))
    groups = range(NSA_KV_HEADS)

    def gate(g, kq, idx):
        r = (g * NSA_GROUP + kq) * N_GATES + idx
        return gates[r:r + 1, :]

    def load_keys(task):
        g, br, k_ref, v_ref, other, j, kind, arg = task
        rows = pl.ds(pl.multiple_of(j * TK, TK), TK)
        k_aug = jnp.where(lane_head == g, k_ref[rows, :], other(rows))
        v_t = jnp.concatenate([v_ref[g * NSA_DH:(g + 1) * NSA_DH, rows],
                               jnp.ones((ACC_ROWS - NSA_DH, TK), BF16)], axis=0)
        return k_aug, v_t

    def softmax(task, kq, s):
        g, br, k_ref, v_ref, other, j, kind, arg = task
        m_prev = m_ref[g, br, kq]
        if kind in ("causal", "lower"):
            h = TK // 2
            lo, hi = slice(0, h), slice(h, TK)
            if kind == "causal":
                tri = rel0[lo, lo] <= 0
                quads = {(0, 0): jnp.where(tri, s[lo, lo], NEG_INF), (0, 1): s[lo, hi],
                         (1, 1): jnp.where(tri, s[hi, hi], NEG_INF)}
            else:
                tri = rel0[lo, lo] > arg
                quads = {(0, 0): jnp.where(tri, s[lo, lo], NEG_INF),
                         (1, 0): s[hi, lo] + jnp.where(arg == 0, 0.0, NEG_INF),
                         (1, 1): jnp.where(tri, s[hi, hi], NEG_INF)}
            col_max = []
            for c in (0, 1):
                parts = [jnp.max(x, axis=0, keepdims=True) for (r, cc), x in quads.items() if cc == c]
                col_max.append(functools.reduce(jnp.maximum, parts))
            m_next = jnp.maximum(m_prev, jnp.concatenate(col_max, axis=1))
            m_cols = (m_next[:, lo], m_next[:, hi])
            zero = jnp.zeros((h, h), BF16)
            blocks = [[jnp.exp2(quads[(r, c)] - m_cols[c]).astype(BF16) if (r, c) in quads else zero
                       for c in (0, 1)] for r in (0, 1)]
            p = jnp.concatenate([jnp.concatenate(row, axis=1) for row in blocks], axis=0)
        else:
            if kind == "bias":
                s = s + arg
            m_next = jnp.maximum(m_prev, jnp.max(s, axis=0, keepdims=True))
            p = jnp.exp2(s - m_next).astype(BF16)
        m_ref[g, br, kq] = m_next
        return p, jnp.exp2(m_prev - m_next)

    def run_tasks(tasks, lookahead, pv_group):
        operands = [load_keys(task) for task in tasks]
        units = [(n, kq) for n in range(len(tasks)) for kq in range(NSA_GROUP)]
        def score(pos):
            n, kq = units[pos]
            s_ref[pos % SCORE_SLOTS] = jnp.dot(operands[n][0], qa_ref[tasks[n][0], kq],
                                               preferred_element_type=F32)

        for pos in range(min(lookahead, len(units))):
            score(pos)
        done = []
        for pos, (n, kq) in enumerate(units):
            if pos + lookahead < len(units):
                score(pos + lookahead)
            done.append((n, kq) + softmax(tasks[n], kq, s_ref[pos % SCORE_SLOTS]))
            if len(done) == pv_group:
                for n2, kq2, p, alpha in done:
                    g, br = tasks[n2][0], tasks[n2][1]
                    acc_ref[g, br, kq2] = alpha * acc_ref[g, br, kq2] + jnp.dot(
                        operands[n2][1], p, preferred_element_type=F32)
                done = []

    def compressed_branch(g, n_rows):
        head_rows = slice(g * NSA_DH, (g + 1) * NSA_DH)
        kc = kc_ref[0:n_rows, head_rows]
        vc_t = vc_ref[head_rows, 0:n_rows]
        c_end = (lax.broadcasted_iota(jnp.int32, (n_rows, TQ), 0) * CMP_STRIDE + (CMP_LEN - 1))
        mask = c_end <= t0 + lax.broadcasted_iota(jnp.int32, (n_rows, TQ), 1)
        heads = [g * NSA_GROUP + kq for kq in range(NSA_GROUP)]
        scores = [jnp.dot(kc, qr_ref[hh * NSA_DH:(hh + 1) * NSA_DH, :], preferred_element_type=F32)
                  for hh in heads]
        psum = jnp.zeros((n_rows, TQ), F32)
        probs = []
        for s in scores:
            s = jnp.where(mask, s, NEG_INF)
            m = jnp.max(s, axis=0, keepdims=True)
            e = jnp.exp2(s - m)
            l = jnp.sum(e, axis=0, keepdims=True)
            p = e * jnp.where(m > 0.5 * NEG_INF, 1.0 / l, 0.0)
            psum = psum + p
            probs.append(p.astype(BF16))
        for kq, hh in enumerate(heads):
            o = jnp.dot(vc_t, probs[kq], preferred_element_type=F32)
            res_ref[hh * NSA_DH:(hh + 1) * NSA_DH, :] = o * gate(g, kq, 0)
        hi = psum.astype(BF16)
        lo = (psum - hi.astype(F32)).astype(BF16)
        mcs = mcs_ref[:, 0:n_rows]
        imp = jnp.dot(mcs, hi, preferred_element_type=F32) + jnp.dot(mcs, lo, preferred_element_type=F32)
        blk = lax.broadcasted_iota(jnp.int32, (n_sel, TQ), 0)
        cur = jnp.right_shift(t0 + lax.broadcasted_iota(jnp.int32, (n_sel, TQ), 1), 6)
        valid = blk <= cur
        forced = (blk == 0) | (blk == cur) | (blk == cur - 1)
        imp_ref[...] = jnp.where(valid, imp + jnp.where(forced, FORCE_BONUS, 0.0), NEG_INF)

    for g in groups:
        first_half = i < pl.num_programs(1) // 2

        @pl.when(first_half)
        def _(g=g):
            compressed_branch(g, n_grp // 2)

        @pl.when(jnp.logical_not(first_half))
        def _(g=g):
            compressed_branch(g, n_grp)

        cnt_ref[...] = jnp.zeros_like(cnt_ref)
        per_tile = TQ // SEL_LEN
        for rnd in range(n_sel // per_tile):
            @pl.when(rnd <= i)
            def _(rnd=rnd):
                sub = 8
                contenders = [imp_ref[kk:kk + 1, :] for kk in range(rnd * per_tile, (rnd + 1) * per_tile)]
                for jg in range(n_sel // sub):
                    rows_j = slice(jg * sub, (jg + 1) * sub)
                    grp = imp_ref[rows_j, :]
                    cnt = cnt_ref[rows_j, :]
                    for off, row in enumerate(contenders):
                        kk = rnd * per_tile + off
                        if kk < jg * sub:
                            beats = row >= grp
                        elif kk >= (jg + 1) * sub:
                            beats = row > grp
                        else:
                            idx = jg * sub + lax.broadcasted_iota(jnp.int32, (sub, TQ), 0)
                            beats = (row > grp) | ((row == grp) & (idx > kk))
                        cnt = cnt + jnp.where(beats, 1.0, 0.0)
                    cnt_ref[rows_j, :] = cnt

        bias = jnp.where(cnt_ref[...] < n_top, 0.0, NEG_INF).astype(BF16)
        pad = jnp.zeros((NSA_DH - n_sel, TQ), BF16)
        for kq in range(NSA_GROUP):
            hh = g * NSA_GROUP + kq
            q_t = qp_ref[hh * NSA_DH:(hh + 1) * NSA_DH, :]
            parts = [q_t, bias, pad] if g == 0 else [bias, pad, q_t]
            qa_ref[g, kq] = jnp.concatenate([x for x in parts if x.shape[0] > 0], axis=0)

    m_ref[...] = jnp.full_like(m_ref, NEG_INF)
    acc_ref[...] = jnp.zeros_like(acc_ref)
    one_hot = lambda rows: oh_ref[rows, :]
    zeros = lambda rows: jnp.zeros((TK, LANE), BF16)

    for g in groups:
        def full_tiles(first, count, g=g):
            return [(g, 0, ks_ref, vs_ref, one_hot, first + n, None, None) for n in range(count)]

        def sel_quad(jj, carry, full_tiles=full_tiles):
            run_tasks(full_tiles(SEL_UNROLL * jj, SEL_UNROLL), LOOP_LOOKAHEAD, 1)
            return carry

        lax.fori_loop(0, i // SEL_UNROLL, sel_quad, 0)

        done_tiles = (i // SEL_UNROLL) * SEL_UNROLL
        leftover = i - done_tiles

        @pl.when(leftover >= SEL_TAIL)
        def _(full_tiles=full_tiles):
            run_tasks(full_tiles(done_tiles, SEL_TAIL), LOOP_LOOKAHEAD, 1)

        for left in range(1, SEL_TAIL):
            @pl.when(leftover % SEL_TAIL == left)
            def _(left=left, full_tiles=full_tiles):
                run_tasks(full_tiles(i - left, left), LOOP_LOOKAHEAD, 1)

    tail = []
    for g in groups:
        tail += [
            (g, 0, ks_ref, vs_ref, one_hot, i, "causal", None),
            (g, 1, kw_ref, vw_ref, zeros, jnp.maximum(i - 2, 0), "lower", jnp.where(i >= 2, 0, 2 ** 30)),
            (g, 1, kw_ref, vw_ref, zeros, jnp.maximum(i - 1, 0), "bias", jnp.where(i >= 1, 0.0, NEG_INF)),
            (g, 1, kw_ref, vw_ref, zeros, i, "causal", None),
        ]
    run_tasks(tail, LOOP_LOOKAHEAD, 1)

    for g in groups:
        for br, idx in ((0, 1), (1, 2)):
            for kq in range(NSA_GROUP):
                hh = g * NSA_GROUP + kq
                dst = slice(hh * NSA_DH, (hh + 1) * NSA_DH)
                o = acc_ref[g, br, kq, 0:NSA_DH, :] / acc_ref[g, br, kq, NSA_DH:NSA_DH + 1, :]
                res_ref[dst, :] += o * gate(g, kq, idx)

    o_ref[...] = res_ref[...].T.astype(BF16)


def _nsa(qr_t, qp_t, kcmp, vcmp_t, ks, vs_t, kw, vw_t, ng_t, mcs_t, onehot, bsz, seq):
    assert WINDOW == 2 * TK and TQ == TK and TQ % SEL_LEN == 0 and SEL_UNROLL == 2 * SEL_TAIL
    t = bsz * seq
    nq = seq // TQ
    n_grp = seq // CMP_STRIDE
    n_sel = seq // SEL_LEN
    assert n_sel <= NSA_DH and n_sel % 16 == 0
    width = NSA_HEADS * NSA_DH
    q_t = pl.BlockSpec((None, width, TQ), lambda b, i: (b, 0, i))
    kv = pl.BlockSpec((seq, LANE), lambda b, i: (b, 0))
    kv_t = pl.BlockSpec((None, LANE, seq), lambda b, i: (b, 0, 0))
    return pl.pallas_call(
        functools.partial(_nsa_body, n_top=min(SEL_TOP, n_sel)),
        grid=(bsz, nq),
        in_specs=[q_t, q_t,
                  pl.BlockSpec((None, n_grp, LANE), lambda b, i: (b, 0, 0)),
                  pl.BlockSpec((None, LANE, n_grp), lambda b, i: (b, 0, 0)),
                  kv, kv_t, kv, kv_t,
                  pl.BlockSpec((None, GATE_ROWS, TQ), lambda b, i: (b, 0, i)),
                  _const_spec((n_sel, n_grp)), _const_spec((seq, LANE))],
        out_specs=pl.BlockSpec((TQ, width), lambda b, i: (b * nq + i, 0)),
        out_shape=jax.ShapeDtypeStruct((t, width), BF16),
        scratch_shapes=[pltpu.VMEM((n_sel, TQ), F32), pltpu.VMEM((n_sel, TQ), F32),
                        pltpu.VMEM((NSA_KV_HEADS, NSA_GROUP, LANE, TQ), BF16),
                        pltpu.VMEM((NSA_KV_HEADS, 2, NSA_GROUP, 1, TQ), F32),
                        pltpu.VMEM((NSA_KV_HEADS, 2, NSA_GROUP, ACC_ROWS, TQ), F32),
                        pltpu.VMEM((width, TQ), F32), pltpu.VMEM((SCORE_SLOTS, TK, TQ), F32)],
        compiler_params=_params(("parallel", "parallel")),
        name="nsa",
    )(qr_t, qp_t, kcmp, vcmp_t, ks, vs_t, kw, vw_t, ng_t, mcs_t, onehot)


def _rope_tables(seq, rot_dim, theta, head_dim):
    half = rot_dim // 2
    inv_freq = theta ** (-2.0 * np.arange(half, dtype=np.float64) / rot_dim)
    ang = np.arange(seq, dtype=np.float64)[:, None] * inv_freq[None, :]
    cos, sin = np.cos(ang), np.sin(ang)
    rest = head_dim - rot_dim
    ones = np.ones((seq, rest))
    zr = np.zeros((seq, rest))
    zh = np.zeros((seq, half))
    cos_t = np.concatenate([cos, cos, ones], axis=1)
    sin_a = np.concatenate([-sin, zh, zr], axis=1)
    sin_b = np.concatenate([zh, sin, zr], axis=1)
    rep = LANE // head_dim
    return [jnp.asarray(np.tile(a, (1, rep)), dtype=F32) for a in (cos_t, sin_a, sin_b)]


def _retention_tables():
    c = RET_CHUNK
    gamma = 1.0 - 2.0 ** (-5.0 - np.arange(RET_HEADS, dtype=np.float64))
    log_g = np.log(gamma)
    j = np.arange(c, dtype=np.float64)
    diff = j[:, None] - j[None, :]
    dmat = np.where(diff >= 0, np.exp(log_g[:, None, None] * np.maximum(diff, 0.0)), 0.0)
    xi = np.exp(log_g[:, None] * (j[None, :] + 1.0))[:, :, None]
    zeta = np.exp(log_g[:, None] * (c - 1.0 - j[None, :]))[:, :, None]
    xi = np.broadcast_to(xi, (RET_HEADS, c, RET_DK))
    zeta = np.broadcast_to(zeta, (RET_HEADS, c, RET_DK))
    g_chunk = np.exp(log_g * c)[:, None, None]
    avg = jnp.full((RET_DV, RET_DV), 1.0 / RET_DV, BF16)
    return [jnp.asarray(a, dtype=F32) for a in (dmat, xi, zeta, g_chunk)] + [avg]


def _cmp_to_sel_t(seq):
    n_grp = seq // CMP_STRIDE
    n_sel = seq // SEL_LEN
    c_start = np.arange(n_grp) * CMP_STRIDE
    s_start = np.arange(n_sel) * SEL_LEN
    overlap = (np.minimum(c_start[None, :] + CMP_LEN, s_start[:, None] + SEL_LEN)
               - np.maximum(c_start[None, :], s_start[:, None]))
    m = np.clip(overlap, 0, None) / CMP_LEN
    m[:, n_grp - 1] = 0.0
    return jnp.asarray(m, dtype=BF16)


def _block_onehot(seq):
    n_sel = seq // SEL_LEN
    e = (np.arange(seq) // SEL_LEN)[:, None] == np.arange(NSA_DH)[None, :]
    return jnp.asarray(np.concatenate([e, e], axis=1), dtype=BF16)


def _pack_w_in(w_in):
    pad = jnp.zeros((D_MODEL, D_CAT - w_in.shape[1]), w_in.dtype)
    return jnp.concatenate([w_in, pad], axis=1).astype(BF16)


def _pack_cmp(pe, w1, b1, w2):
    pe2 = pe.reshape(CMP_LEN, NSA_KV_HEADS * NSA_DH)
    w1l = w1.reshape(CMP_LEN, NSA_DH, CMP_HIDDEN)
    z1 = jnp.zeros_like(w1l)
    w1bd = jnp.concatenate([jnp.concatenate([w1l, z1], axis=2),
                            jnp.concatenate([z1, w1l], axis=2)], axis=1).astype(BF16)
    b1bd = jnp.concatenate([b1, b1])[None, :]
    z2 = jnp.zeros_like(w2)
    w2bd = jnp.concatenate([jnp.concatenate([w2, z2], axis=1),
                            jnp.concatenate([z2, w2], axis=1)], axis=0).astype(BF16)
    return [pe2, w1bd, b1bd, w2bd]


def kernel(x, ffn1_norm_w, ffn1_w_gate, ffn1_w_up, ffn1_w_down, mix_norm_w, w_in, ret_norm_w, cmp_pe_k, cmp_k_w1, cmp_k_b1, cmp_k_w2, cmp_pe_v, cmp_v_w1, cmp_v_b1, cmp_v_w2, w_out, ffn2_norm_w, ffn2_w_gate, ffn2_w_up, ffn2_w_down, final_norm_w):
    bsz, seq, d = x.shape
    assert d == D_MODEL and seq % TM == 0 and seq % TQ == 0 and seq >= WINDOW
    assert ffn1_norm_w.shape[0] == 1, "the final norm is fused into the (single) layer's second FFN"
    t = bsz * seq
    h = x.reshape(t, d)
    n_layers = ffn1_norm_w.shape[0]
    ret_tabs = _retention_tables()
    ret_rope = _rope_tables(seq, RET_DK, RET_THETA, RET_DK)
    nsa_rope = _rope_tables(seq, ROPE_DIM, ROPE_THETA, NSA_DH)
    mcs_t = _cmp_to_sel_t(seq)
    onehot = _block_onehot(seq)
    for layer in range(n_layers):
        h1 = _ffn1(h, ffn1_norm_w[layer][None, :], ffn1_w_gate[layer], ffn1_w_up[layer], ffn1_w_down[layer])
        (rq, rk, rv, rg, qr_t, qp_t, kc, vc, ks, vs_t, kw, vw_t, ng_t) = _proj(
            h1, mix_norm_w[layer][None, :], _pack_w_in(w_in[layer]), ret_rope + nsa_rope, seq)
        ret = _retention(rq, rk, rv, rg, ret_tabs, ret_norm_w[layer][None, :], bsz, seq)
        kcmp, vcmp_t = _compress(
            kc, vc,
            _pack_cmp(cmp_pe_k[layer], cmp_k_w1[layer], cmp_k_b1[layer], cmp_k_w2[layer]),
            _pack_cmp(cmp_pe_v[layer], cmp_v_w1[layer], cmp_v_b1[layer], cmp_v_w2[layer]),
            bsz, seq)
        nsa = _nsa(qr_t, qp_t, kcmp, vcmp_t, ks, vs_t, kw, vw_t, ng_t, mcs_t, onehot, bsz, seq)
        h = _ffn2(h1, ret, nsa, w_out[layer], ffn2_norm_w[layer][None, :], ffn2_w_gate[layer],
                  ffn2_w_up[layer], ffn2_w_down[layer], final_norm_w[None, :])
    return h.reshape(bsz, seq, d)
```

```python
import functools

import numpy as np
import jax
import jax.numpy as jnp
from jax import lax
from jax.experimental import pallas as pl
from jax.experimental.pallas import tpu as pltpu

F32 = jnp.float32
BF16 = jnp.bfloat16

D_MODEL = 1024
D_FF = 2816
RMS_EPS = 1e-6
GN_EPS = 1e-5

RET_HEADS = 4
RET_DK = 64
RET_DV = 128
RET_THETA = 10000.0

NSA_HEADS = 8
NSA_KV_HEADS = 2
NSA_GROUP = 4
NSA_DH = 64
CMP_LEN = 32
CMP_STRIDE = 16
CMP_HIDDEN = 256
SEL_LEN = 64
SEL_TOP = 16
WINDOW = 512
N_GATES = 3
ROPE_THETA = 500000.0
ROPE_DIM = 16

NEG_INF = -1e30
LOG2E = 1.4426950408889634
FORCE_BONUS = 1e4

LANE = 128
VMEM_LIMIT = 56 * 1024 * 1024

TM = 512
RET_CHUNK = 128
RET_GROUP = 8
TQ = 256
TK = 256
SEL_UNROLL = 8
SEL_TAIL = 4
LOOP_LOOKAHEAD = 6
CMP_PREFIX_STEPS = 4
SCORE_SLOTS = 16
ACC_ROWS = 80
FF_SPLIT = (1280, 1536)
W_CHUNKS = 8

SEG = dict(rq=(0, 256), rk=(256, 512), rv=(512, 1024), rg=(1024, 1536), nq=(1536, 2048),
           kvc=(2048, 2304), kvs=(2304, 2560), kvw=(2560, 2816), ng=(2816, 2944))
D_CAT = 2944
GATE_ROWS = 32


def _rms(x, w):
    return x * lax.rsqrt(jnp.mean(x * x, axis=-1, keepdims=True) + RMS_EPS) * w


def _params(sem):
    return pltpu.CompilerParams(dimension_semantics=sem, vmem_limit_bytes=VMEM_LIMIT)


def _const_spec(shape):
    nd = len(shape)
    return pl.BlockSpec(shape, lambda *_: (0,) * nd, pipeline_mode=pl.Buffered(1))


def _swiglu_half_step(h, nw, wg_ref, wu_ref, wd_ref):
    hn = _rms(h, nw).astype(BF16)
    acc = None
    off = 0
    for width in FF_SPLIT:
        g = jnp.dot(hn, wg_ref[:, off:off + width], preferred_element_type=F32)
        u = jnp.dot(hn, wu_ref[:, off:off + width], preferred_element_type=F32)
        a = (g * jax.nn.sigmoid(g) * u).astype(BF16)
        y = jnp.dot(a, wd_ref[off:off + width, :], preferred_element_type=F32)
        acc = y if acc is None else acc + y
        off += width
    return h + 0.5 * acc


def _convert_weight_chunk(step, pairs):
    for src_ref, dst_ref in pairs:
        rows = src_ref.shape[0]
        dst_ref[pl.ds(pl.multiple_of(step * rows, 16), rows), :] = src_ref[...].astype(BF16)


def _ffn1_body(x_ref, nw_ref, wg_ref, wu_ref, wd_ref, o_ref, wg_bf, wu_bf, wd_bf):
    step = pl.program_id(0)

    @pl.when(step < W_CHUNKS)
    def _():
        _convert_weight_chunk(step, [(wg_ref, wg_bf), (wu_ref, wu_bf), (wd_ref, wd_bf)])

    @pl.when(step >= W_CHUNKS)
    def _():
        o_ref[...] = _swiglu_half_step(x_ref[...], nw_ref[...], wg_bf, wu_bf, wd_bf)


def _token_tile(width):
    return pl.BlockSpec((TM, width), lambda i: (jnp.maximum(i - W_CHUNKS, 0), 0))


def _weight_chunk(rows, cols):
    return pl.BlockSpec((rows // W_CHUNKS, cols), lambda i: (jnp.minimum(i, W_CHUNKS - 1), 0))


def _ffn1(x2, nw, wg, wu, wd):
    t = x2.shape[0]
    return pl.pallas_call(
        _ffn1_body,
        grid=(W_CHUNKS + t // TM,),
        in_specs=[_token_tile(D_MODEL), _const_spec((1, D_MODEL)), _weight_chunk(D_MODEL, D_FF),
                  _weight_chunk(D_MODEL, D_FF), _weight_chunk(D_FF, D_MODEL)],
        out_specs=_token_tile(D_MODEL),
        out_shape=jax.ShapeDtypeStruct((t, D_MODEL), F32),
        scratch_shapes=[pltpu.VMEM((D_MODEL, D_FF), BF16), pltpu.VMEM((D_MODEL, D_FF), BF16),
                        pltpu.VMEM((D_FF, D_MODEL), BF16)],
        compiler_params=_params(("arbitrary",)),
        name="ffn1",
    )(x2, nw, wg, wu, wd)


def _ffn2_body(h_ref, ret_ref, nsa_ref, wo_ref, nw_ref, wg_ref, wu_ref, wd_ref, fw_ref, o_ref,
               wo_bf, wg_bf, wu_bf, wd_bf):
    step = pl.program_id(0)

    @pl.when(step < W_CHUNKS)
    def _():
        _convert_weight_chunk(step, [(wo_ref, wo_bf), (wg_ref, wg_bf), (wu_ref, wu_bf), (wd_ref, wd_bf)])

    @pl.when(step >= W_CHUNKS)
    def _():
        half = RET_HEADS * RET_DV
        mix = (jnp.dot(ret_ref[...], wo_bf[:half, :], preferred_element_type=F32)
               + jnp.dot(nsa_ref[...], wo_bf[half:, :], preferred_element_type=F32))
        h = h_ref[...] + mix
        h = _swiglu_half_step(h, nw_ref[...], wg_bf, wu_bf, wd_bf)
        o_ref[...] = _rms(h, fw_ref[...])


def _ffn2(h1, ret, nsa, wo, nw, wg, wu, wd, fw):
    t = h1.shape[0]
    return pl.pallas_call(
        _ffn2_body,
        grid=(W_CHUNKS + t // TM,),
        in_specs=[_token_tile(D_MODEL), _token_tile(D_MODEL // 2), _token_tile(D_MODEL // 2),
                  _weight_chunk(D_MODEL, D_MODEL), _const_spec((1, D_MODEL)),
                  _weight_chunk(D_MODEL, D_FF), _weight_chunk(D_MODEL, D_FF),
                  _weight_chunk(D_FF, D_MODEL), _const_spec((1, D_MODEL))],
        out_specs=_token_tile(D_MODEL),
        out_shape=jax.ShapeDtypeStruct((t, D_MODEL), F32),
        scratch_shapes=[pltpu.VMEM((D_MODEL, D_MODEL), BF16), pltpu.VMEM((D_MODEL, D_FF), BF16),
                        pltpu.VMEM((D_MODEL, D_FF), BF16), pltpu.VMEM((D_FF, D_MODEL), BF16)],
        compiler_params=_params(("arbitrary",)),
        name="ffn2",
    )(h1, ret, nsa, wo, nw, wg, wu, wd, fw)


def _rope128(x, cos, sin_a, sin_b, half):
    return (x * cos + pltpu.roll(x, LANE - half, 1) * sin_a + pltpu.roll(x, half, 1) * sin_b)


def _proj_body(h_ref, nw_ref, w_ref, rc_ref, ra_ref, rb_ref, nc_ref, na_ref, nb_ref,
               rq_ref, rk_ref, rv_ref, rg_ref, qr_ref, qp_ref, kc_ref, vc_ref,
               ks_ref, vs_ref, kw_ref, vw_ref, ng_ref):
    u = _rms(h_ref[...], nw_ref[...]).astype(BF16)

    def seg(name):
        a, b = SEG[name]
        return jnp.dot(u, w_ref[:, a:b], preferred_element_type=F32)

    rc, ra, rb = rc_ref[...], ra_ref[...], rb_ref[...]
    nc, na, nb = nc_ref[...], na_ref[...], nb_ref[...]
    ret_half = RET_DK // 2
    nsa_half = ROPE_DIM // 2
    nsa_scale = NSA_DH ** -0.5 * LOG2E

    p = seg("nq")
    for j in range(4):
        sl = slice(j * LANE, (j + 1) * LANE)
        qr_ref[sl, :] = (p[:, sl] * nsa_scale).T.astype(BF16)
        qp_ref[sl, :] = (_rope128(p[:, sl], nc, na, nb, nsa_half) * nsa_scale).T.astype(BF16)
    p = seg("kvc")
    kc_ref[...] = p[:, :LANE]
    vc_ref[...] = p[:, LANE:]
    p = seg("kvs")
    ks_ref[...] = _rope128(p[:, :LANE], nc, na, nb, nsa_half).astype(BF16)
    vs_ref[...] = p[:, LANE:].T.astype(BF16)
    p = seg("kvw")
    kw_ref[...] = _rope128(p[:, :LANE], nc, na, nb, nsa_half).astype(BF16)
    vw_ref[...] = p[:, LANE:].T.astype(BF16)
    ng_ref[...] = seg("ng").T[:GATE_ROWS, :]
    p = seg("rq")
    for j in range(2):
        sl = slice(j * LANE, (j + 1) * LANE)
        rq_ref[:, sl] = _rope128(p[:, sl], rc, ra, rb, ret_half)
    p = seg("rk")
    for j in range(2):
        sl = slice(j * LANE, (j + 1) * LANE)
        rk_ref[:, sl] = _rope128(p[:, sl], rc, ra, rb, ret_half) * (RET_DK ** -0.5)
    rv_ref[...] = seg("rv").astype(BF16)
    rg_ref[...] = seg("rg")


def _proj(h1, nw, wcat, tabs, seq):
    t = h1.shape[0]
    bsz = t // seq
    per_seq = seq // TM

    def tok(width, dtype):
        return pl.BlockSpec((TM, width), lambda i: (i, 0)), jax.ShapeDtypeStruct((t, width), dtype)

    def tr(rows, dtype):
        return (pl.BlockSpec((None, rows, TM), lambda i: (i // per_seq, 0, i % per_seq)),
                jax.ShapeDtypeStruct((bsz, rows, seq), dtype))

    tab = pl.BlockSpec((TM, LANE), lambda i: (i % per_seq, 0))
    width = NSA_HEADS * NSA_DH
    outs = [tok(256, F32), tok(256, F32), tok(512, BF16), tok(512, F32),
            tr(width, BF16), tr(width, BF16),
            tok(LANE, F32), tok(LANE, F32),
            tok(LANE, BF16), tr(LANE, BF16), tok(LANE, BF16), tr(LANE, BF16),
            tr(GATE_ROWS, F32)]
    return pl.pallas_call(
        _proj_body,
        grid=(t // TM,),
        in_specs=[tok(D_MODEL, F32)[0], _const_spec((1, D_MODEL)), _const_spec((D_MODEL, D_CAT))] + [tab] * 6,
        out_specs=[o[0] for o in outs],
        out_shape=[o[1] for o in outs],
        compiler_params=_params(("parallel",)),
        name="proj",
    )(h1, nw, wcat, *tabs)


def _ret_body(q_ref, k_ref, v_ref, g_ref, dm_ref, xi_ref, ze_ref, gc_ref, avg_ref, nw_ref, o_ref, st_ref, *, seq):
    c = RET_CHUNK
    st_ref[...] = jnp.zeros_like(st_ref)

    def chunk_group(n, carry):
        heads = range(RET_HEADS)
        members = range(RET_GROUP)
        ks = [slice(h * RET_DK, (h + 1) * RET_DK) for h in heads]
        vs = [slice(h * RET_DV, (h + 1) * RET_DV) for h in heads]
        rows = [pl.ds(pl.multiple_of((n * RET_GROUP + ci) * c, c), c) for ci in members]
        q = [[q_ref[rows[ci], ks[h]] for h in heads] for ci in members]
        k = [[k_ref[rows[ci], ks[h]] for h in heads] for ci in members]
        v = [[v_ref[rows[ci], vs[h]] for h in heads] for ci in members]
        inner = [[lax.dot_general(q[ci][h].astype(BF16), k[ci][h].astype(BF16), (((1,), (1,)), ((), ())),
                                  preferred_element_type=F32) for h in heads] for ci in members]
        upd = [[lax.dot_general((k[ci][h] * ze_ref[h]).astype(BF16), v[ci][h], (((0,), (0,)), ((), ())),
                                preferred_element_type=F32) for h in heads] for ci in members]
        cross = [[None] * RET_HEADS for _ in members]
        for h in heads:
            state = st_ref[h]
            for ci in members:
                cross[ci][h] = jnp.dot((q[ci][h] * xi_ref[h]).astype(BF16), state.astype(BF16),
                                       preferred_element_type=F32)
                state = gc_ref[h] * state + upd[ci][h]
            st_ref[h] = state
        pairs = [(ci, h) for ci in members for h in heads]
        out = {u: jnp.dot((inner[u[0]][u[1]] * dm_ref[u[1]]).astype(BF16), v[u[0]][u[1]],
                          preferred_element_type=F32) + cross[u[0]][u[1]] for u in pairs}
        mu = {u: jnp.dot(out[u].astype(BF16), avg_ref[...], preferred_element_type=F32) for u in pairs}
        dev = {u: out[u] - mu[u] for u in pairs}
        var = {u: jnp.dot((dev[u] * dev[u]).astype(BF16), avg_ref[...], preferred_element_type=F32)
               for u in pairs}
        for ci, h in pairs:
            y = dev[(ci, h)] * lax.rsqrt(var[(ci, h)] + GN_EPS) * nw_ref[:, vs[h]]
            gate = g_ref[rows[ci], vs[h]]
            o_ref[rows[ci], vs[h]] = (y * (gate * jax.nn.sigmoid(gate))).astype(BF16)
        return carry

    lax.fori_loop(0, seq // (c * RET_GROUP), chunk_group, 0)


def _retention(rq, rk, rv, rg, tabs, nw, bsz, seq):
    t = bsz * seq
    c = RET_CHUNK
    assert seq % (c * RET_GROUP) == 0

    def seq_blk(width):
        return pl.BlockSpec((seq, width), lambda b: (b, 0))

    return pl.pallas_call(
        functools.partial(_ret_body, seq=seq),
        grid=(bsz,),
        in_specs=[seq_blk(256), seq_blk(256), seq_blk(512), seq_blk(512),
                  _const_spec((RET_HEADS, c, c)), _const_spec((RET_HEADS, c, RET_DK)),
                  _const_spec((RET_HEADS, c, RET_DK)), _const_spec((RET_HEADS, 1, 1)),
                  _const_spec((RET_DV, RET_DV)), _const_spec((1, RET_HEADS * RET_DV))],
        out_specs=seq_blk(512),
        out_shape=jax.ShapeDtypeStruct((t, RET_HEADS * RET_DV), BF16),
        scratch_shapes=[pltpu.VMEM((RET_HEADS, RET_DK, RET_DV), F32)],
        compiler_params=_params(("parallel",)),
        name="retention",
    )(rq, rk, rv, rg, *tabs, nw)


def _gelu_tanh(x):
    return 0.5 * x * (1.0 + jnp.tanh(np.sqrt(2.0 / np.pi).astype(np.float32) * (x + 0.044715 * (x * x * x))))


def _cmp_body(kc_ref, vc_ref, pek_ref, w1k_ref, b1k_ref, w2k_ref, pev_ref, w1v_ref, b1v_ref, w2v_ref,
              ko_ref, vo_ref, *, seq):
    n_grp = seq // CMP_STRIDE

    def one(x_ref, pe_ref, w1_ref, b1_ref, w2_ref, o_ref, transposed):
        def half_sum(base):
            total = None
            for l in range(0, CMP_STRIDE, 2):
                xs = [(x_ref[pl.ds(l + d, n_grp, stride=CMP_STRIDE), :]
                       + pe_ref[base + l + d:base + l + d + 1, :]).astype(BF16) for d in range(2)]
                part = jnp.dot(jnp.concatenate(xs, axis=1), w1_ref[(base + l) // 2],
                               preferred_element_type=F32)
                total = part if total is None else total + part
            return total

        first = half_sum(0)
        second = half_sum(CMP_STRIDE)
        pre = first + pltpu.roll(second, n_grp - 1, 0) + b1_ref[...]
        hid = _gelu_tanh(pre).astype(BF16)
        res = jnp.dot(hid, w2_ref[...], preferred_element_type=F32)
        row = lax.broadcasted_iota(jnp.int32, res.shape, 0)
        res = jnp.where(row < n_grp - 1, res, 0.0)
        if transposed:
            o_ref[...] = res.T.astype(BF16)
        else:
            o_ref[...] = res.astype(BF16)

    one(kc_ref, pek_ref, w1k_ref, b1k_ref, w2k_ref, ko_ref, False)
    one(vc_ref, pev_ref, w1v_ref, b1v_ref, w2v_ref, vo_ref, True)


def _compress(kc, vc, wk, wv, bsz, seq):
    n_grp = seq // CMP_STRIDE
    blk = pl.BlockSpec((seq, LANE), lambda b: (b, 0))
    wspecs = [_const_spec((CMP_LEN, LANE)), _const_spec((CMP_LEN // 2, 2 * LANE, 2 * CMP_HIDDEN)),
              _const_spec((1, 2 * CMP_HIDDEN)), _const_spec((2 * CMP_HIDDEN, LANE))]
    return pl.pallas_call(
        functools.partial(_cmp_body, seq=seq),
        grid=(bsz,),
        in_specs=[blk, blk] + wspecs + wspecs,
        out_specs=[pl.BlockSpec((None, n_grp, LANE), lambda b: (b, 0, 0)),
                   pl.BlockSpec((None, LANE, n_grp), lambda b: (b, 0, 0))],
        out_shape=[jax.ShapeDtypeStruct((bsz, n_grp, LANE), BF16),
                   jax.ShapeDtypeStruct((bsz, LANE, n_grp), BF16)],
        compiler_params=_params(("parallel",)),
        name="compress",
    )(kc, vc, *wk, *wv)


def _nsa_body(qr_ref, qp_ref, kc_ref, vc_ref, ks_ref, vs_ref, kw_ref, vw_ref, ng_ref, mcs_ref, oh_ref,
              o_ref, imp_ref, cnt_ref, qa_ref, m_ref, acc_ref, res_ref, s_ref, *, n_top):
    i = pl.program_id(1)
    n_grp = kc_ref.shape[0]
    n_sel = mcs_ref.shape[0]
    t0 = i * TQ
    gates = jax.nn.sigmoid(ng_ref[...])
    lane_head = lax.broadcasted_iota(jnp.int32, (TK, LANE), 1) // NSA_DH
    rel0 = (lax.broadcasted_iota(jnp.int32, (TK, TQ), 0)
            - lax.broadcasted_iota(jnp.int32, (TK, TQ), 1))
    groups = range(NSA_KV_HEADS)

    def gate(g, kq, idx):
        r = (g * NSA_GROUP + kq) * N_GATES + idx
        return gates[r:r + 1, :]

    def load_keys(task):
        g, br, k_ref, v_ref, other, j, kind, arg = task
        rows = pl.ds(pl.multiple_of(j * TK, TK), TK)
        k_aug = jnp.where(lane_head == g, k_ref[rows, :], other(rows))
        v_t = jnp.concatenate([v_ref[g * NSA_DH:(g + 1) * NSA_DH, rows],
                               jnp.ones((ACC_ROWS - NSA_DH, TK), BF16)], axis=0)
        return k_aug, v_t

    def softmax(task, kq, s):
        g, br, k_ref, v_ref, other, j, kind, arg = task
        m_prev = m_ref[g, br, kq]
        if kind in ("causal", "lower"):
            h = TK // 2
            lo, hi = slice(0, h), slice(h, TK)
            if kind == "causal":
                tri = rel0[lo, lo] <= 0
                quads = {(0, 0): jnp.where(tri, s[lo, lo], NEG_INF), (0, 1): s[lo, hi],
                         (1, 1): jnp.where(tri, s[hi, hi], NEG_INF)}
            else:
                tri = rel0[lo, lo] > arg
                quads = {(0, 0): jnp.where(tri, s[lo, lo], NEG_INF),
                         (1, 0): s[hi, lo] + jnp.where(arg == 0, 0.0, NEG_INF),
                         (1, 1): jnp.where(tri, s[hi, hi], NEG_INF)}
            col_max = []
            for c in (0, 1):
                parts = [jnp.max(x, axis=0, keepdims=True) for (r, cc), x in quads.items() if cc == c]
                col_max.append(functools.reduce(jnp.maximum, parts))
            m_next = jnp.maximum(m_prev, jnp.concatenate(col_max, axis=1))
            m_cols = (m_next[:, lo], m_next[:, hi])
            zero = jnp.zeros((h, h), BF16)
            blocks = [[jnp.exp2(quads[(r, c)] - m_cols[c]).astype(BF16) if (r, c) in quads else zero
                       for c in (0, 1)] for r in (0, 1)]
            p = jnp.concatenate([jnp.concatenate(row, axis=1) for row in blocks], axis=0)
        else:
            if kind == "bias":
                s = s + arg
            m_next = jnp.maximum(m_prev, jnp.max(s, axis=0, keepdims=True))
            p = jnp.exp2(s - m_next).astype(BF16)
        m_ref[g, br, kq] = m_next
        return p, jnp.exp2(m_prev - m_next)

    def run_tasks(tasks, lookahead, pv_group):
        operands = [load_keys(task) for task in tasks]
        units = [(n, kq) for n in range(len(tasks)) for kq in range(NSA_GROUP)]
        def score(pos):
            n, kq = units[pos]
            s_ref[pos % SCORE_SLOTS] = jnp.dot(operands[n][0], qa_ref[tasks[n][0], kq],
                                               preferred_element_type=F32)

        for pos in range(min(lookahead, len(units))):
            score(pos)
        done = []
        for pos, (n, kq) in enumerate(units):
            if pos + lookahead < len(units):
                score(pos + lookahead)
            done.append((n, kq) + softmax(tasks[n], kq, s_ref[pos % SCORE_SLOTS]))
            if len(done) == pv_group:
                for n2, kq2, p, alpha in done:
                    g, br = tasks[n2][0], tasks[n2][1]
                    acc_ref[g, br, kq2] = alpha * acc_ref[g, br, kq2] + jnp.dot(
                        operands[n2][1], p, preferred_element_type=F32)
                done = []

    def compressed_branch(n_rows):
        c_end = (lax.broadcasted_iota(jnp.int32, (n_rows, TQ), 0) * CMP_STRIDE + (CMP_LEN - 1))
        mask = c_end <= t0 + lax.broadcasted_iota(jnp.int32, (n_rows, TQ), 1)
        heads = range(NSA_HEADS)
        kv_rows = [slice(g * NSA_DH, (g + 1) * NSA_DH) for g in groups]
        scores = [jnp.dot(kc_ref[0:n_rows, kv_rows[hh // NSA_GROUP]], qr_ref[hh * NSA_DH:(hh + 1) * NSA_DH, :],
                          preferred_element_type=F32) for hh in heads]
        psum = [jnp.zeros((n_rows, TQ), F32) for _ in groups]
        probs = []
        for hh in heads:
            s = jnp.where(mask, scores[hh], NEG_INF)
            m = jnp.max(s, axis=0, keepdims=True)
            e = jnp.exp2(s - m)
            l = jnp.sum(e, axis=0, keepdims=True)
            p = e * jnp.where(m > 0.5 * NEG_INF, 1.0 / l, 0.0)
            psum[hh // NSA_GROUP] = psum[hh // NSA_GROUP] + p
            probs.append(p.astype(BF16))
        for hh in heads:
            g, kq = divmod(hh, NSA_GROUP)
            o = jnp.dot(vc_ref[kv_rows[g], 0:n_rows], probs[hh], preferred_element_type=F32)
            res_ref[hh * NSA_DH:(hh + 1) * NSA_DH, :] = o * gate(g, kq, 0)
        mcs = mcs_ref[:, 0:n_rows]
        blk = lax.broadcasted_iota(jnp.int32, (n_sel, TQ), 0)
        cur = jnp.right_shift(t0 + lax.broadcasted_iota(jnp.int32, (n_sel, TQ), 1), 6)
        valid = blk <= cur
        bonus = jnp.where((blk == 0) | (blk == cur) | (blk == cur - 1), FORCE_BONUS, 0.0)
        for g in groups:
            hi = psum[g].astype(BF16)
            lo = (psum[g] - hi.astype(F32)).astype(BF16)
            imp = jnp.dot(mcs, hi, preferred_element_type=F32) + jnp.dot(mcs, lo, preferred_element_type=F32)
            imp_ref[g] = jnp.where(valid, imp + bonus, NEG_INF)

    tiles_per_step = pl.num_programs(1) // CMP_PREFIX_STEPS
    for step in range(CMP_PREFIX_STEPS):
        @pl.when(i // tiles_per_step == step)
        def _(step=step):
            compressed_branch((step + 1) * n_grp // CMP_PREFIX_STEPS)

    cnt_ref[...] = jnp.zeros_like(cnt_ref)
    per_tile = TQ // SEL_LEN
    for rnd in range(n_sel // per_tile):
        @pl.when(rnd <= i)
        def _(rnd=rnd):
            sub = 8
            for g in groups:
                contenders = [imp_ref[g, kk:kk + 1, :] for kk in range(rnd * per_tile, (rnd + 1) * per_tile)]
                for jg in range(n_sel // sub):
                    rows_j = slice(jg * sub, (jg + 1) * sub)
                    grp = imp_ref[g, rows_j, :]
                    cnt = cnt_ref[g, rows_j, :]
                    for off, row in enumerate(contenders):
                        kk = rnd * per_tile + off
                        if kk < jg * sub:
                            beats = row >= grp
                        elif kk >= (jg + 1) * sub:
                            beats = row > grp
                        else:
                            idx = jg * sub + lax.broadcasted_iota(jnp.int32, (sub, TQ), 0)
                            beats = (row > grp) | ((row == grp) & (idx > kk))
                        cnt = cnt + jnp.where(beats, 1.0, 0.0)
                    cnt_ref[g, rows_j, :] = cnt

    for g in groups:
        bias = jnp.where(cnt_ref[g] < n_top, 0.0, NEG_INF).astype(BF16)
        pad = jnp.zeros((NSA_DH - n_sel, TQ), BF16)
        for kq in range(NSA_GROUP):
            hh = g * NSA_GROUP + kq
            q_t = qp_ref[hh * NSA_DH:(hh + 1) * NSA_DH, :]
            parts = [q_t, bias, pad] if g == 0 else [bias, pad, q_t]
            qa_ref[g, kq] = jnp.concatenate([x for x in parts if x.shape[0] > 0], axis=0)

    m_ref[...] = jnp.full_like(m_ref, NEG_INF)
    acc_ref[...] = jnp.zeros_like(acc_ref)
    one_hot = lambda rows: oh_ref[rows, :]
    zeros = lambda rows: jnp.zeros((TK, LANE), BF16)

    for g in groups:
        def full_tiles(first, count, g=g):
            return [(g, 0, ks_ref, vs_ref, one_hot, first + n, None, None) for n in range(count)]

        def sel_quad(jj, carry, full_tiles=full_tiles):
            run_tasks(full_tiles(SEL_UNROLL * jj, SEL_UNROLL), LOOP_LOOKAHEAD, 1)
            return carry

        lax.fori_loop(0, i // SEL_UNROLL, sel_quad, 0)

        done_tiles = (i // SEL_UNROLL) * SEL_UNROLL
        leftover = i - done_tiles

        @pl.when(leftover >= SEL_TAIL)
        def _(full_tiles=full_tiles):
            run_tasks(full_tiles(done_tiles, SEL_TAIL), LOOP_LOOKAHEAD, 1)

        for left in range(1, SEL_TAIL):
            @pl.when(leftover % SEL_TAIL == left)
            def _(left=left, full_tiles=full_tiles):
                run_tasks(full_tiles(i - left, left), LOOP_LOOKAHEAD, 1)

    tail = []
    for g in groups:
        tail += [
            (g, 0, ks_ref, vs_ref, one_hot, i, "causal", None),
            (g, 1, kw_ref, vw_ref, zeros, jnp.maximum(i - 2, 0), "lower", jnp.where(i >= 2, 0, 2 ** 30)),
            (g, 1, kw_ref, vw_ref, zeros, jnp.maximum(i - 1, 0), "bias", jnp.where(i >= 1, 0.0, NEG_INF)),
            (g, 1, kw_ref, vw_ref, zeros, i, "causal", None),
        ]
    run_tasks(tail, LOOP_LOOKAHEAD, 1)

    for g in groups:
        for br, idx in ((0, 1), (1, 2)):
            for kq in range(NSA_GROUP):
                hh = g * NSA_GROUP + kq
                dst = slice(hh * NSA_DH, (hh + 1) * NSA_DH)
                o = acc_ref[g, br, kq, 0:NSA_DH, :] / acc_ref[g, br, kq, NSA_DH:NSA_DH + 1, :]
                res_ref[dst, :] += o * gate(g, kq, idx)

    o_ref[...] = res_ref[...].T.astype(BF16)


def _nsa(qr_t, qp_t, kcmp, vcmp_t, ks, vs_t, kw, vw_t, ng_t, mcs_t, onehot, bsz, seq):
    assert WINDOW == 2 * TK and TQ == TK and TQ % SEL_LEN == 0 and SEL_UNROLL == 2 * SEL_TAIL
    t = bsz * seq
    nq = seq // TQ
    n_grp = seq // CMP_STRIDE
    n_sel = seq // SEL_LEN
    assert n_sel <= NSA_DH and n_sel % 16 == 0
    width = NSA_HEADS * NSA_DH
    q_t = pl.BlockSpec((None, width, TQ), lambda b, i: (b, 0, i))
    kv = pl.BlockSpec((seq, LANE), lambda b, i: (b, 0))
    kv_t = pl.BlockSpec((None, LANE, seq), lambda b, i: (b, 0, 0))
    return pl.pallas_call(
        functools.partial(_nsa_body, n_top=min(SEL_TOP, n_sel)),
        grid=(bsz, nq),
        in_specs=[q_t, q_t,
                  pl.BlockSpec((None, n_grp, LANE), lambda b, i: (b, 0, 0)),
                  pl.BlockSpec((None, LANE, n_grp), lambda b, i: (b, 0, 0)),
                  kv, kv_t, kv, kv_t,
                  pl.BlockSpec((None, GATE_ROWS, TQ), lambda b, i: (b, 0, i)),
                  _const_spec((n_sel, n_grp)), _const_spec((seq, LANE))],
        out_specs=pl.BlockSpec((TQ, width), lambda b, i: (b * nq + i, 0)),
        out_shape=jax.ShapeDtypeStruct((t, width), BF16),
        scratch_shapes=[pltpu.VMEM((NSA_KV_HEADS, n_sel, TQ), F32), pltpu.VMEM((NSA_KV_HEADS, n_sel, TQ), F32),
                        pltpu.VMEM((NSA_KV_HEADS, NSA_GROUP, LANE, TQ), BF16),
                        pltpu.VMEM((NSA_KV_HEADS, 2, NSA_GROUP, 1, TQ), F32),
                        pltpu.VMEM((NSA_KV_HEADS, 2, NSA_GROUP, ACC_ROWS, TQ), F32),
                        pltpu.VMEM((width, TQ), F32), pltpu.VMEM((SCORE_SLOTS, TK, TQ), F32)],
        compiler_params=_params(("parallel", "parallel")),
        name="nsa",
    )(qr_t, qp_t, kcmp, vcmp_t, ks, vs_t, kw, vw_t, ng_t, mcs_t, onehot)


def _rope_tables(seq, rot_dim, theta, head_dim):
    half = rot_dim // 2
    inv_freq = theta ** (-2.0 * np.arange(half, dtype=np.float64) / rot_dim)
    ang = np.arange(seq, dtype=np.float64)[:, None] * inv_freq[None, :]
    cos, sin = np.cos(ang), np.sin(ang)
    rest = head_dim - rot_dim
    ones = np.ones((seq, rest))
    zr = np.zeros((seq, rest))
    zh = np.zeros((seq, half))
    cos_t = np.concatenate([cos, cos, ones], axis=1)
    sin_a = np.concatenate([-sin, zh, zr], axis=1)
    sin_b = np.concatenate([zh, sin, zr], axis=1)
    rep = LANE // head_dim
    return [jnp.asarray(np.tile(a, (1, rep)), dtype=F32) for a in (cos_t, sin_a, sin_b)]


def _retention_tables():
    c = RET_CHUNK
    gamma = 1.0 - 2.0 ** (-5.0 - np.arange(RET_HEADS, dtype=np.float64))
    log_g = np.log(gamma)
    j = np.arange(c, dtype=np.float64)
    diff = j[:, None] - j[None, :]
    dmat = np.where(diff >= 0, np.exp(log_g[:, None, None] * np.maximum(diff, 0.0)), 0.0)
    xi = np.exp(log_g[:, None] * (j[None, :] + 1.0))[:, :, None]
    zeta = np.exp(log_g[:, None] * (c - 1.0 - j[None, :]))[:, :, None]
    xi = np.broadcast_to(xi, (RET_HEADS, c, RET_DK))
    zeta = np.broadcast_to(zeta, (RET_HEADS, c, RET_DK))
    g_chunk = np.exp(log_g * c)[:, None, None]
    avg = jnp.full((RET_DV, RET_DV), 1.0 / RET_DV, BF16)
    return [jnp.asarray(a, dtype=F32) for a in (dmat, xi, zeta, g_chunk)] + [avg]


def _cmp_to_sel_t(seq):
    n_grp = seq // CMP_STRIDE
    n_sel = seq // SEL_LEN
    c_start = np.arange(n_grp) * CMP_STRIDE
    s_start = np.arange(n_sel) * SEL_LEN
    overlap = (np.minimum(c_start[None, :] + CMP_LEN, s_start[:, None] + SEL_LEN)
               - np.maximum(c_start[None, :], s_start[:, None]))
    m = np.clip(overlap, 0, None) / CMP_LEN
    m[:, n_grp - 1] = 0.0
    return jnp.asarray(m, dtype=BF16)


def _block_onehot(seq):
    n_sel = seq // SEL_LEN
    e = (np.arange(seq) // SEL_LEN)[:, None] == np.arange(NSA_DH)[None, :]
    return jnp.asarray(np.concatenate([e, e], axis=1), dtype=BF16)


def _pack_w_in(w_in):
    pad = jnp.zeros((D_MODEL, D_CAT - w_in.shape[1]), w_in.dtype)
    return jnp.concatenate([w_in, pad], axis=1).astype(BF16)


def _pack_cmp(pe, w1, b1, w2):
    pe2 = pe.reshape(CMP_LEN, NSA_KV_HEADS * NSA_DH)
    w1l = w1.reshape(CMP_LEN, NSA_DH, CMP_HIDDEN)
    z1 = jnp.zeros_like(w1l)
    w1bd = jnp.concatenate([jnp.concatenate([w1l, z1], axis=2),
                            jnp.concatenate([z1, w1l], axis=2)], axis=1).astype(BF16)
    b1bd = jnp.concatenate([b1, b1])[None, :]
    z2 = jnp.zeros_like(w2)
    w2bd = jnp.concatenate([jnp.concatenate([w2, z2], axis=1),
                            jnp.concatenate([z2, w2], axis=1)], axis=0).astype(BF16)
    w1pair = w1bd.reshape(CMP_LEN // 2, 2 * LANE, 2 * CMP_HIDDEN)
    return [pe2, w1pair, b1bd, w2bd]


def kernel(x, ffn1_norm_w, ffn1_w_gate, ffn1_w_up, ffn1_w_down, mix_norm_w, w_in, ret_norm_w, cmp_pe_k, cmp_k_w1, cmp_k_b1, cmp_k_w2, cmp_pe_v, cmp_v_w1, cmp_v_b1, cmp_v_w2, w_out, ffn2_norm_w, ffn2_w_gate, ffn2_w_up, ffn2_w_down, final_norm_w):
    bsz, seq, d = x.shape
    assert d == D_MODEL and seq % TM == 0 and seq % TQ == 0 and seq >= WINDOW
    assert ffn1_norm_w.shape[0] == 1, "the final norm is fused into the (single) layer's second FFN"
    t = bsz * seq
    h = x.reshape(t, d)
    n_layers = ffn1_norm_w.shape[0]
    ret_tabs = _retention_tables()
    ret_rope = _rope_tables(seq, RET_DK, RET_THETA, RET_DK)
    nsa_rope = _rope_tables(seq, ROPE_DIM, ROPE_THETA, NSA_DH)
    mcs_t = _cmp_to_sel_t(seq)
    onehot = _block_onehot(seq)
    for layer in range(n_layers):
        h1 = _ffn1(h, ffn1_norm_w[layer][None, :], ffn1_w_gate[layer], ffn1_w_up[layer], ffn1_w_down[layer])
        (rq, rk, rv, rg, qr_t, qp_t, kc, vc, ks, vs_t, kw, vw_t, ng_t) = _proj(
            h1, mix_norm_w[layer][None, :], _pack_w_in(w_in[layer]), ret_rope + nsa_rope, seq)
        ret = _retention(rq, rk, rv, rg, ret_tabs, ret_norm_w[layer][None, :], bsz, seq)
        kcmp, vcmp_t = _compress(
            kc, vc,
            _pack_cmp(cmp_pe_k[layer], cmp_k_w1[layer], cmp_k_b1[layer], cmp_k_w2[layer]),
            _pack_cmp(cmp_pe_v[layer], cmp_v_w1[layer], cmp_v_b1[layer], cmp_v_w2[layer]),
            bsz, seq)
        nsa = _nsa(qr_t, qp_t, kcmp, vcmp_t, ks, vs_t, kw, vw_t, ng_t, mcs_t, onehot, bsz, seq)
        h = _ffn2(h1, ret, nsa, w_out[layer], ffn2_norm_w[layer][None, :], ffn2_w_gate[layer],
                  ffn2_w_up[layer], ffn2_w_down[layer], final_norm_w[None, :])
    return h.reshape(bsz, seq, d)
```

```python
import functools

import numpy as np
import jax
import jax.numpy as jnp
from jax import lax
from jax.experimental import pallas as pl
from jax.experimental.pallas import tpu as pltpu

F32 = jnp.float32
BF16 = jnp.bfloat16

D_MODEL = 1024
D_FF = 2816
RMS_EPS = 1e-6
GN_EPS = 1e-5

RET_HEADS = 4
RET_DK = 64
RET_DV = 128
RET_THETA = 10000.0

NSA_HEADS = 8
NSA_KV_HEADS = 2
NSA_GROUP = 4
NSA_DH = 64
CMP_LEN = 32
CMP_STRIDE = 16
CMP_HIDDEN = 256
SEL_LEN = 64
SEL_TOP = 16
WINDOW = 512
N_GATES = 3
ROPE_THETA = 500000.0
ROPE_DIM = 16

NEG_INF = -1e30
LOG2E = 1.4426950408889634
FORCE_BONUS = 1e4

LANE = 128
VMEM_LIMIT = 56 * 1024 * 1024

TM = 512
PROJ_TM = 1024
RET_CHUNK = 128
RET_GROUP = 8
TQ = 256
TK = 256
SEL_UNROLL = 8
SEL_TAIL = 4
LOOP_LOOKAHEAD = 6
CMP_PREFIX_STEPS = 4
SCORE_SLOTS = 16
ACC_ROWS = 80
FF_SPLIT = (1280, 1536)
W_CHUNKS = 8

SEG = dict(rq=(0, 256), rk=(256, 512), rv=(512, 1024), rg=(1024, 1536), nq=(1536, 2048),
           kvc=(2048, 2304), kvs=(2304, 2560), kvw=(2560, 2816), ng=(2816, 2944))
D_CAT = 2944
GATE_ROWS = 32


def _rms(x, w):
    return x * lax.rsqrt(jnp.mean(x * x, axis=-1, keepdims=True) + RMS_EPS) * w


def _params(sem):
    return pltpu.CompilerParams(dimension_semantics=sem, vmem_limit_bytes=VMEM_LIMIT)


def _const_spec(shape):
    nd = len(shape)
    return pl.BlockSpec(shape, lambda *_: (0,) * nd, pipeline_mode=pl.Buffered(1))


def _swiglu_half_step(h, nw, wg_ref, wu_ref, wd_ref):
    hw = (h * nw).astype(BF16)
    inv_rms = lax.rsqrt(jnp.mean(h * h, axis=-1, keepdims=True) + RMS_EPS)
    acc = None
    off = 0
    for width in FF_SPLIT:
        g = jnp.dot(hw, wg_ref[:, off:off + width], preferred_element_type=F32) * inv_rms
        u = jnp.dot(hw, wu_ref[:, off:off + width], preferred_element_type=F32) * inv_rms
        a = (g * jax.nn.sigmoid(g) * u).astype(BF16)
        y = jnp.dot(a, wd_ref[off:off + width, :], preferred_element_type=F32)
        acc = y if acc is None else acc + y
        off += width
    return h + 0.5 * acc


def _convert_weight_chunk(step, pairs):
    for src_ref, dst_ref in pairs:
        rows = src_ref.shape[0]
        dst_ref[pl.ds(pl.multiple_of(step * rows, 16), rows), :] = src_ref[...].astype(BF16)


def _ffn1_body(x_ref, nw_ref, wg_ref, wu_ref, wd_ref, o_ref, wg_bf, wu_bf, wd_bf):
    step = pl.program_id(0)

    @pl.when(step < W_CHUNKS)
    def _():
        _convert_weight_chunk(step, [(wg_ref, wg_bf), (wu_ref, wu_bf), (wd_ref, wd_bf)])

    @pl.when(step >= W_CHUNKS)
    def _():
        o_ref[...] = _swiglu_half_step(x_ref[...], nw_ref[...], wg_bf, wu_bf, wd_bf)


def _token_tile(width):
    return pl.BlockSpec((TM, width), lambda i: (jnp.maximum(i - W_CHUNKS, 0), 0))


def _weight_chunk(rows, cols):
    return pl.BlockSpec((rows // W_CHUNKS, cols), lambda i: (jnp.minimum(i, W_CHUNKS - 1), 0))


def _ffn1(x2, nw, wg, wu, wd):
    t = x2.shape[0]
    return pl.pallas_call(
        _ffn1_body,
        grid=(W_CHUNKS + t // TM,),
        in_specs=[_token_tile(D_MODEL), _const_spec((1, D_MODEL)), _weight_chunk(D_MODEL, D_FF),
                  _weight_chunk(D_MODEL, D_FF), _weight_chunk(D_FF, D_MODEL)],
        out_specs=_token_tile(D_MODEL),
        out_shape=jax.ShapeDtypeStruct((t, D_MODEL), F32),
        scratch_shapes=[pltpu.VMEM((D_MODEL, D_FF), BF16), pltpu.VMEM((D_MODEL, D_FF), BF16),
                        pltpu.VMEM((D_FF, D_MODEL), BF16)],
        compiler_params=_params(("arbitrary",)),
        name="ffn1",
    )(x2, nw, wg, wu, wd)


def _ffn2_body(h_ref, ret_ref, nsa_ref, wo_ref, nw_ref, wg_ref, wu_ref, wd_ref, fw_ref, o_ref,
               wo_bf, wg_bf, wu_bf, wd_bf):
    step = pl.program_id(0)

    @pl.when(step < W_CHUNKS)
    def _():
        _convert_weight_chunk(step, [(wo_ref, wo_bf), (wg_ref, wg_bf), (wu_ref, wu_bf), (wd_ref, wd_bf)])

    @pl.when(step >= W_CHUNKS)
    def _():
        half = RET_HEADS * RET_DV
        mix = (jnp.dot(ret_ref[...], wo_bf[:half, :], preferred_element_type=F32)
               + jnp.dot(nsa_ref[...], wo_bf[half:, :], preferred_element_type=F32))
        h = h_ref[...] + mix
        h = _swiglu_half_step(h, nw_ref[...], wg_bf, wu_bf, wd_bf)
        o_ref[...] = _rms(h, fw_ref[...])


def _ffn2(h1, ret, nsa, wo, nw, wg, wu, wd, fw):
    t = h1.shape[0]
    return pl.pallas_call(
        _ffn2_body,
        grid=(W_CHUNKS + t // TM,),
        in_specs=[_token_tile(D_MODEL), _token_tile(D_MODEL // 2), _token_tile(D_MODEL // 2),
                  _weight_chunk(D_MODEL, D_MODEL), _const_spec((1, D_MODEL)),
                  _weight_chunk(D_MODEL, D_FF), _weight_chunk(D_MODEL, D_FF),
                  _weight_chunk(D_FF, D_MODEL), _const_spec((1, D_MODEL))],
        out_specs=_token_tile(D_MODEL),
        out_shape=jax.ShapeDtypeStruct((t, D_MODEL), F32),
        scratch_shapes=[pltpu.VMEM((D_MODEL, D_MODEL), BF16), pltpu.VMEM((D_MODEL, D_FF), BF16),
                        pltpu.VMEM((D_MODEL, D_FF), BF16), pltpu.VMEM((D_FF, D_MODEL), BF16)],
        compiler_params=_params(("arbitrary",)),
        name="ffn2",
    )(h1, ret, nsa, wo, nw, wg, wu, wd, fw)


def _rope128(x, cos, sin_a, sin_b, half):
    return (x * cos + pltpu.roll(x, LANE - half, 1) * sin_a + pltpu.roll(x, half, 1) * sin_b)


def _proj_body(h_ref, nw_ref, w_ref, rc_ref, ra_ref, rb_ref, nc_ref, na_ref, nb_ref,
               rq_ref, rk_ref, rv_ref, rg_ref, qr_ref, qp_ref, kc_ref, vc_ref,
               ks_ref, vs_ref, kw_ref, vw_ref, ng_ref):
    u = _rms(h_ref[...], nw_ref[...]).astype(BF16)

    def seg(name):
        a, b = SEG[name]
        return jnp.dot(u, w_ref[:, a:b], preferred_element_type=F32)

    rc, ra, rb = rc_ref[...], ra_ref[...], rb_ref[...]
    nc, na, nb = nc_ref[...], na_ref[...], nb_ref[...]
    ret_half = RET_DK // 2
    nsa_half = ROPE_DIM // 2
    nsa_scale = NSA_DH ** -0.5 * LOG2E

    p = seg("nq")
    for j in range(4):
        sl = slice(j * LANE, (j + 1) * LANE)
        qr_ref[sl, :] = (p[:, sl] * nsa_scale).T.astype(BF16)
        qp_ref[sl, :] = (_rope128(p[:, sl], nc, na, nb, nsa_half) * nsa_scale).T.astype(BF16)
    p = seg("kvc")
    kc_ref[...] = p[:, :LANE]
    vc_ref[...] = p[:, LANE:]
    p = seg("kvs")
    ks_ref[...] = _rope128(p[:, :LANE], nc, na, nb, nsa_half).astype(BF16)
    vs_ref[...] = p[:, LANE:].T.astype(BF16)
    p = seg("kvw")
    kw_ref[...] = _rope128(p[:, :LANE], nc, na, nb, nsa_half).astype(BF16)
    vw_ref[...] = p[:, LANE:].T.astype(BF16)
    ng_ref[...] = seg("ng").T[:GATE_ROWS, :]
    p = seg("rq")
    for j in range(2):
        sl = slice(j * LANE, (j + 1) * LANE)
        rq_ref[:, sl] = _rope128(p[:, sl], rc, ra, rb, ret_half)
    p = seg("rk")
    for j in range(2):
        sl = slice(j * LANE, (j + 1) * LANE)
        rk_ref[:, sl] = _rope128(p[:, sl], rc, ra, rb, ret_half) * (RET_DK ** -0.5)
    rv_ref[...] = seg("rv").astype(BF16)
    rg_ref[...] = seg("rg")


def _proj(h1, nw, wcat, tabs, seq):
    t = h1.shape[0]
    bsz = t // seq
    per_seq = seq // PROJ_TM

    def tok(width, dtype):
        return pl.BlockSpec((PROJ_TM, width), lambda i: (i, 0)), jax.ShapeDtypeStruct((t, width), dtype)

    def tr(rows, dtype):
        return (pl.BlockSpec((None, rows, PROJ_TM), lambda i: (i // per_seq, 0, i % per_seq)),
                jax.ShapeDtypeStruct((bsz, rows, seq), dtype))

    tab = pl.BlockSpec((PROJ_TM, LANE), lambda i: (i % per_seq, 0))
    width = NSA_HEADS * NSA_DH
    outs = [tok(256, F32), tok(256, F32), tok(512, BF16), tok(512, F32),
            tr(width, BF16), tr(width, BF16),
            tok(LANE, F32), tok(LANE, F32),
            tok(LANE, BF16), tr(LANE, BF16), tok(LANE, BF16), tr(LANE, BF16),
            tr(GATE_ROWS, F32)]
    return pl.pallas_call(
        _proj_body,
        grid=(t // PROJ_TM,),
        in_specs=[tok(D_MODEL, F32)[0], _const_spec((1, D_MODEL)), _const_spec((D_MODEL, D_CAT))] + [tab] * 6,
        out_specs=[o[0] for o in outs],
        out_shape=[o[1] for o in outs],
        compiler_params=_params(("parallel",)),
        name="proj",
    )(h1, nw, wcat, *tabs)


def _ret_body(q_ref, k_ref, v_ref, g_ref, dm_ref, xi_ref, ze_ref, gc_ref, avg_ref, nw_ref, o_ref, st_ref, *, seq):
    c = RET_CHUNK
    st_ref[...] = jnp.zeros_like(st_ref)

    def chunk_group(n, carry):
        heads = range(RET_HEADS)
        members = range(RET_GROUP)
        ks = [slice(h * RET_DK, (h + 1) * RET_DK) for h in heads]
        vs = [slice(h * RET_DV, (h + 1) * RET_DV) for h in heads]
        rows = [pl.ds(pl.multiple_of((n * RET_GROUP + ci) * c, c), c) for ci in members]
        q = [[q_ref[rows[ci], ks[h]] for h in heads] for ci in members]
        k = [[k_ref[rows[ci], ks[h]] for h in heads] for ci in members]
        v = [[v_ref[rows[ci], vs[h]] for h in heads] for ci in members]
        inner = [[lax.dot_general(q[ci][h].astype(BF16), k[ci][h].astype(BF16), (((1,), (1,)), ((), ())),
                                  preferred_element_type=F32) for h in heads] for ci in members]
        upd = [[lax.dot_general((k[ci][h] * ze_ref[h]).astype(BF16), v[ci][h], (((0,), (0,)), ((), ())),
                                preferred_element_type=F32) for h in heads] for ci in members]
        cross = [[None] * RET_HEADS for _ in members]
        for h in heads:
            state = st_ref[h]
            for ci in members:
                cross[ci][h] = jnp.dot((q[ci][h] * xi_ref[h]).astype(BF16), state.astype(BF16),
                                       preferred_element_type=F32)
                state = gc_ref[h] * state + upd[ci][h]
            st_ref[h] = state
        pairs = [(ci, h) for ci in members for h in heads]
        out = {u: jnp.dot((inner[u[0]][u[1]] * dm_ref[u[1]]).astype(BF16), v[u[0]][u[1]],
                          preferred_element_type=F32) + cross[u[0]][u[1]] for u in pairs}
        mu = {u: jnp.dot(out[u].astype(BF16), avg_ref[...], preferred_element_type=F32) for u in pairs}
        dev = {u: out[u] - mu[u] for u in pairs}
        var = {u: jnp.dot((dev[u] * dev[u]).astype(BF16), avg_ref[...], preferred_element_type=F32)
               for u in pairs}
        for ci, h in pairs:
            y = dev[(ci, h)] * lax.rsqrt(var[(ci, h)] + GN_EPS) * nw_ref[:, vs[h]]
            gate = g_ref[rows[ci], vs[h]]
            o_ref[rows[ci], vs[h]] = (y * (gate * jax.nn.sigmoid(gate))).astype(BF16)
        return carry

    lax.fori_loop(0, seq // (c * RET_GROUP), chunk_group, 0)


def _retention(rq, rk, rv, rg, tabs, nw, bsz, seq):
    t = bsz * seq
    c = RET_CHUNK
    assert seq % (c * RET_GROUP) == 0

    def seq_blk(width):
        return pl.BlockSpec((seq, width), lambda b: (b, 0))

    return pl.pallas_call(
        functools.partial(_ret_body, seq=seq),
        grid=(bsz,),
        in_specs=[seq_blk(256), seq_blk(256), seq_blk(512), seq_blk(512),
                  _const_spec((RET_HEADS, c, c)), _const_spec((RET_HEADS, c, RET_DK)),
                  _const_spec((RET_HEADS, c, RET_DK)), _const_spec((RET_HEADS, 1, 1)),
                  _const_spec((RET_DV, RET_DV)), _const_spec((1, RET_HEADS * RET_DV))],
        out_specs=seq_blk(512),
        out_shape=jax.ShapeDtypeStruct((t, RET_HEADS * RET_DV), BF16),
        scratch_shapes=[pltpu.VMEM((RET_HEADS, RET_DK, RET_DV), F32)],
        compiler_params=_params(("parallel",)),
        name="retention",
    )(rq, rk, rv, rg, *tabs, nw)


def _gelu_tanh(x):
    return 0.5 * x * (1.0 + jnp.tanh(np.sqrt(2.0 / np.pi).astype(np.float32) * (x + 0.044715 * (x * x * x))))


def _cmp_body(kc_ref, vc_ref, pek_ref, w1k_ref, b1k_ref, w2k_ref, pev_ref, w1v_ref, b1v_ref, w2v_ref,
              ko_ref, vo_ref, *, seq):
    n_grp = seq // CMP_STRIDE

    def one(x_ref, pe_ref, w1_ref, b1_ref, w2_ref, o_ref, transposed):
        def half_sum(base):
            total = None
            for l in range(0, CMP_STRIDE, 2):
                xs = [(x_ref[pl.ds(l + d, n_grp, stride=CMP_STRIDE), :]
                       + pe_ref[base + l + d:base + l + d + 1, :]).astype(BF16) for d in range(2)]
                part = jnp.dot(jnp.concatenate(xs, axis=1), w1_ref[(base + l) // 2],
                               preferred_element_type=F32)
                total = part if total is None else total + part
            return total

        first = half_sum(0)
        second = half_sum(CMP_STRIDE)
        pre = first + pltpu.roll(second, n_grp - 1, 0) + b1_ref[...]
        hid = _gelu_tanh(pre).astype(BF16)
        res = jnp.dot(hid, w2_ref[...], preferred_element_type=F32)
        row = lax.broadcasted_iota(jnp.int32, res.shape, 0)
        res = jnp.where(row < n_grp - 1, res, 0.0)
        if transposed:
            o_ref[...] = res.T.astype(BF16)
        else:
            o_ref[...] = res.astype(BF16)

    one(kc_ref, pek_ref, w1k_ref, b1k_ref, w2k_ref, ko_ref, False)
    one(vc_ref, pev_ref, w1v_ref, b1v_ref, w2v_ref, vo_ref, True)


def _compress(kc, vc, wk, wv, bsz, seq):
    n_grp = seq // CMP_STRIDE
    blk = pl.BlockSpec((seq, LANE), lambda b: (b, 0))
    wspecs = [_const_spec((CMP_LEN, LANE)), _const_spec((CMP_LEN // 2, 2 * LANE, 2 * CMP_HIDDEN)),
              _const_spec((1, 2 * CMP_HIDDEN)), _const_spec((2 * CMP_HIDDEN, LANE))]
    return pl.pallas_call(
        functools.partial(_cmp_body, seq=seq),
        grid=(bsz,),
        in_specs=[blk, blk] + wspecs + wspecs,
        out_specs=[pl.BlockSpec((None, n_grp, LANE), lambda b: (b, 0, 0)),
                   pl.BlockSpec((None, LANE, n_grp), lambda b: (b, 0, 0))],
        out_shape=[jax.ShapeDtypeStruct((bsz, n_grp, LANE), BF16),
                   jax.ShapeDtypeStruct((bsz, LANE, n_grp), BF16)],
        compiler_params=_params(("parallel",)),
        name="compress",
    )(kc, vc, *wk, *wv)


def _nsa_body(qr_ref, qp_ref, kc_ref, vc_ref, ks_ref, vs_ref, kw_ref, vw_ref, ng_ref, mcs_ref, oh_ref,
              o_ref, imp_ref, cnt_ref, qa_ref, m_ref, acc_ref, res_ref, s_ref, *, n_top):
    i = pl.program_id(1)
    n_grp = kc_ref.shape[0]
    n_sel = mcs_ref.shape[0]
    t0 = i * TQ
    gates = jax.nn.sigmoid(ng_ref[...])
    lane_head = lax.broadcasted_iota(jnp.int32, (TK, LANE), 1) // NSA_DH
    rel0 = (lax.broadcasted_iota(jnp.int32, (TK, TQ), 0)
            - lax.broadcasted_iota(jnp.int32, (TK, TQ), 1))
    groups = range(NSA_KV_HEADS)

    def gate(g, kq, idx):
        r = (g * NSA_GROUP + kq) * N_GATES + idx
        return gates[r:r + 1, :]

    def load_keys(task):
        g, br, k_ref, v_ref, other, j, kind, arg = task
        rows = pl.ds(pl.multiple_of(j * TK, TK), TK)
        k_aug = jnp.where(lane_head == g, k_ref[rows, :], other(rows))
        v_t = jnp.concatenate([v_ref[g * NSA_DH:(g + 1) * NSA_DH, rows],
                               jnp.ones((ACC_ROWS - NSA_DH, TK), BF16)], axis=0)
        return k_aug, v_t

    def softmax(task, kq, s):
        g, br, k_ref, v_ref, other, j, kind, arg = task
        m_prev = m_ref[g, br, kq]
        if kind in ("causal", "lower"):
            h = TK // 2
            lo, hi = slice(0, h), slice(h, TK)
            if kind == "causal":
                tri = rel0[lo, lo] <= 0
                quads = {(0, 0): jnp.where(tri, s[lo, lo], NEG_INF), (0, 1): s[lo, hi],
                         (1, 1): jnp.where(tri, s[hi, hi], NEG_INF)}
            else:
                tri = rel0[lo, lo] > arg
                quads = {(0, 0): jnp.where(tri, s[lo, lo], NEG_INF),
                         (1, 0): s[hi, lo] + jnp.where(arg == 0, 0.0, NEG_INF),
                         (1, 1): jnp.where(tri, s[hi, hi], NEG_INF)}
            col_max = []
            for c in (0, 1):
                parts = [jnp.max(x, axis=0, keepdims=True) for (r, cc), x in quads.items() if cc == c]
                col_max.append(functools.reduce(jnp.maximum, parts))
            m_next = jnp.maximum(m_prev, jnp.concatenate(col_max, axis=1))
            m_cols = (m_next[:, lo], m_next[:, hi])
            zero = jnp.zeros((h, h), BF16)
            blocks = [[jnp.exp2(quads[(r, c)] - m_cols[c]).astype(BF16) if (r, c) in quads else zero
                       for c in (0, 1)] for r in (0, 1)]
            p = jnp.concatenate([jnp.concatenate(row, axis=1) for row in blocks], axis=0)
        else:
            if kind == "bias":
                s = s + arg
            m_next = jnp.maximum(m_prev, jnp.max(s, axis=0, keepdims=True))
            p = jnp.exp2(s - m_next).astype(BF16)
        m_ref[g, br, kq] = m_next
        return p, jnp.exp2(m_prev - m_next)

    def run_tasks(tasks, lookahead, pv_group):
        operands = [load_keys(task) for task in tasks]
        units = [(n, kq) for n in range(len(tasks)) for kq in range(NSA_GROUP)]
        def score(pos):
            n, kq = units[pos]
            s_ref[pos % SCORE_SLOTS] = jnp.dot(operands[n][0], qa_ref[tasks[n][0], kq],
                                               preferred_element_type=F32)

        for pos in range(min(lookahead, len(units))):
            score(pos)
        done = []
        for pos, (n, kq) in enumerate(units):
            if pos + lookahead < len(units):
                score(pos + lookahead)
            done.append((n, kq) + softmax(tasks[n], kq, s_ref[pos % SCORE_SLOTS]))
            if len(done) == pv_group:
                for n2, kq2, p, alpha in done:
                    g, br = tasks[n2][0], tasks[n2][1]
                    acc_ref[g, br, kq2] = alpha * acc_ref[g, br, kq2] + jnp.dot(
                        operands[n2][1], p, preferred_element_type=F32)
                done = []

    def compressed_branch(n_rows):
        c_end = (lax.broadcasted_iota(jnp.int32, (n_rows, TQ), 0) * CMP_STRIDE + (CMP_LEN - 1))
        mask = c_end <= t0 + lax.broadcasted_iota(jnp.int32, (n_rows, TQ), 1)
        heads = range(NSA_HEADS)
        kv_rows = [slice(g * NSA_DH, (g + 1) * NSA_DH) for g in groups]
        scores = [jnp.dot(kc_ref[0:n_rows, kv_rows[hh // NSA_GROUP]], qr_ref[hh * NSA_DH:(hh + 1) * NSA_DH, :],
                          preferred_element_type=F32) for hh in heads]
        psum = [jnp.zeros((n_rows, TQ), F32) for _ in groups]
        probs = []
        for hh in heads:
            s = jnp.where(mask, scores[hh], NEG_INF)
            m = jnp.max(s, axis=0, keepdims=True)
            e = jnp.exp2(s - m)
            l = jnp.sum(e, axis=0, keepdims=True)
            p = e * jnp.where(m > 0.5 * NEG_INF, 1.0 / l, 0.0)
            psum[hh // NSA_GROUP] = psum[hh // NSA_GROUP] + p
            probs.append(p.astype(BF16))
        for hh in heads:
            g, kq = divmod(hh, NSA_GROUP)
            o = jnp.dot(vc_ref[kv_rows[g], 0:n_rows], probs[hh], preferred_element_type=F32)
            res_ref[hh * NSA_DH:(hh + 1) * NSA_DH, :] = o * gate(g, kq, 0)
        mcs = mcs_ref[:, 0:n_rows]
        blk = lax.broadcasted_iota(jnp.int32, (n_sel, TQ), 0)
        cur = jnp.right_shift(t0 + lax.broadcasted_iota(jnp.int32, (n_sel, TQ), 1), 6)
        valid = blk <= cur
        bonus = jnp.where((blk == 0) | (blk == cur) | (blk == cur - 1), FORCE_BONUS, 0.0)
        for g in groups:
            hi = psum[g].astype(BF16)
            lo = (psum[g] - hi.astype(F32)).astype(BF16)
            imp = jnp.dot(mcs, hi, preferred_element_type=F32) + jnp.dot(mcs, lo, preferred_element_type=F32)
            imp_ref[g] = jnp.where(valid, imp + bonus, NEG_INF)

    tiles_per_step = pl.num_programs(1) // CMP_PREFIX_STEPS
    for step in range(CMP_PREFIX_STEPS):
        @pl.when(i // tiles_per_step == step)
        def _(step=step):
            compressed_branch((step + 1) * n_grp // CMP_PREFIX_STEPS)

    cnt_ref[...] = jnp.zeros_like(cnt_ref)
    per_tile = TQ // SEL_LEN
    for rnd in range(n_sel // per_tile):
        @pl.when(rnd <= i)
        def _(rnd=rnd):
            sub = 8
            for g in groups:
                contenders = [imp_ref[g, kk:kk + 1, :] for kk in range(rnd * per_tile, (rnd + 1) * per_tile)]
                for jg in range(n_sel // sub):
                    rows_j = slice(jg * sub, (jg + 1) * sub)
                    grp = imp_ref[g, rows_j, :]
                    cnt = cnt_ref[g, rows_j, :]
                    for off, row in enumerate(contenders):
                        kk = rnd * per_tile + off
                        if kk < jg * sub:
                            beats = row >= grp
                        elif kk >= (jg + 1) * sub:
                            beats = row > grp
                        else:
                            idx = jg * sub + lax.broadcasted_iota(jnp.int32, (sub, TQ), 0)
                            beats = (row > grp) | ((row == grp) & (idx > kk))
                        cnt = cnt + jnp.where(beats, 1.0, 0.0)
                    cnt_ref[g, rows_j, :] = cnt

    for g in groups:
        bias = jnp.where(cnt_ref[g] < n_top, 0.0, NEG_INF).astype(BF16)
        pad = jnp.zeros((NSA_DH - n_sel, TQ), BF16)
        for kq in range(NSA_GROUP):
            hh = g * NSA_GROUP + kq
            q_t = qp_ref[hh * NSA_DH:(hh + 1) * NSA_DH, :]
            parts = [q_t, bias, pad] if g == 0 else [bias, pad, q_t]
            qa_ref[g, kq] = jnp.concatenate([x for x in parts if x.shape[0] > 0], axis=0)

    m_ref[...] = jnp.full_like(m_ref, NEG_INF)
    acc_ref[...] = jnp.zeros_like(acc_ref)
    one_hot = lambda rows: oh_ref[rows, :]
    zeros = lambda rows: jnp.zeros((TK, LANE), BF16)

    for g in groups:
        def full_tiles(first, count, g=g):
            return [(g, 0, ks_ref, vs_ref, one_hot, first + n, None, None) for n in range(count)]

        def sel_quad(jj, carry, full_tiles=full_tiles):
            run_tasks(full_tiles(SEL_UNROLL * jj, SEL_UNROLL), LOOP_LOOKAHEAD, 1)
            return carry

        lax.fori_loop(0, i // SEL_UNROLL, sel_quad, 0)

        done_tiles = (i // SEL_UNROLL) * SEL_UNROLL
        leftover = i - done_tiles

        @pl.when(leftover >= SEL_TAIL)
        def _(full_tiles=full_tiles):
            run_tasks(full_tiles(done_tiles, SEL_TAIL), LOOP_LOOKAHEAD, 1)

        for left in range(1, SEL_TAIL):
            @pl.when(leftover % SEL_TAIL == left)
            def _(left=left, full_tiles=full_tiles):
                run_tasks(full_tiles(i - left, left), LOOP_LOOKAHEAD, 1)

    tail = []
    for g in groups:
        tail += [
            (g, 0, ks_ref, vs_ref, one_hot, i, "causal", None),
            (g, 1, kw_ref, vw_ref, zeros, jnp.maximum(i - 2, 0), "lower", jnp.where(i >= 2, 0, 2 ** 30)),
            (g, 1, kw_ref, vw_ref, zeros, jnp.maximum(i - 1, 0), "bias", jnp.where(i >= 1, 0.0, NEG_INF)),
            (g, 1, kw_ref, vw_ref, zeros, i, "causal", None),
        ]
    run_tasks(tail, LOOP_LOOKAHEAD, 1)

    for g in groups:
        for br, idx in ((0, 1), (1, 2)):
            for kq in range(NSA_GROUP):
                hh = g * NSA_GROUP + kq
                dst = slice(hh * NSA_DH, (hh + 1) * NSA_DH)
                o = acc_ref[g, br, kq, 0:NSA_DH, :] / acc_ref[g, br, kq, NSA_DH:NSA_DH + 1, :]
                res_ref[dst, :] += o * gate(g, kq, idx)

    o_ref[...] = res_ref[...].T.astype(BF16)


def _nsa(qr_t, qp_t, kcmp, vcmp_t, ks, vs_t, kw, vw_t, ng_t, mcs_t, onehot, bsz, seq):
    assert WINDOW == 2 * TK and TQ == TK and TQ % SEL_LEN == 0 and SEL_UNROLL == 2 * SEL_TAIL
    t = bsz * seq
    nq = seq // TQ
    n_grp = seq // CMP_STRIDE
    n_sel = seq // SEL_LEN
    assert n_sel <= NSA_DH and n_sel % 16 == 0
    width = NSA_HEADS * NSA_DH
    q_t = pl.BlockSpec((None, width, TQ), lambda b, i: (b, 0, i))
    kv = pl.BlockSpec((seq, LANE), lambda b, i: (b, 0))
    kv_t = pl.BlockSpec((None, LANE, seq), lambda b, i: (b, 0, 0))
    return pl.pallas_call(
        functools.partial(_nsa_body, n_top=min(SEL_TOP, n_sel)),
        grid=(bsz, nq),
        in_specs=[q_t, q_t,
                  pl.BlockSpec((None, n_grp, LANE), lambda b, i: (b, 0, 0)),
                  pl.BlockSpec((None, LANE, n_grp), lambda b, i: (b, 0, 0)),
                  kv, kv_t, kv, kv_t,
                  pl.BlockSpec((None, GATE_ROWS, TQ), lambda b, i: (b, 0, i)),
                  _const_spec((n_sel, n_grp)), _const_spec((seq, LANE))],
        out_specs=pl.BlockSpec((TQ, width), lambda b, i: (b * nq + i, 0)),
        out_shape=jax.ShapeDtypeStruct((t, width), BF16),
        scratch_shapes=[pltpu.VMEM((NSA_KV_HEADS, n_sel, TQ), F32), pltpu.VMEM((NSA_KV_HEADS, n_sel, TQ), F32),
                        pltpu.VMEM((NSA_KV_HEADS, NSA_GROUP, LANE, TQ), BF16),
                        pltpu.VMEM((NSA_KV_HEADS, 2, NSA_GROUP, 1, TQ), F32),
                        pltpu.VMEM((NSA_KV_HEADS, 2, NSA_GROUP, ACC_ROWS, TQ), F32),
                        pltpu.VMEM((width, TQ), F32), pltpu.VMEM((SCORE_SLOTS, TK, TQ), F32)],
        compiler_params=_params(("parallel", "parallel")),
        name="nsa",
    )(qr_t, qp_t, kcmp, vcmp_t, ks, vs_t, kw, vw_t, ng_t, mcs_t, onehot)


def _rope_tables(seq, rot_dim, theta, head_dim):
    half = rot_dim // 2
    inv_freq = theta ** (-2.0 * np.arange(half, dtype=np.float64) / rot_dim)
    ang = np.arange(seq, dtype=np.float64)[:, None] * inv_freq[None, :]
    cos, sin = np.cos(ang), np.sin(ang)
    rest = head_dim - rot_dim
    ones = np.ones((seq, rest))
    zr = np.zeros((seq, rest))
    zh = np.zeros((seq, half))
    cos_t = np.concatenate([cos, cos, ones], axis=1)
    sin_a = np.concatenate([-sin, zh, zr], axis=1)
    sin_b = np.concatenate([zh, sin, zr], axis=1)
    rep = LANE // head_dim
    return [jnp.asarray(np.tile(a, (1, rep)), dtype=F32) for a in (cos_t, sin_a, sin_b)]


def _retention_tables():
    c = RET_CHUNK
    gamma = 1.0 - 2.0 ** (-5.0 - np.arange(RET_HEADS, dtype=np.float64))
    log_g = np.log(gamma)
    j = np.arange(c, dtype=np.float64)
    diff = j[:, None] - j[None, :]
    dmat = np.where(diff >= 0, np.exp(log_g[:, None, None] * np.maximum(diff, 0.0)), 0.0)
    xi = np.exp(log_g[:, None] * (j[None, :] + 1.0))[:, :, None]
    zeta = np.exp(log_g[:, None] * (c - 1.0 - j[None, :]))[:, :, None]
    xi = np.broadcast_to(xi, (RET_HEADS, c, RET_DK))
    zeta = np.broadcast_to(zeta, (RET_HEADS, c, RET_DK))
    g_chunk = np.exp(log_g * c)[:, None, None]
    avg = jnp.full((RET_DV, RET_DV), 1.0 / RET_DV, BF16)
    return [jnp.asarray(a, dtype=F32) for a in (dmat, xi, zeta, g_chunk)] + [avg]


def _cmp_to_sel_t(seq):
    n_grp = seq // CMP_STRIDE
    n_sel = seq // SEL_LEN
    c_start = np.arange(n_grp) * CMP_STRIDE
    s_start = np.arange(n_sel) * SEL_LEN
    overlap = (np.minimum(c_start[None, :] + CMP_LEN, s_start[:, None] + SEL_LEN)
               - np.maximum(c_start[None, :], s_start[:, None]))
    m = np.clip(overlap, 0, None) / CMP_LEN
    m[:, n_grp - 1] = 0.0
    return jnp.asarray(m, dtype=BF16)


def _block_onehot(seq):
    n_sel = seq // SEL_LEN
    e = (np.arange(seq) // SEL_LEN)[:, None] == np.arange(NSA_DH)[None, :]
    return jnp.asarray(np.concatenate([e, e], axis=1), dtype=BF16)


def _pack_w_in(w_in):
    pad = jnp.zeros((D_MODEL, D_CAT - w_in.shape[1]), w_in.dtype)
    return jnp.concatenate([w_in, pad], axis=1).astype(BF16)


def _pack_cmp(pe, w1, b1, w2):
    pe2 = pe.reshape(CMP_LEN, NSA_KV_HEADS * NSA_DH)
    w1l = w1.reshape(CMP_LEN, NSA_DH, CMP_HIDDEN)
    z1 = jnp.zeros_like(w1l)
    w1bd = jnp.concatenate([jnp.concatenate([w1l, z1], axis=2),
                            jnp.concatenate([z1, w1l], axis=2)], axis=1).astype(BF16)
    b1bd = jnp.concatenate([b1, b1])[None, :]
    z2 = jnp.zeros_like(w2)
    w2bd = jnp.concatenate([jnp.concatenate([w2, z2], axis=1),
                            jnp.concatenate([z2, w2], axis=1)], axis=0).astype(BF16)
    w1pair = w1bd.reshape(CMP_LEN // 2, 2 * LANE, 2 * CMP_HIDDEN)
    return [pe2, w1pair, b1bd, w2bd]


def kernel(x, ffn1_norm_w, ffn1_w_gate, ffn1_w_up, ffn1_w_down, mix_norm_w, w_in, ret_norm_w, cmp_pe_k, cmp_k_w1, cmp_k_b1, cmp_k_w2, cmp_pe_v, cmp_v_w1, cmp_v_b1, cmp_v_w2, w_out, ffn2_norm_w, ffn2_w_gate, ffn2_w_up, ffn2_w_down, final_norm_w):
    bsz, seq, d = x.shape
    assert d == D_MODEL and seq % TM == 0 and seq % PROJ_TM == 0 and seq % TQ == 0 and seq >= WINDOW
    assert ffn1_norm_w.shape[0] == 1, "the final norm is fused into the (single) layer's second FFN"
    t = bsz * seq
    h = x.reshape(t, d)
    n_layers = ffn1_norm_w.shape[0]
    ret_tabs = _retention_tables()
    ret_rope = _rope_tables(seq, RET_DK, RET_THETA, RET_DK)
    nsa_rope = _rope_tables(seq, ROPE_DIM, ROPE_THETA, NSA_DH)
    mcs_t = _cmp_to_sel_t(seq)
    onehot = _block_onehot(seq)
    for layer in range(n_layers):
        h1 = _ffn1(h, ffn1_norm_w[layer][None, :], ffn1_w_gate[layer], ffn1_w_up[layer], ffn1_w_down[layer])
        (rq, rk, rv, rg, qr_t, qp_t, kc, vc, ks, vs_t, kw, vw_t, ng_t) = _proj(
            h1, mix_norm_w[layer][None, :], _pack_w_in(w_in[layer]), ret_rope + nsa_rope, seq)
        ret = _retention(rq, rk, rv, rg, ret_tabs, ret_norm_w[layer][None, :], bsz, seq)
        kcmp, vcmp_t = _compress(
            kc, vc,
            _pack_cmp(cmp_pe_k[layer], cmp_k_w1[layer], cmp_k_b1[layer], cmp_k_w2[layer]),
            _pack_cmp(cmp_pe_v[layer], cmp_v_w1[layer], cmp_v_b1[layer], cmp_v_w2[layer]),
            bsz, seq)
        nsa = _nsa(qr_t, qp_t, kcmp, vcmp_t, ks, vs_t, kw, vw_t, ng_t, mcs_t, onehot, bsz, seq)
        h = _ffn2(h1, ret, nsa, w_out[layer], ffn2_norm_w[layer][None, :], ffn2_w_gate[layer],
                  ffn2_w_up[layer], ffn2_w_down[layer], final_norm_w[None, :])
    return h.reshape(bsz, seq, d)
```

```python
import functools

import numpy as np
import jax
import jax.numpy as jnp
from jax import lax
from jax.experimental import pallas as pl
from jax.experimental.pallas import tpu as pltpu

F32 = jnp.float32
BF16 = jnp.bfloat16

D_MODEL = 1024
D_FF = 2816
RMS_EPS = 1e-6
GN_EPS = 1e-5

RET_HEADS = 4
RET_DK = 64
RET_DV = 128
RET_THETA = 10000.0

NSA_HEADS = 8
NSA_KV_HEADS = 2
NSA_GROUP = 4
NSA_DH = 64
CMP_LEN = 32
CMP_STRIDE = 16
CMP_HIDDEN = 256
SEL_LEN = 64
SEL_TOP = 16
WINDOW = 512
N_GATES = 3
ROPE_THETA = 500000.0
ROPE_DIM = 16

NEG_INF = -1e30
LOG2E = 1.4426950408889634
FORCE_BONUS = 1e4

LANE = 128
VMEM_LIMIT = 56 * 1024 * 1024

TM = 512
PROJ_TM = 1024
RET_CHUNK = 128
RET_GROUP = 8
TQ = 256
TK = 256
SEL_UNROLL = 8
SEL_TAIL = 4
LOOP_LOOKAHEAD = 6
CMP_PREFIX_STEPS = 4
SCORE_SLOTS = 16
ACC_ROWS = 80
FF_SPLIT = (1280, 1536)
W_CHUNKS = 8

SEG = dict(rq=(0, 256), rk=(256, 512), rv=(512, 1024), rg=(1024, 1536), nq=(1536, 2048),
           kvc=(2048, 2304), kvs=(2304, 2560), kvw=(2560, 2816), ng=(2816, 2944))
D_CAT = 2944
GATE_ROWS = 32


def _rms(x, w):
    return x * lax.rsqrt(jnp.mean(x * x, axis=-1, keepdims=True) + RMS_EPS) * w


def _params(sem):
    return pltpu.CompilerParams(dimension_semantics=sem, vmem_limit_bytes=VMEM_LIMIT)


def _const_spec(shape):
    nd = len(shape)
    return pl.BlockSpec(shape, lambda *_: (0,) * nd, pipeline_mode=pl.Buffered(1))


def _swiglu_half_step(h, nw, wg_ref, wu_ref, wd_ref):
    hw = (h * nw).astype(BF16)
    inv_rms = lax.rsqrt(jnp.mean(h * h, axis=-1, keepdims=True) + RMS_EPS)
    acc = None
    off = 0
    for width in FF_SPLIT:
        g = jnp.dot(hw, wg_ref[:, off:off + width], preferred_element_type=F32) * inv_rms
        u = jnp.dot(hw, wu_ref[:, off:off + width], preferred_element_type=F32) * inv_rms
        a = (g * jax.nn.sigmoid(g) * u).astype(BF16)
        y = jnp.dot(a, wd_ref[off:off + width, :], preferred_element_type=F32)
        acc = y if acc is None else acc + y
        off += width
    return h + 0.5 * acc


def _convert_weight_chunk(step, pairs):
    for src_ref, dst_ref in pairs:
        rows = src_ref.shape[0]
        dst_ref[pl.ds(pl.multiple_of(step * rows, 16), rows), :] = src_ref[...].astype(BF16)


def _ffn1_body(x_ref, nw_ref, wg_ref, wu_ref, wd_ref, o_ref, wg_bf, wu_bf, wd_bf):
    step = pl.program_id(0)

    @pl.when(step < W_CHUNKS)
    def _():
        _convert_weight_chunk(step, [(wg_ref, wg_bf), (wu_ref, wu_bf), (wd_ref, wd_bf)])

    @pl.when(step >= W_CHUNKS)
    def _():
        o_ref[...] = _swiglu_half_step(x_ref[...], nw_ref[...], wg_bf, wu_bf, wd_bf)


def _token_tile(width):
    return pl.BlockSpec((TM, width), lambda i: (jnp.maximum(i - W_CHUNKS, 0), 0))


def _weight_chunk(rows, cols):
    return pl.BlockSpec((rows // W_CHUNKS, cols), lambda i: (jnp.minimum(i, W_CHUNKS - 1), 0))


def _ffn1(x2, nw, wg, wu, wd):
    t = x2.shape[0]
    return pl.pallas_call(
        _ffn1_body,
        grid=(W_CHUNKS + t // TM,),
        in_specs=[_token_tile(D_MODEL), _const_spec((1, D_MODEL)), _weight_chunk(D_MODEL, D_FF),
                  _weight_chunk(D_MODEL, D_FF), _weight_chunk(D_FF, D_MODEL)],
        out_specs=_token_tile(D_MODEL),
        out_shape=jax.ShapeDtypeStruct((t, D_MODEL), F32),
        scratch_shapes=[pltpu.VMEM((D_MODEL, D_FF), BF16), pltpu.VMEM((D_MODEL, D_FF), BF16),
                        pltpu.VMEM((D_FF, D_MODEL), BF16)],
        compiler_params=_params(("arbitrary",)),
        name="ffn1",
    )(x2, nw, wg, wu, wd)


def _ffn2_body(h_ref, ret_ref, nsa_ref, wo_ref, nw_ref, wg_ref, wu_ref, wd_ref, fw_ref, o_ref,
               wo_bf, wg_bf, wu_bf, wd_bf):
    step = pl.program_id(0)

    @pl.when(step < W_CHUNKS)
    def _():
        _convert_weight_chunk(step, [(wo_ref, wo_bf), (wg_ref, wg_bf), (wu_ref, wu_bf), (wd_ref, wd_bf)])

    @pl.when(step >= W_CHUNKS)
    def _():
        half = RET_HEADS * RET_DV
        mix = (jnp.dot(ret_ref[...], wo_bf[:half, :], preferred_element_type=F32)
               + jnp.dot(nsa_ref[...], wo_bf[half:, :], preferred_element_type=F32))
        h = h_ref[...] + mix
        h = _swiglu_half_step(h, nw_ref[...], wg_bf, wu_bf, wd_bf)
        o_ref[...] = _rms(h, fw_ref[...])


def _ffn2(h1, ret, nsa, wo, nw, wg, wu, wd, fw):
    t = h1.shape[0]
    return pl.pallas_call(
        _ffn2_body,
        grid=(W_CHUNKS + t // TM,),
        in_specs=[_token_tile(D_MODEL), _token_tile(D_MODEL // 2), _token_tile(D_MODEL // 2),
                  _weight_chunk(D_MODEL, D_MODEL), _const_spec((1, D_MODEL)),
                  _weight_chunk(D_MODEL, D_FF), _weight_chunk(D_MODEL, D_FF),
                  _weight_chunk(D_FF, D_MODEL), _const_spec((1, D_MODEL))],
        out_specs=_token_tile(D_MODEL),
        out_shape=jax.ShapeDtypeStruct((t, D_MODEL), F32),
        scratch_shapes=[pltpu.VMEM((D_MODEL, D_MODEL), BF16), pltpu.VMEM((D_MODEL, D_FF), BF16),
                        pltpu.VMEM((D_MODEL, D_FF), BF16), pltpu.VMEM((D_FF, D_MODEL), BF16)],
        compiler_params=_params(("arbitrary",)),
        name="ffn2",
    )(h1, ret, nsa, wo, nw, wg, wu, wd, fw)


def _rope128(x, cos, sin_a, sin_b, half):
    return (x * cos + pltpu.roll(x, LANE - half, 1) * sin_a + pltpu.roll(x, half, 1) * sin_b)


def _proj_body(h_ref, nw_ref, w_ref, rc_ref, ra_ref, rb_ref, nc_ref, na_ref, nb_ref,
               rq_ref, rk_ref, rv_ref, rg_ref, qr_ref, qp_ref, kc_ref, vc_ref,
               ks_ref, vs_ref, kw_ref, vw_ref, ng_ref):
    u = _rms(h_ref[...], nw_ref[...]).astype(BF16)

    def seg(name):
        a, b = SEG[name]
        return jnp.dot(u, w_ref[:, a:b], preferred_element_type=F32)

    rc, ra, rb = rc_ref[...], ra_ref[...], rb_ref[...]
    nc, na, nb = nc_ref[...], na_ref[...], nb_ref[...]
    ret_half = RET_DK // 2
    nsa_half = ROPE_DIM // 2
    nsa_scale = NSA_DH ** -0.5 * LOG2E

    p = seg("nq")
    for j in range(4):
        sl = slice(j * LANE, (j + 1) * LANE)
        qr_ref[sl, :] = (p[:, sl] * nsa_scale).T.astype(BF16)
        qp_ref[sl, :] = (_rope128(p[:, sl], nc, na, nb, nsa_half) * nsa_scale).T.astype(BF16)
    p = seg("kvc")
    kc_ref[...] = p[:, :LANE]
    vc_ref[...] = p[:, LANE:]
    p = seg("kvs")
    ks_ref[...] = _rope128(p[:, :LANE], nc, na, nb, nsa_half).astype(BF16)
    vs_ref[...] = p[:, LANE:].T.astype(BF16)
    p = seg("kvw")
    kw_ref[...] = _rope128(p[:, :LANE], nc, na, nb, nsa_half).astype(BF16)
    vw_ref[...] = p[:, LANE:].T.astype(BF16)
    ng_ref[...] = seg("ng").T[:GATE_ROWS, :]
    p = seg("rq")
    for j in range(2):
        sl = slice(j * LANE, (j + 1) * LANE)
        rq_ref[:, sl] = _rope128(p[:, sl], rc, ra, rb, ret_half)
    p = seg("rk")
    for j in range(2):
        sl = slice(j * LANE, (j + 1) * LANE)
        rk_ref[:, sl] = _rope128(p[:, sl], rc, ra, rb, ret_half) * (RET_DK ** -0.5)
    rv_ref[...] = seg("rv").astype(BF16)
    rg_ref[...] = seg("rg")


def _proj(h1, nw, wcat, tabs, seq):
    t = h1.shape[0]
    bsz = t // seq
    per_seq = seq // PROJ_TM

    def tok(width, dtype):
        return pl.BlockSpec((PROJ_TM, width), lambda i: (i, 0)), jax.ShapeDtypeStruct((t, width), dtype)

    def tr(rows, dtype):
        return (pl.BlockSpec((None, rows, PROJ_TM), lambda i: (i // per_seq, 0, i % per_seq)),
                jax.ShapeDtypeStruct((bsz, rows, seq), dtype))

    tab = pl.BlockSpec((PROJ_TM, LANE), lambda i: (i % per_seq, 0))
    width = NSA_HEADS * NSA_DH
    outs = [tok(256, F32), tok(256, F32), tok(512, BF16), tok(512, F32),
            tr(width, BF16), tr(width, BF16),
            tok(LANE, F32), tok(LANE, F32),
            tok(LANE, BF16), tr(LANE, BF16), tok(LANE, BF16), tr(LANE, BF16),
            tr(GATE_ROWS, F32)]
    return pl.pallas_call(
        _proj_body,
        grid=(t // PROJ_TM,),
        in_specs=[tok(D_MODEL, F32)[0], _const_spec((1, D_MODEL)), _const_spec((D_MODEL, D_CAT))] + [tab] * 6,
        out_specs=[o[0] for o in outs],
        out_shape=[o[1] for o in outs],
        compiler_params=_params(("parallel",)),
        name="proj",
    )(h1, nw, wcat, *tabs)


def _ret_body(q_ref, k_ref, v_ref, g_ref, dm_ref, xi_ref, ze_ref, gc_ref, avg_ref, nw_ref, o_ref, st_ref, *, seq):
    c = RET_CHUNK
    st_ref[...] = jnp.zeros_like(st_ref)

    def chunk_group(n, carry):
        heads = range(RET_HEADS)
        members = range(RET_GROUP)
        ks = [slice(h * RET_DK, (h + 1) * RET_DK) for h in heads]
        vs = [slice(h * RET_DV, (h + 1) * RET_DV) for h in heads]
        rows = [pl.ds(pl.multiple_of((n * RET_GROUP + ci) * c, c), c) for ci in members]
        q = [[q_ref[rows[ci], ks[h]] for h in heads] for ci in members]
        k = [[k_ref[rows[ci], ks[h]] for h in heads] for ci in members]
        v = [[v_ref[rows[ci], vs[h]] for h in heads] for ci in members]
        inner = [[lax.dot_general(q[ci][h].astype(BF16), k[ci][h].astype(BF16), (((1,), (1,)), ((), ())),
                                  preferred_element_type=F32) for h in heads] for ci in members]
        upd = [[lax.dot_general((k[ci][h] * ze_ref[h]).astype(BF16), v[ci][h], (((0,), (0,)), ((), ())),
                                preferred_element_type=F32) for h in heads] for ci in members]
        cross = [[None] * RET_HEADS for _ in members]
        for h in heads:
            state = st_ref[h]
            for ci in members:
                cross[ci][h] = jnp.dot((q[ci][h] * xi_ref[h]).astype(BF16), state.astype(BF16),
                                       preferred_element_type=F32)
                state = gc_ref[h] * state + upd[ci][h]
            st_ref[h] = state
        pairs = [(ci, h) for ci in members for h in heads]
        out = {u: jnp.dot((inner[u[0]][u[1]] * dm_ref[u[1]]).astype(BF16), v[u[0]][u[1]],
                          preferred_element_type=F32) + cross[u[0]][u[1]] for u in pairs}
        mu = {u: jnp.dot(out[u].astype(BF16), avg_ref[...], preferred_element_type=F32) for u in pairs}
        dev = {u: out[u] - mu[u] for u in pairs}
        var = {u: jnp.dot((dev[u] * dev[u]).astype(BF16), avg_ref[...], preferred_element_type=F32)
               for u in pairs}
        for ci, h in pairs:
            y = dev[(ci, h)] * lax.rsqrt(var[(ci, h)] + GN_EPS) * nw_ref[:, vs[h]]
            gate = g_ref[rows[ci], vs[h]]
            o_ref[rows[ci], vs[h]] = (y * (gate * jax.nn.sigmoid(gate))).astype(BF16)
        return carry

    lax.fori_loop(0, seq // (c * RET_GROUP), chunk_group, 0)


def _retention(rq, rk, rv, rg, tabs, nw, bsz, seq):
    t = bsz * seq
    c = RET_CHUNK
    assert seq % (c * RET_GROUP) == 0

    def seq_blk(width):
        return pl.BlockSpec((seq, width), lambda b: (b, 0))

    return pl.pallas_call(
        functools.partial(_ret_body, seq=seq),
        grid=(bsz,),
        in_specs=[seq_blk(256), seq_blk(256), seq_blk(512), seq_blk(512),
                  _const_spec((RET_HEADS, c, c)), _const_spec((RET_HEADS, c, RET_DK)),
                  _const_spec((RET_HEADS, c, RET_DK)), _const_spec((RET_HEADS, 1, 1)),
                  _const_spec((RET_DV, RET_DV)), _const_spec((1, RET_HEADS * RET_DV))],
        out_specs=seq_blk(512),
        out_shape=jax.ShapeDtypeStruct((t, RET_HEADS * RET_DV), BF16),
        scratch_shapes=[pltpu.VMEM((RET_HEADS, RET_DK, RET_DV), F32)],
        compiler_params=_params(("parallel",)),
        name="retention",
    )(rq, rk, rv, rg, *tabs, nw)


def _gelu_tanh(x):
    return 0.5 * x * (1.0 + jnp.tanh(np.sqrt(2.0 / np.pi).astype(np.float32) * (x + 0.044715 * (x * x * x))))


def _cmp_body(kc_ref, vc_ref, pek_ref, w1k_ref, b1k_ref, w2k_ref, pev_ref, w1v_ref, b1v_ref, w2v_ref,
              ko_ref, vo_ref, *, seq):
    n_grp = seq // CMP_STRIDE

    def one(x_ref, pe_ref, w1_ref, b1_ref, w2_ref, o_ref, transposed):
        def half_sum(base):
            total = None
            for l in range(0, CMP_STRIDE, 2):
                xs = [(x_ref[pl.ds(l + d, n_grp, stride=CMP_STRIDE), :]
                       + pe_ref[base + l + d:base + l + d + 1, :]).astype(BF16) for d in range(2)]
                part = jnp.dot(jnp.concatenate(xs, axis=1), w1_ref[(base + l) // 2],
                               preferred_element_type=F32)
                total = part if total is None else total + part
            return total

        first = half_sum(0)
        second = half_sum(CMP_STRIDE)
        pre = first + pltpu.roll(second, n_grp - 1, 0) + b1_ref[...]
        hid = _gelu_tanh(pre).astype(BF16)
        res = jnp.dot(hid, w2_ref[...], preferred_element_type=F32)
        row = lax.broadcasted_iota(jnp.int32, res.shape, 0)
        res = jnp.where(row < n_grp - 1, res, 0.0)
        if transposed:
            o_ref[...] = res.T.astype(BF16)
        else:
            o_ref[...] = res.astype(BF16)

    one(kc_ref, pek_ref, w1k_ref, b1k_ref, w2k_ref, ko_ref, False)
    one(vc_ref, pev_ref, w1v_ref, b1v_ref, w2v_ref, vo_ref, True)


def _compress(kc, vc, wk, wv, bsz, seq):
    n_grp = seq // CMP_STRIDE
    blk = pl.BlockSpec((seq, LANE), lambda b: (b, 0))
    wspecs = [_const_spec((CMP_LEN, LANE)), _const_spec((CMP_LEN // 2, 2 * LANE, 2 * CMP_HIDDEN)),
              _const_spec((1, 2 * CMP_HIDDEN)), _const_spec((2 * CMP_HIDDEN, LANE))]
    return pl.pallas_call(
        functools.partial(_cmp_body, seq=seq),
        grid=(bsz,),
        in_specs=[blk, blk] + wspecs + wspecs,
        out_specs=[pl.BlockSpec((None, n_grp, LANE), lambda b: (b, 0, 0)),
                   pl.BlockSpec((None, LANE, n_grp), lambda b: (b, 0, 0))],
        out_shape=[jax.ShapeDtypeStruct((bsz, n_grp, LANE), BF16),
                   jax.ShapeDtypeStruct((bsz, LANE, n_grp), BF16)],
        compiler_params=_params(("parallel",)),
        name="compress",
    )(kc, vc, *wk, *wv)


def _nsa_body(qr_ref, qp_ref, kc_ref, vc_ref, ks_ref, vs_ref, kw_ref, vw_ref, ng_ref, mcs_ref, oh_ref,
              o_ref, imp_ref, cnt_ref, qa_ref, m_ref, acc_ref, res_ref, s_ref, *, n_top):
    i = pl.program_id(1)
    n_grp = kc_ref.shape[0]
    n_sel = mcs_ref.shape[0]
    t0 = i * TQ
    gates = jax.nn.sigmoid(ng_ref[...])
    lane_head = lax.broadcasted_iota(jnp.int32, (TK, LANE), 1) // NSA_DH
    rel0 = (lax.broadcasted_iota(jnp.int32, (TK, TQ), 0)
            - lax.broadcasted_iota(jnp.int32, (TK, TQ), 1))
    groups = range(NSA_KV_HEADS)

    def gate(g, kq, idx):
        r = (g * NSA_GROUP + kq) * N_GATES + idx
        return gates[r:r + 1, :]

    def load_keys(task):
        g, br, k_ref, v_ref, other, j, kind, arg = task
        rows = pl.ds(pl.multiple_of(j * TK, TK), TK)
        k_aug = jnp.where(lane_head == g, k_ref[rows, :], other(rows))
        v_t = jnp.concatenate([v_ref[g * NSA_DH:(g + 1) * NSA_DH, rows],
                               jnp.ones((ACC_ROWS - NSA_DH, TK), BF16)], axis=0)
        return k_aug, v_t

    def softmax(task, kq, s):
        g, br, k_ref, v_ref, other, j, kind, arg = task
        m_prev = m_ref[g, br, kq]
        if kind in ("causal", "lower"):
            h = TK // 2
            lo, hi = slice(0, h), slice(h, TK)
            if kind == "causal":
                tri = rel0[lo, lo] <= 0
                quads = {(0, 0): jnp.where(tri, s[lo, lo], NEG_INF), (0, 1): s[lo, hi],
                         (1, 1): jnp.where(tri, s[hi, hi], NEG_INF)}
            else:
                tri = rel0[lo, lo] > arg
                quads = {(0, 0): jnp.where(tri, s[lo, lo], NEG_INF),
                         (1, 0): s[hi, lo] + jnp.where(arg == 0, 0.0, NEG_INF),
                         (1, 1): jnp.where(tri, s[hi, hi], NEG_INF)}
            col_max = []
            for c in (0, 1):
                parts = [jnp.max(x, axis=0, keepdims=True) for (r, cc), x in quads.items() if cc == c]
                col_max.append(functools.reduce(jnp.maximum, parts))
            m_next = jnp.maximum(m_prev, jnp.concatenate(col_max, axis=1))
            m_cols = (m_next[:, lo], m_next[:, hi])
            zero = jnp.zeros((h, h), BF16)
            blocks = [[jnp.exp2(quads[(r, c)] - m_cols[c]).astype(BF16) if (r, c) in quads else zero
                       for c in (0, 1)] for r in (0, 1)]
            p = jnp.concatenate([jnp.concatenate(row, axis=1) for row in blocks], axis=0)
        else:
            if kind == "bias":
                s = s + arg
            m_next = jnp.maximum(m_prev, jnp.max(s, axis=0, keepdims=True))
            p = jnp.exp2(s - m_next).astype(BF16)
        m_ref[g, br, kq] = m_next
        return p, jnp.exp2(m_prev - m_next)

    def run_tasks(tasks, lookahead, pv_lag):
        operands = [load_keys(task) for task in tasks]
        units = [(n, kq) for n in range(len(tasks)) for kq in range(NSA_GROUP)]
        def score(pos):
            n, kq = units[pos]
            s_ref[pos % SCORE_SLOTS] = jnp.dot(operands[n][0], qa_ref[tasks[n][0], kq],
                                               preferred_element_type=F32)

        for pos in range(min(lookahead, len(units))):
            score(pos)
        def accumulate(n2, kq2, p, alpha):
            g, br = tasks[n2][0], tasks[n2][1]
            acc_ref[g, br, kq2] = alpha * acc_ref[g, br, kq2] + jnp.dot(
                operands[n2][1], p, preferred_element_type=F32)

        waiting = []
        for pos, (n, kq) in enumerate(units):
            if pos + lookahead < len(units):
                score(pos + lookahead)
            waiting.append((n, kq) + softmax(tasks[n], kq, s_ref[pos % SCORE_SLOTS]))
            if len(waiting) > pv_lag:
                accumulate(*waiting.pop(0))
        for item in waiting:
            accumulate(*item)

    def compressed_branch(n_rows):
        c_end = (lax.broadcasted_iota(jnp.int32, (n_rows, TQ), 0) * CMP_STRIDE + (CMP_LEN - 1))
        mask = c_end <= t0 + lax.broadcasted_iota(jnp.int32, (n_rows, TQ), 1)
        heads = range(NSA_HEADS)
        kv_rows = [slice(g * NSA_DH, (g + 1) * NSA_DH) for g in groups]
        scores = [jnp.dot(kc_ref[0:n_rows, kv_rows[hh // NSA_GROUP]], qr_ref[hh * NSA_DH:(hh + 1) * NSA_DH, :],
                          preferred_element_type=F32) for hh in heads]
        psum = [jnp.zeros((n_rows, TQ), F32) for _ in groups]
        probs = []
        for hh in heads:
            s = jnp.where(mask, scores[hh], NEG_INF)
            m = jnp.max(s, axis=0, keepdims=True)
            e = jnp.exp2(s - m)
            l = jnp.sum(e, axis=0, keepdims=True)
            p = e * jnp.where(m > 0.5 * NEG_INF, 1.0 / l, 0.0)
            psum[hh // NSA_GROUP] = psum[hh // NSA_GROUP] + p
            probs.append(p.astype(BF16))
        for hh in heads:
            g, kq = divmod(hh, NSA_GROUP)
            o = jnp.dot(vc_ref[kv_rows[g], 0:n_rows], probs[hh], preferred_element_type=F32)
            res_ref[hh * NSA_DH:(hh + 1) * NSA_DH, :] = o * gate(g, kq, 0)
        mcs = mcs_ref[:, 0:n_rows]
        blk = lax.broadcasted_iota(jnp.int32, (n_sel, TQ), 0)
        cur = jnp.right_shift(t0 + lax.broadcasted_iota(jnp.int32, (n_sel, TQ), 1), 6)
        valid = blk <= cur
        bonus = jnp.where((blk == 0) | (blk == cur) | (blk == cur - 1), FORCE_BONUS, 0.0)
        for g in groups:
            hi = psum[g].astype(BF16)
            lo = (psum[g] - hi.astype(F32)).astype(BF16)
            imp = jnp.dot(mcs, hi, preferred_element_type=F32) + jnp.dot(mcs, lo, preferred_element_type=F32)
            imp_ref[g] = jnp.where(valid, imp + bonus, NEG_INF)

    tiles_per_step = pl.num_programs(1) // CMP_PREFIX_STEPS
    for step in range(CMP_PREFIX_STEPS):
        @pl.when(i // tiles_per_step == step)
        def _(step=step):
            compressed_branch((step + 1) * n_grp // CMP_PREFIX_STEPS)

    cnt_ref[...] = jnp.zeros_like(cnt_ref)
    per_tile = TQ // SEL_LEN
    for rnd in range(n_sel // per_tile):
        @pl.when(rnd <= i)
        def _(rnd=rnd):
            sub = 8
            for g in groups:
                contenders = [imp_ref[g, kk:kk + 1, :] for kk in range(rnd * per_tile, (rnd + 1) * per_tile)]
                for jg in range(n_sel // sub):
                    rows_j = slice(jg * sub, (jg + 1) * sub)
                    grp = imp_ref[g, rows_j, :]
                    cnt = cnt_ref[g, rows_j, :]
                    for off, row in enumerate(contenders):
                        kk = rnd * per_tile + off
                        if kk < jg * sub:
                            beats = row >= grp
                        elif kk >= (jg + 1) * sub:
                            beats = row > grp
                        else:
                            idx = jg * sub + lax.broadcasted_iota(jnp.int32, (sub, TQ), 0)
                            beats = (row > grp) | ((row == grp) & (idx > kk))
                        cnt = cnt + jnp.where(beats, 1.0, 0.0)
                    cnt_ref[g, rows_j, :] = cnt

    for g in groups:
        bias = jnp.where(cnt_ref[g] < n_top, 0.0, NEG_INF).astype(BF16)
        pad = jnp.zeros((NSA_DH - n_sel, TQ), BF16)
        for kq in range(NSA_GROUP):
            hh = g * NSA_GROUP + kq
            q_t = qp_ref[hh * NSA_DH:(hh + 1) * NSA_DH, :]
            parts = [q_t, bias, pad] if g == 0 else [bias, pad, q_t]
            qa_ref[g, kq] = jnp.concatenate([x for x in parts if x.shape[0] > 0], axis=0)

    m_ref[...] = jnp.full_like(m_ref, NEG_INF)
    acc_ref[...] = jnp.zeros_like(acc_ref)
    one_hot = lambda rows: oh_ref[rows, :]
    zeros = lambda rows: jnp.zeros((TK, LANE), BF16)

    for g in groups:
        def full_tiles(first, count, g=g):
            return [(g, 0, ks_ref, vs_ref, one_hot, first + n, None, None) for n in range(count)]

        def sel_quad(jj, carry, full_tiles=full_tiles):
            run_tasks(full_tiles(SEL_UNROLL * jj, SEL_UNROLL), LOOP_LOOKAHEAD, 1)
            return carry

        lax.fori_loop(0, i // SEL_UNROLL, sel_quad, 0)

        done_tiles = (i // SEL_UNROLL) * SEL_UNROLL
        leftover = i - done_tiles

        @pl.when(leftover >= SEL_TAIL)
        def _(full_tiles=full_tiles):
            run_tasks(full_tiles(done_tiles, SEL_TAIL), LOOP_LOOKAHEAD, 1)

        for left in range(1, SEL_TAIL):
            @pl.when(leftover % SEL_TAIL == left)
            def _(left=left, full_tiles=full_tiles):
                run_tasks(full_tiles(i - left, left), LOOP_LOOKAHEAD, 1)

    tail = []
    for g in groups:
        tail += [
            (g, 0, ks_ref, vs_ref, one_hot, i, "causal", None),
            (g, 1, kw_ref, vw_ref, zeros, jnp.maximum(i - 2, 0), "lower", jnp.where(i >= 2, 0, 2 ** 30)),
            (g, 1, kw_ref, vw_ref, zeros, jnp.maximum(i - 1, 0), "bias", jnp.where(i >= 1, 0.0, NEG_INF)),
            (g, 1, kw_ref, vw_ref, zeros, i, "causal", None),
        ]
    run_tasks(tail, LOOP_LOOKAHEAD, 1)

    for g in groups:
        for br, idx in ((0, 1), (1, 2)):
            for kq in range(NSA_GROUP):
                hh = g * NSA_GROUP + kq
                dst = slice(hh * NSA_DH, (hh + 1) * NSA_DH)
                o = acc_ref[g, br, kq, 0:NSA_DH, :] / acc_ref[g, br, kq, NSA_DH:NSA_DH + 1, :]
                res_ref[dst, :] += o * gate(g, kq, idx)

    o_ref[...] = res_ref[...].T.astype(BF16)


def _nsa(qr_t, qp_t, kcmp, vcmp_t, ks, vs_t, kw, vw_t, ng_t, mcs_t, onehot, bsz, seq):
    assert WINDOW == 2 * TK and TQ == TK and TQ % SEL_LEN == 0 and SEL_UNROLL == 2 * SEL_TAIL
    t = bsz * seq
    nq = seq // TQ
    n_grp = seq // CMP_STRIDE
    n_sel = seq // SEL_LEN
    assert n_sel <= NSA_DH and n_sel % 16 == 0
    width = NSA_HEADS * NSA_DH
    q_t = pl.BlockSpec((None, width, TQ), lambda b, i: (b, 0, i))
    kv = pl.BlockSpec((seq, LANE), lambda b, i: (b, 0))
    kv_t = pl.BlockSpec((None, LANE, seq), lambda b, i: (b, 0, 0))
    return pl.pallas_call(
        functools.partial(_nsa_body, n_top=min(SEL_TOP, n_sel)),
        grid=(bsz, nq),
        in_specs=[q_t, q_t,
                  pl.BlockSpec((None, n_grp, LANE), lambda b, i: (b, 0, 0)),
                  pl.BlockSpec((None, LANE, n_grp), lambda b, i: (b, 0, 0)),
                  kv, kv_t, kv, kv_t,
                  pl.BlockSpec((None, GATE_ROWS, TQ), lambda b, i: (b, 0, i)),
                  _const_spec((n_sel, n_grp)), _const_spec((seq, LANE))],
        out_specs=pl.BlockSpec((TQ, width), lambda b, i: (b * nq + i, 0)),
        out_shape=jax.ShapeDtypeStruct((t, width), BF16),
        scratch_shapes=[pltpu.VMEM((NSA_KV_HEADS, n_sel, TQ), F32), pltpu.VMEM((NSA_KV_HEADS, n_sel, TQ), F32),
                        pltpu.VMEM((NSA_KV_HEADS, NSA_GROUP, LANE, TQ), BF16),
                        pltpu.VMEM((NSA_KV_HEADS, 2, NSA_GROUP, 1, TQ), F32),
                        pltpu.VMEM((NSA_KV_HEADS, 2, NSA_GROUP, ACC_ROWS, TQ), F32),
                        pltpu.VMEM((width, TQ), F32), pltpu.VMEM((SCORE_SLOTS, TK, TQ), F32)],
        compiler_params=_params(("parallel", "parallel")),
        name="nsa",
    )(qr_t, qp_t, kcmp, vcmp_t, ks, vs_t, kw, vw_t, ng_t, mcs_t, onehot)


def _rope_tables(seq, rot_dim, theta, head_dim):
    half = rot_dim // 2
    inv_freq = theta ** (-2.0 * np.arange(half, dtype=np.float64) / rot_dim)
    ang = np.arange(seq, dtype=np.float64)[:, None] * inv_freq[None, :]
    cos, sin = np.cos(ang), np.sin(ang)
    rest = head_dim - rot_dim
    ones = np.ones((seq, rest))
    zr = np.zeros((seq, rest))
    zh = np.zeros((seq, half))
    cos_t = np.concatenate([cos, cos, ones], axis=1)
    sin_a = np.concatenate([-sin, zh, zr], axis=1)
    sin_b = np.concatenate([zh, sin, zr], axis=1)
    rep = LANE // head_dim
    return [jnp.asarray(np.tile(a, (1, rep)), dtype=F32) for a in (cos_t, sin_a, sin_b)]


def _retention_tables():
    c = RET_CHUNK
    gamma = 1.0 - 2.0 ** (-5.0 - np.arange(RET_HEADS, dtype=np.float64))
    log_g = np.log(gamma)
    j = np.arange(c, dtype=np.float64)
    diff = j[:, None] - j[None, :]
    dmat = np.where(diff >= 0, np.exp(log_g[:, None, None] * np.maximum(diff, 0.0)), 0.0)
    xi = np.exp(log_g[:, None] * (j[None, :] + 1.0))[:, :, None]
    zeta = np.exp(log_g[:, None] * (c - 1.0 - j[None, :]))[:, :, None]
    xi = np.broadcast_to(xi, (RET_HEADS, c, RET_DK))
    zeta = np.broadcast_to(zeta, (RET_HEADS, c, RET_DK))
    g_chunk = np.exp(log_g * c)[:, None, None]
    avg = jnp.full((RET_DV, RET_DV), 1.0 / RET_DV, BF16)
    return [jnp.asarray(a, dtype=F32) for a in (dmat, xi, zeta, g_chunk)] + [avg]


def _cmp_to_sel_t(seq):
    n_grp = seq // CMP_STRIDE
    n_sel = seq // SEL_LEN
    c_start = np.arange(n_grp) * CMP_STRIDE
    s_start = np.arange(n_sel) * SEL_LEN
    overlap = (np.minimum(c_start[None, :] + CMP_LEN, s_start[:, None] + SEL_LEN)
               - np.maximum(c_start[None, :], s_start[:, None]))
    m = np.clip(overlap, 0, None) / CMP_LEN
    m[:, n_grp - 1] = 0.0
    return jnp.asarray(m, dtype=BF16)


def _block_onehot(seq):
    n_sel = seq // SEL_LEN
    e = (np.arange(seq) // SEL_LEN)[:, None] == np.arange(NSA_DH)[None, :]
    return jnp.asarray(np.concatenate([e, e], axis=1), dtype=BF16)


def _pack_w_in(w_in):
    pad = jnp.zeros((D_MODEL, D_CAT - w_in.shape[1]), w_in.dtype)
    return jnp.concatenate([w_in, pad], axis=1).astype(BF16)


def _pack_cmp(pe, w1, b1, w2):
    pe2 = pe.reshape(CMP_LEN, NSA_KV_HEADS * NSA_DH)
    w1l = w1.reshape(CMP_LEN, NSA_DH, CMP_HIDDEN)
    z1 = jnp.zeros_like(w1l)
    w1bd = jnp.concatenate([jnp.concatenate([w1l, z1], axis=2),
                            jnp.concatenate([z1, w1l], axis=2)], axis=1).astype(BF16)
    b1bd = jnp.concatenate([b1, b1])[None, :]
    z2 = jnp.zeros_like(w2)
    w2bd = jnp.concatenate([jnp.concatenate([w2, z2], axis=1),
                            jnp.concatenate([z2, w2], axis=1)], axis=0).astype(BF16)
    w1pair = w1bd.reshape(CMP_LEN // 2, 2 * LANE, 2 * CMP_HIDDEN)
    return [pe2, w1pair, b1bd, w2bd]


def kernel(x, ffn1_norm_w, ffn1_w_gate, ffn1_w_up, ffn1_w_down, mix_norm_w, w_in, ret_norm_w, cmp_pe_k, cmp_k_w1, cmp_k_b1, cmp_k_w2, cmp_pe_v, cmp_v_w1, cmp_v_b1, cmp_v_w2, w_out, ffn2_norm_w, ffn2_w_gate, ffn2_w_up, ffn2_w_down, final_norm_w):
    bsz, seq, d = x.shape
    assert d == D_MODEL and seq % TM == 0 and seq % PROJ_TM == 0 and seq % TQ == 0 and seq >= WINDOW
    assert ffn1_norm_w.shape[0] == 1, "the final norm is fused into the (single) layer's second FFN"
    t = bsz * seq
    h = x.reshape(t, d)
    n_layers = ffn1_norm_w.shape[0]
    ret_tabs = _retention_tables()
    ret_rope = _rope_tables(seq, RET_DK, RET_THETA, RET_DK)
    nsa_rope = _rope_tables(seq, ROPE_DIM, ROPE_THETA, NSA_DH)
    mcs_t = _cmp_to_sel_t(seq)
    onehot = _block_onehot(seq)
    for layer in range(n_layers):
        h1 = _ffn1(h, ffn1_norm_w[layer][None, :], ffn1_w_gate[layer], ffn1_w_up[layer], ffn1_w_down[layer])
        (rq, rk, rv, rg, qr_t, qp_t, kc, vc, ks, vs_t, kw, vw_t, ng_t) = _proj(
            h1, mix_norm_w[layer][None, :], _pack_w_in(w_in[layer]), ret_rope + nsa_rope, seq)
        ret = _retention(rq, rk, rv, rg, ret_tabs, ret_norm_w[layer][None, :], bsz, seq)
        kcmp, vcmp_t = _compress(
            kc, vc,
            _pack_cmp(cmp_pe_k[layer], cmp_k_w1[layer], cmp_k_b1[layer], cmp_k_w2[layer]),
            _pack_cmp(cmp_pe_v[layer], cmp_v_w1[layer], cmp_v_b1[layer], cmp_v_w2[layer]),
            bsz, seq)
        nsa = _nsa(qr_t, qp_t, kcmp, vcmp_t, ks, vs_t, kw, vw_t, ng_t, mcs_t, onehot, bsz, seq)
        h = _ffn2(h1, ret, nsa, w_out[layer], ffn2_norm_w[layer][None, :], ffn2_w_gate[layer],
                  ffn2_w_up[layer], ffn2_w_down[layer], final_norm_w[None, :])
    return h.reshape(bsz, seq, d)
```

```python
import functools

import numpy as np
import jax
import jax.numpy as jnp
from jax import lax
from jax.experimental import pallas as pl
from jax.experimental.pallas import tpu as pltpu

F32 = jnp.float32
BF16 = jnp.bfloat16

D_MODEL = 1024
D_FF = 2816
RMS_EPS = 1e-6
GN_EPS = 1e-5

RET_HEADS = 4
RET_DK = 64
RET_DV = 128
RET_THETA = 10000.0

NSA_HEADS = 8
NSA_KV_HEADS = 2
NSA_GROUP = 4
NSA_DH = 64
CMP_LEN = 32
CMP_STRIDE = 16
CMP_HIDDEN = 256
SEL_LEN = 64
SEL_TOP = 16
WINDOW = 512
N_GATES = 3
ROPE_THETA = 500000.0
ROPE_DIM = 16

NEG_INF = -1e30
LOG2E = 1.4426950408889634
FORCE_BONUS = 1e4

LANE = 128
VMEM_LIMIT = 56 * 1024 * 1024

TM = 512
PROJ_TM = 1024
RET_CHUNK = 128
RET_GROUP = 8
TQ = 256
TK = 256
SEL_UNROLL = 8
SEL_TAIL = 4
LOOP_LOOKAHEAD = 6
CMP_PREFIX_STEPS = 4
SCORE_SLOTS = 16
ACC_ROWS = 80
FF_SPLIT = (1280, 1536)
W_CHUNKS = 8

SEG = dict(rq=(0, 256), rk=(256, 512), rv=(512, 1024), rg=(1024, 1536), nq=(1536, 2048),
           kvc=(2048, 2304), kvs=(2304, 2560), kvw=(2560, 2816), ng=(2816, 2944))
D_CAT = 2944
GATE_ROWS = 32


def _rms(x, w):
    return x * lax.rsqrt(jnp.mean(x * x, axis=-1, keepdims=True) + RMS_EPS) * w


def _params(sem):
    return pltpu.CompilerParams(dimension_semantics=sem, vmem_limit_bytes=VMEM_LIMIT)


def _const_spec(shape):
    nd = len(shape)
    return pl.BlockSpec(shape, lambda *_: (0,) * nd, pipeline_mode=pl.Buffered(1))


def _swiglu_half_step(h, nw, wg_ref, wu_ref, wd_ref):
    hw = (h * nw).astype(BF16)
    inv_rms = lax.rsqrt(jnp.mean(h * h, axis=-1, keepdims=True) + RMS_EPS)
    acc = None
    off = 0
    for width in FF_SPLIT:
        g = jnp.dot(hw, wg_ref[:, off:off + width], preferred_element_type=F32) * inv_rms
        u = jnp.dot(hw, wu_ref[:, off:off + width], preferred_element_type=F32) * inv_rms
        a = (g * jax.nn.sigmoid(g) * u).astype(BF16)
        y = jnp.dot(a, wd_ref[off:off + width, :], preferred_element_type=F32)
        acc = y if acc is None else acc + y
        off += width
    return h + 0.5 * acc


def _convert_weight_chunk(step, pairs):
    for src_ref, dst_ref in pairs:
        rows = src_ref.shape[0]
        dst_ref[pl.ds(pl.multiple_of(step * rows, 16), rows), :] = src_ref[...].astype(BF16)


def _ffn1_body(x_ref, nw_ref, wg_ref, wu_ref, wd_ref, o_ref, wg_bf, wu_bf, wd_bf):
    step = pl.program_id(0)

    @pl.when(step < W_CHUNKS)
    def _():
        _convert_weight_chunk(step, [(wg_ref, wg_bf), (wu_ref, wu_bf), (wd_ref, wd_bf)])

    @pl.when(step >= W_CHUNKS)
    def _():
        o_ref[...] = _swiglu_half_step(x_ref[...], nw_ref[...], wg_bf, wu_bf, wd_bf)


def _token_tile(width):
    return pl.BlockSpec((TM, width), lambda i: (jnp.maximum(i - W_CHUNKS, 0), 0))


def _weight_chunk(rows, cols):
    return pl.BlockSpec((rows // W_CHUNKS, cols), lambda i: (jnp.minimum(i, W_CHUNKS - 1), 0))


def _ffn1(x2, nw, wg, wu, wd):
    t = x2.shape[0]
    return pl.pallas_call(
        _ffn1_body,
        grid=(W_CHUNKS + t // TM,),
        in_specs=[_token_tile(D_MODEL), _const_spec((1, D_MODEL)), _weight_chunk(D_MODEL, D_FF),
                  _weight_chunk(D_MODEL, D_FF), _weight_chunk(D_FF, D_MODEL)],
        out_specs=_token_tile(D_MODEL),
        out_shape=jax.ShapeDtypeStruct((t, D_MODEL), F32),
        scratch_shapes=[pltpu.VMEM((D_MODEL, D_FF), BF16), pltpu.VMEM((D_MODEL, D_FF), BF16),
                        pltpu.VMEM((D_FF, D_MODEL), BF16)],
        compiler_params=_params(("arbitrary",)),
        name="ffn1",
    )(x2, nw, wg, wu, wd)


def _ffn2_body(h_ref, ret_ref, nsa_ref, wo_ref, nw_ref, wg_ref, wu_ref, wd_ref, fw_ref, o_ref,
               wo_bf, wg_bf, wu_bf, wd_bf):
    step = pl.program_id(0)

    @pl.when(step < W_CHUNKS)
    def _():
        _convert_weight_chunk(step, [(wo_ref, wo_bf), (wg_ref, wg_bf), (wu_ref, wu_bf), (wd_ref, wd_bf)])

    @pl.when(step >= W_CHUNKS)
    def _():
        half = RET_HEADS * RET_DV
        mix = (jnp.dot(ret_ref[...], wo_bf[:half, :], preferred_element_type=F32)
               + jnp.dot(nsa_ref[...], wo_bf[half:, :], preferred_element_type=F32))
        h = h_ref[...] + mix
        h = _swiglu_half_step(h, nw_ref[...], wg_bf, wu_bf, wd_bf)
        o_ref[...] = _rms(h, fw_ref[...])


def _ffn2(h1, ret, nsa, wo, nw, wg, wu, wd, fw):
    t = h1.shape[0]
    return pl.pallas_call(
        _ffn2_body,
        grid=(W_CHUNKS + t // TM,),
        in_specs=[_token_tile(D_MODEL), _token_tile(D_MODEL // 2), _token_tile(D_MODEL // 2),
                  _weight_chunk(D_MODEL, D_MODEL), _const_spec((1, D_MODEL)),
                  _weight_chunk(D_MODEL, D_FF), _weight_chunk(D_MODEL, D_FF),
                  _weight_chunk(D_FF, D_MODEL), _const_spec((1, D_MODEL))],
        out_specs=_token_tile(D_MODEL),
        out_shape=jax.ShapeDtypeStruct((t, D_MODEL), F32),
        scratch_shapes=[pltpu.VMEM((D_MODEL, D_MODEL), BF16), pltpu.VMEM((D_MODEL, D_FF), BF16),
                        pltpu.VMEM((D_MODEL, D_FF), BF16), pltpu.VMEM((D_FF, D_MODEL), BF16)],
        compiler_params=_params(("arbitrary",)),
        name="ffn2",
    )(h1, ret, nsa, wo, nw, wg, wu, wd, fw)


def _rope128(x, cos, sin_a, sin_b, half):
    return (x * cos + pltpu.roll(x, LANE - half, 1) * sin_a + pltpu.roll(x, half, 1) * sin_b)


def _proj_body(h_ref, nw_ref, w_ref, rc_ref, ra_ref, rb_ref, nc_ref, na_ref, nb_ref,
               rq_ref, rk_ref, rv_ref, rg_ref, qr_ref, qp_ref, kc_ref, vc_ref,
               ks_ref, vs_ref, kw_ref, vw_ref, ng_ref):
    u = _rms(h_ref[...], nw_ref[...]).astype(BF16)

    def seg(name):
        a, b = SEG[name]
        return jnp.dot(u, w_ref[:, a:b], preferred_element_type=F32)

    rc, ra, rb = rc_ref[...], ra_ref[...], rb_ref[...]
    nc, na, nb = nc_ref[...], na_ref[...], nb_ref[...]
    ret_half = RET_DK // 2
    nsa_half = ROPE_DIM // 2
    nsa_scale = NSA_DH ** -0.5 * LOG2E

    p = seg("nq")
    for j in range(4):
        sl = slice(j * LANE, (j + 1) * LANE)
        raw_t = (p[:, sl] * nsa_scale).T.astype(BF16)
        rot_t = (_rope128(p[:, sl], nc, na, nb, nsa_half) * nsa_scale).T.astype(BF16)
        for n in range(PROJ_TM // TQ):
            cols = slice(n * TQ, (n + 1) * TQ)
            qr_ref[n, sl, :] = raw_t[:, cols]
            qp_ref[n, sl, :] = rot_t[:, cols]
    p = seg("kvc")
    kc_ref[...] = p[:, :LANE]
    vc_ref[...] = p[:, LANE:]
    p = seg("kvs")
    ks_ref[...] = _rope128(p[:, :LANE], nc, na, nb, nsa_half).astype(BF16)
    vs_ref[...] = p[:, LANE:].T.astype(BF16)
    p = seg("kvw")
    kw_ref[...] = _rope128(p[:, :LANE], nc, na, nb, nsa_half).astype(BF16)
    vw_ref[...] = p[:, LANE:].T.astype(BF16)
    ng_ref[...] = seg("ng").T[:GATE_ROWS, :]
    p = seg("rq")
    for j in range(2):
        sl = slice(j * LANE, (j + 1) * LANE)
        rq_ref[:, sl] = _rope128(p[:, sl], rc, ra, rb, ret_half)
    p = seg("rk")
    for j in range(2):
        sl = slice(j * LANE, (j + 1) * LANE)
        rk_ref[:, sl] = _rope128(p[:, sl], rc, ra, rb, ret_half) * (RET_DK ** -0.5)
    rv_ref[...] = seg("rv").astype(BF16)
    rg_ref[...] = seg("rg")


def _proj(h1, nw, wcat, tabs, seq):
    t = h1.shape[0]
    bsz = t // seq
    per_seq = seq // PROJ_TM

    def tok(width, dtype):
        return pl.BlockSpec((PROJ_TM, width), lambda i: (i, 0)), jax.ShapeDtypeStruct((t, width), dtype)

    def tr(rows, dtype):
        return (pl.BlockSpec((None, rows, PROJ_TM), lambda i: (i // per_seq, 0, i % per_seq)),
                jax.ShapeDtypeStruct((bsz, rows, seq), dtype))

    tab = pl.BlockSpec((PROJ_TM, LANE), lambda i: (i % per_seq, 0))
    width = NSA_HEADS * NSA_DH
    per_tm = PROJ_TM // TQ
    q_tiles = (pl.BlockSpec((None, per_tm, width, TQ), lambda i: (i // per_seq, i % per_seq, 0, 0)),
               jax.ShapeDtypeStruct((bsz, seq // TQ, width, TQ), BF16))
    outs = [tok(256, F32), tok(256, F32), tok(512, BF16), tok(512, F32),
            q_tiles, q_tiles,
            tok(LANE, F32), tok(LANE, F32),
            tok(LANE, BF16), tr(LANE, BF16), tok(LANE, BF16), tr(LANE, BF16),
            tr(GATE_ROWS, F32)]
    return pl.pallas_call(
        _proj_body,
        grid=(t // PROJ_TM,),
        in_specs=[tok(D_MODEL, F32)[0], _const_spec((1, D_MODEL)), _const_spec((D_MODEL, D_CAT))] + [tab] * 6,
        out_specs=[o[0] for o in outs],
        out_shape=[o[1] for o in outs],
        compiler_params=_params(("parallel",)),
        name="proj",
    )(h1, nw, wcat, *tabs)


def _ret_body(q_ref, k_ref, v_ref, g_ref, dm_ref, xi_ref, ze_ref, gc_ref, avg_ref, nw_ref, o_ref, st_ref, *, seq):
    c = RET_CHUNK
    st_ref[...] = jnp.zeros_like(st_ref)

    def chunk_group(n, carry):
        heads = range(RET_HEADS)
        members = range(RET_GROUP)
        ks = [slice(h * RET_DK, (h + 1) * RET_DK) for h in heads]
        vs = [slice(h * RET_DV, (h + 1) * RET_DV) for h in heads]
        rows = [pl.ds(pl.multiple_of((n * RET_GROUP + ci) * c, c), c) for ci in members]
        q = [[q_ref[rows[ci], ks[h]] for h in heads] for ci in members]
        k = [[k_ref[rows[ci], ks[h]] for h in heads] for ci in members]
        v = [[v_ref[rows[ci], vs[h]] for h in heads] for ci in members]
        inner = [[lax.dot_general(q[ci][h].astype(BF16), k[ci][h].astype(BF16), (((1,), (1,)), ((), ())),
                                  preferred_element_type=F32) for h in heads] for ci in members]
        upd = [[lax.dot_general((k[ci][h] * ze_ref[h]).astype(BF16), v[ci][h], (((0,), (0,)), ((), ())),
                                preferred_element_type=F32) for h in heads] for ci in members]
        cross = [[None] * RET_HEADS for _ in members]
        for h in heads:
            state = st_ref[h]
            for ci in members:
                cross[ci][h] = jnp.dot((q[ci][h] * xi_ref[h]).astype(BF16), state.astype(BF16),
                                       preferred_element_type=F32)
                state = gc_ref[h] * state + upd[ci][h]
            st_ref[h] = state
        pairs = [(ci, h) for ci in members for h in heads]
        out = {u: jnp.dot((inner[u[0]][u[1]] * dm_ref[u[1]]).astype(BF16), v[u[0]][u[1]],
                          preferred_element_type=F32) + cross[u[0]][u[1]] for u in pairs}
        mu = {u: jnp.dot(out[u].astype(BF16), avg_ref[...], preferred_element_type=F32) for u in pairs}
        dev = {u: out[u] - mu[u] for u in pairs}
        var = {u: jnp.dot((dev[u] * dev[u]).astype(BF16), avg_ref[...], preferred_element_type=F32)
               for u in pairs}
        for ci, h in pairs:
            y = dev[(ci, h)] * lax.rsqrt(var[(ci, h)] + GN_EPS) * nw_ref[:, vs[h]]
            gate = g_ref[rows[ci], vs[h]]
            o_ref[rows[ci], vs[h]] = (y * (gate * jax.nn.sigmoid(gate))).astype(BF16)
        return carry

    lax.fori_loop(0, seq // (c * RET_GROUP), chunk_group, 0)


def _retention(rq, rk, rv, rg, tabs, nw, bsz, seq):
    t = bsz * seq
    c = RET_CHUNK
    assert seq % (c * RET_GROUP) == 0

    def seq_blk(width):
        return pl.BlockSpec((seq, width), lambda b: (b, 0))

    return pl.pallas_call(
        functools.partial(_ret_body, seq=seq),
        grid=(bsz,),
        in_specs=[seq_blk(256), seq_blk(256), seq_blk(512), seq_blk(512),
                  _const_spec((RET_HEADS, c, c)), _const_spec((RET_HEADS, c, RET_DK)),
                  _const_spec((RET_HEADS, c, RET_DK)), _const_spec((RET_HEADS, 1, 1)),
                  _const_spec((RET_DV, RET_DV)), _const_spec((1, RET_HEADS * RET_DV))],
        out_specs=seq_blk(512),
        out_shape=jax.ShapeDtypeStruct((t, RET_HEADS * RET_DV), BF16),
        scratch_shapes=[pltpu.VMEM((RET_HEADS, RET_DK, RET_DV), F32)],
        compiler_params=_params(("parallel",)),
        name="retention",
    )(rq, rk, rv, rg, *tabs, nw)


def _gelu_tanh(x):
    return 0.5 * x * (1.0 + jnp.tanh(np.sqrt(2.0 / np.pi).astype(np.float32) * (x + 0.044715 * (x * x * x))))


def _cmp_body(kc_ref, vc_ref, pek_ref, w1k_ref, b1k_ref, w2k_ref, pev_ref, w1v_ref, b1v_ref, w2v_ref,
              ko_ref, vo_ref, *, seq):
    n_grp = seq // CMP_STRIDE

    def one(x_ref, pe_ref, w1_ref, b1_ref, w2_ref, o_ref, transposed):
        def half_sum(base):
            total = None
            for l in range(0, CMP_STRIDE, 2):
                xs = [(x_ref[pl.ds(l + d, n_grp, stride=CMP_STRIDE), :]
                       + pe_ref[base + l + d:base + l + d + 1, :]).astype(BF16) for d in range(2)]
                part = jnp.dot(jnp.concatenate(xs, axis=1), w1_ref[(base + l) // 2],
                               preferred_element_type=F32)
                total = part if total is None else total + part
            return total

        first = half_sum(0)
        second = half_sum(CMP_STRIDE)
        pre = first + pltpu.roll(second, n_grp - 1, 0) + b1_ref[...]
        hid = _gelu_tanh(pre).astype(BF16)
        res = jnp.dot(hid, w2_ref[...], preferred_element_type=F32)
        row = lax.broadcasted_iota(jnp.int32, res.shape, 0)
        res = jnp.where(row < n_grp - 1, res, 0.0)
        if transposed:
            o_ref[...] = res.T.astype(BF16)
        else:
            o_ref[...] = res.astype(BF16)

    one(kc_ref, pek_ref, w1k_ref, b1k_ref, w2k_ref, ko_ref, False)
    one(vc_ref, pev_ref, w1v_ref, b1v_ref, w2v_ref, vo_ref, True)


def _compress(kc, vc, wk, wv, bsz, seq):
    n_grp = seq // CMP_STRIDE
    blk = pl.BlockSpec((seq, LANE), lambda b: (b, 0))
    wspecs = [_const_spec((CMP_LEN, LANE)), _const_spec((CMP_LEN // 2, 2 * LANE, 2 * CMP_HIDDEN)),
              _const_spec((1, 2 * CMP_HIDDEN)), _const_spec((2 * CMP_HIDDEN, LANE))]
    return pl.pallas_call(
        functools.partial(_cmp_body, seq=seq),
        grid=(bsz,),
        in_specs=[blk, blk] + wspecs + wspecs,
        out_specs=[pl.BlockSpec((None, n_grp, LANE), lambda b: (b, 0, 0)),
                   pl.BlockSpec((None, LANE, n_grp), lambda b: (b, 0, 0))],
        out_shape=[jax.ShapeDtypeStruct((bsz, n_grp, LANE), BF16),
                   jax.ShapeDtypeStruct((bsz, LANE, n_grp), BF16)],
        compiler_params=_params(("parallel",)),
        name="compress",
    )(kc, vc, *wk, *wv)


def _nsa_body(qr_ref, qp_ref, kc_ref, vc_ref, ks_ref, vs_ref, kw_ref, vw_ref, ng_ref, mcs_ref, oh_ref,
              o_ref, imp_ref, cnt_ref, qa_ref, m_ref, acc_ref, res_ref, s_ref, *, n_top):
    i = pl.program_id(1)
    n_grp = kc_ref.shape[0]
    n_sel = mcs_ref.shape[0]
    t0 = i * TQ
    gates = jax.nn.sigmoid(ng_ref[...])
    lane_head = lax.broadcasted_iota(jnp.int32, (TK, LANE), 1) // NSA_DH
    rel0 = (lax.broadcasted_iota(jnp.int32, (TK, TQ), 0)
            - lax.broadcasted_iota(jnp.int32, (TK, TQ), 1))
    groups = range(NSA_KV_HEADS)

    def gate(g, kq, idx):
        r = (g * NSA_GROUP + kq) * N_GATES + idx
        return gates[r:r + 1, :]

    def load_keys(task):
        g, br, k_ref, v_ref, other, j, kind, arg = task
        rows = pl.ds(pl.multiple_of(j * TK, TK), TK)
        k_aug = jnp.where(lane_head == g, k_ref[rows, :], other(rows))
        v_t = jnp.concatenate([v_ref[g * NSA_DH:(g + 1) * NSA_DH, rows],
                               jnp.ones((ACC_ROWS - NSA_DH, TK), BF16)], axis=0)
        return k_aug, v_t

    def softmax(task, kq, s):
        g, br, k_ref, v_ref, other, j, kind, arg = task
        m_prev = m_ref[g, br, kq]
        if kind in ("causal", "lower"):
            h = TK // 2
            lo, hi = slice(0, h), slice(h, TK)
            if kind == "causal":
                tri = rel0[lo, lo] <= 0
                quads = {(0, 0): jnp.where(tri, s[lo, lo], NEG_INF), (0, 1): s[lo, hi],
                         (1, 1): jnp.where(tri, s[hi, hi], NEG_INF)}
            else:
                tri = rel0[lo, lo] > arg
                quads = {(0, 0): jnp.where(tri, s[lo, lo], NEG_INF),
                         (1, 0): s[hi, lo] + jnp.where(arg == 0, 0.0, NEG_INF),
                         (1, 1): jnp.where(tri, s[hi, hi], NEG_INF)}
            col_max = []
            for c in (0, 1):
                parts = [jnp.max(x, axis=0, keepdims=True) for (r, cc), x in quads.items() if cc == c]
                col_max.append(functools.reduce(jnp.maximum, parts))
            m_next = jnp.maximum(m_prev, jnp.concatenate(col_max, axis=1))
            m_cols = (m_next[:, lo], m_next[:, hi])
            zero = jnp.zeros((h, h), BF16)
            blocks = [[jnp.exp2(quads[(r, c)] - m_cols[c]).astype(BF16) if (r, c) in quads else zero
                       for c in (0, 1)] for r in (0, 1)]
            p = jnp.concatenate([jnp.concatenate(row, axis=1) for row in blocks], axis=0)
        else:
            if kind == "bias":
                s = s + arg
            m_next = jnp.maximum(m_prev, jnp.max(s, axis=0, keepdims=True))
            p = jnp.exp2(s - m_next).astype(BF16)
        m_ref[g, br, kq] = m_next
        return p, jnp.exp2(m_prev - m_next)

    def run_tasks(tasks, lookahead, pv_lag):
        operands = [load_keys(task) for task in tasks]
        units = [(n, kq) for n in range(len(tasks)) for kq in range(NSA_GROUP)]
        def score(pos):
            n, kq = units[pos]
            s_ref[pos % SCORE_SLOTS] = jnp.dot(operands[n][0], qa_ref[tasks[n][0], kq],
                                               preferred_element_type=F32)

        for pos in range(min(lookahead, len(units))):
            score(pos)
        def accumulate(n2, kq2, p, alpha):
            g, br = tasks[n2][0], tasks[n2][1]
            acc_ref[g, br, kq2] = alpha * acc_ref[g, br, kq2] + jnp.dot(
                operands[n2][1], p, preferred_element_type=F32)

        waiting = []
        for pos, (n, kq) in enumerate(units):
            if pos + lookahead < len(units):
                score(pos + lookahead)
            waiting.append((n, kq) + softmax(tasks[n], kq, s_ref[pos % SCORE_SLOTS]))
            if len(waiting) > pv_lag:
                accumulate(*waiting.pop(0))
        for item in waiting:
            accumulate(*item)

    def compressed_branch(n_rows):
        c_end = (lax.broadcasted_iota(jnp.int32, (n_rows, TQ), 0) * CMP_STRIDE + (CMP_LEN - 1))
        mask = c_end <= t0 + lax.broadcasted_iota(jnp.int32, (n_rows, TQ), 1)
        heads = range(NSA_HEADS)
        kv_rows = [slice(g * NSA_DH, (g + 1) * NSA_DH) for g in groups]
        scores = [jnp.dot(kc_ref[0:n_rows, kv_rows[hh // NSA_GROUP]], qr_ref[hh * NSA_DH:(hh + 1) * NSA_DH, :],
                          preferred_element_type=F32) for hh in heads]
        psum = [jnp.zeros((n_rows, TQ), F32) for _ in groups]
        probs = []
        for hh in heads:
            s = jnp.where(mask, scores[hh], NEG_INF)
            m = jnp.max(s, axis=0, keepdims=True)
            e = jnp.exp2(s - m)
            l = jnp.sum(e, axis=0, keepdims=True)
            p = e * jnp.where(m > 0.5 * NEG_INF, 1.0 / l, 0.0)
            psum[hh // NSA_GROUP] = psum[hh // NSA_GROUP] + p
            probs.append(p.astype(BF16))
        for hh in heads:
            g, kq = divmod(hh, NSA_GROUP)
            o = jnp.dot(vc_ref[kv_rows[g], 0:n_rows], probs[hh], preferred_element_type=F32)
            res_ref[hh * NSA_DH:(hh + 1) * NSA_DH, :] = o * gate(g, kq, 0)
        mcs = mcs_ref[:, 0:n_rows]
        blk = lax.broadcasted_iota(jnp.int32, (n_sel, TQ), 0)
        cur = jnp.right_shift(t0 + lax.broadcasted_iota(jnp.int32, (n_sel, TQ), 1), 6)
        valid = blk <= cur
        bonus = jnp.where((blk == 0) | (blk == cur) | (blk == cur - 1), FORCE_BONUS, 0.0)
        for g in groups:
            hi = psum[g].astype(BF16)
            lo = (psum[g] - hi.astype(F32)).astype(BF16)
            imp = jnp.dot(mcs, hi, preferred_element_type=F32) + jnp.dot(mcs, lo, preferred_element_type=F32)
            imp_ref[g] = jnp.where(valid, imp + bonus, NEG_INF)

    tiles_per_step = pl.num_programs(1) // CMP_PREFIX_STEPS
    for step in range(CMP_PREFIX_STEPS):
        @pl.when(i // tiles_per_step == step)
        def _(step=step):
            compressed_branch((step + 1) * n_grp // CMP_PREFIX_STEPS)

    cnt_ref[...] = jnp.zeros_like(cnt_ref)
    per_tile = TQ // SEL_LEN
    for rnd in range(n_sel // per_tile):
        @pl.when(rnd <= i)
        def _(rnd=rnd):
            sub = 8
            for g in groups:
                contenders = [imp_ref[g, kk:kk + 1, :] for kk in range(rnd * per_tile, (rnd + 1) * per_tile)]
                for jg in range(n_sel // sub):
                    rows_j = slice(jg * sub, (jg + 1) * sub)
                    grp = imp_ref[g, rows_j, :]
                    cnt = cnt_ref[g, rows_j, :]
                    for off, row in enumerate(contenders):
                        kk = rnd * per_tile + off
                        if kk < jg * sub:
                            beats = row >= grp
                        elif kk >= (jg + 1) * sub:
                            beats = row > grp
                        else:
                            idx = jg * sub + lax.broadcasted_iota(jnp.int32, (sub, TQ), 0)
                            beats = (row > grp) | ((row == grp) & (idx > kk))
                        cnt = cnt + jnp.where(beats, 1.0, 0.0)
                    cnt_ref[g, rows_j, :] = cnt

    for g in groups:
        bias = jnp.where(cnt_ref[g] < n_top, 0.0, NEG_INF).astype(BF16)
        pad = jnp.zeros((NSA_DH - n_sel, TQ), BF16)
        for kq in range(NSA_GROUP):
            hh = g * NSA_GROUP + kq
            q_t = qp_ref[hh * NSA_DH:(hh + 1) * NSA_DH, :]
            parts = [q_t, bias, pad] if g == 0 else [bias, pad, q_t]
            qa_ref[g, kq] = jnp.concatenate([x for x in parts if x.shape[0] > 0], axis=0)

    m_ref[...] = jnp.full_like(m_ref, NEG_INF)
    acc_ref[...] = jnp.zeros_like(acc_ref)
    one_hot = lambda rows: oh_ref[rows, :]
    zeros = lambda rows: jnp.zeros((TK, LANE), BF16)

    for g in groups:
        def full_tiles(first, count, g=g):
            return [(g, 0, ks_ref, vs_ref, one_hot, first + n, None, None) for n in range(count)]

        def sel_quad(jj, carry, full_tiles=full_tiles):
            run_tasks(full_tiles(SEL_UNROLL * jj, SEL_UNROLL), LOOP_LOOKAHEAD, 1)
            return carry

        lax.fori_loop(0, i // SEL_UNROLL, sel_quad, 0)

        done_tiles = (i // SEL_UNROLL) * SEL_UNROLL
        leftover = i - done_tiles

        @pl.when(leftover >= SEL_TAIL)
        def _(full_tiles=full_tiles):
            run_tasks(full_tiles(done_tiles, SEL_TAIL), LOOP_LOOKAHEAD, 1)

        for left in range(1, SEL_TAIL):
            @pl.when(leftover % SEL_TAIL == left)
            def _(left=left, full_tiles=full_tiles):
                run_tasks(full_tiles(i - left, left), LOOP_LOOKAHEAD, 1)

    tail = []
    for g in groups:
        tail += [
            (g, 0, ks_ref, vs_ref, one_hot, i, "causal", None),
            (g, 1, kw_ref, vw_ref, zeros, jnp.maximum(i - 2, 0), "lower", jnp.where(i >= 2, 0, 2 ** 30)),
            (g, 1, kw_ref, vw_ref, zeros, jnp.maximum(i - 1, 0), "bias", jnp.where(i >= 1, 0.0, NEG_INF)),
            (g, 1, kw_ref, vw_ref, zeros, i, "causal", None),
        ]
    run_tasks(tail, LOOP_LOOKAHEAD, 1)

    for g in groups:
        for br, idx in ((0, 1), (1, 2)):
            for kq in range(NSA_GROUP):
                hh = g * NSA_GROUP + kq
                dst = slice(hh * NSA_DH, (hh + 1) * NSA_DH)
                o = acc_ref[g, br, kq, 0:NSA_DH, :] / acc_ref[g, br, kq, NSA_DH:NSA_DH + 1, :]
                res_ref[dst, :] += o * gate(g, kq, idx)

    o_ref[...] = res_ref[...].T.astype(BF16)


def _nsa(qr_t, qp_t, kcmp, vcmp_t, ks, vs_t, kw, vw_t, ng_t, mcs_t, onehot, bsz, seq):
    assert WINDOW == 2 * TK and TQ == TK and TQ % SEL_LEN == 0 and SEL_UNROLL == 2 * SEL_TAIL
    t = bsz * seq
    nq = seq // TQ
    n_grp = seq // CMP_STRIDE
    n_sel = seq // SEL_LEN
    assert n_sel <= NSA_DH and n_sel % 16 == 0
    width = NSA_HEADS * NSA_DH
    q_t = pl.BlockSpec((None, None, width, TQ), lambda b, i: (b, i, 0, 0))
    kv = pl.BlockSpec((seq, LANE), lambda b, i: (b, 0))
    kv_t = pl.BlockSpec((None, LANE, seq), lambda b, i: (b, 0, 0))
    return pl.pallas_call(
        functools.partial(_nsa_body, n_top=min(SEL_TOP, n_sel)),
        grid=(bsz, nq),
        in_specs=[q_t, q_t,
                  pl.BlockSpec((None, n_grp, LANE), lambda b, i: (b, 0, 0)),
                  pl.BlockSpec((None, LANE, n_grp), lambda b, i: (b, 0, 0)),
                  kv, kv_t, kv, kv_t,
                  pl.BlockSpec((None, GATE_ROWS, TQ), lambda b, i: (b, 0, i)),
                  _const_spec((n_sel, n_grp)), _const_spec((seq, LANE))],
        out_specs=pl.BlockSpec((TQ, width), lambda b, i: (b * nq + i, 0)),
        out_shape=jax.ShapeDtypeStruct((t, width), BF16),
        scratch_shapes=[pltpu.VMEM((NSA_KV_HEADS, n_sel, TQ), F32), pltpu.VMEM((NSA_KV_HEADS, n_sel, TQ), F32),
                        pltpu.VMEM((NSA_KV_HEADS, NSA_GROUP, LANE, TQ), BF16),
                        pltpu.VMEM((NSA_KV_HEADS, 2, NSA_GROUP, 1, TQ), F32),
                        pltpu.VMEM((NSA_KV_HEADS, 2, NSA_GROUP, ACC_ROWS, TQ), F32),
                        pltpu.VMEM((width, TQ), F32), pltpu.VMEM((SCORE_SLOTS, TK, TQ), F32)],
        compiler_params=_params(("parallel", "parallel")),
        name="nsa",
    )(qr_t, qp_t, kcmp, vcmp_t, ks, vs_t, kw, vw_t, ng_t, mcs_t, onehot)


def _rope_tables(seq, rot_dim, theta, head_dim):
    half = rot_dim // 2
    inv_freq = theta ** (-2.0 * np.arange(half, dtype=np.float64) / rot_dim)
    ang = np.arange(seq, dtype=np.float64)[:, None] * inv_freq[None, :]
    cos, sin = np.cos(ang), np.sin(ang)
    rest = head_dim - rot_dim
    ones = np.ones((seq, rest))
    zr = np.zeros((seq, rest))
    zh = np.zeros((seq, half))
    cos_t = np.concatenate([cos, cos, ones], axis=1)
    sin_a = np.concatenate([-sin, zh, zr], axis=1)
    sin_b = np.concatenate([zh, sin, zr], axis=1)
    rep = LANE // head_dim
    return [jnp.asarray(np.tile(a, (1, rep)), dtype=F32) for a in (cos_t, sin_a, sin_b)]


def _retention_tables():
    c = RET_CHUNK
    gamma = 1.0 - 2.0 ** (-5.0 - np.arange(RET_HEADS, dtype=np.float64))
    log_g = np.log(gamma)
    j = np.arange(c, dtype=np.float64)
    diff = j[:, None] - j[None, :]
    dmat = np.where(diff >= 0, np.exp(log_g[:, None, None] * np.maximum(diff, 0.0)), 0.0)
    xi = np.exp(log_g[:, None] * (j[None, :] + 1.0))[:, :, None]
    zeta = np.exp(log_g[:, None] * (c - 1.0 - j[None, :]))[:, :, None]
    xi = np.broadcast_to(xi, (RET_HEADS, c, RET_DK))
    zeta = np.broadcast_to(zeta, (RET_HEADS, c, RET_DK))
    g_chunk = np.exp(log_g * c)[:, None, None]
    avg = jnp.full((RET_DV, RET_DV), 1.0 / RET_DV, BF16)
    return [jnp.asarray(a, dtype=F32) for a in (dmat, xi, zeta, g_chunk)] + [avg]


def _cmp_to_sel_t(seq):
    n_grp = seq // CMP_STRIDE
    n_sel = seq // SEL_LEN
    c_start = np.arange(n_grp) * CMP_STRIDE
    s_start = np.arange(n_sel) * SEL_LEN
    overlap = (np.minimum(c_start[None, :] + CMP_LEN, s_start[:, None] + SEL_LEN)
               - np.maximum(c_start[None, :], s_start[:, None]))
    m = np.clip(overlap, 0, None) / CMP_LEN
    m[:, n_grp - 1] = 0.0
    return jnp.asarray(m, dtype=BF16)


def _block_onehot(seq):
    n_sel = seq // SEL_LEN
    e = (np.arange(seq) // SEL_LEN)[:, None] == np.arange(NSA_DH)[None, :]
    return jnp.asarray(np.concatenate([e, e], axis=1), dtype=BF16)


def _pack_w_in(w_in):
    pad = jnp.zeros((D_MODEL, D_CAT - w_in.shape[1]), w_in.dtype)
    return jnp.concatenate([w_in, pad], axis=1).astype(BF16)


def _pack_cmp(pe, w1, b1, w2):
    pe2 = pe.reshape(CMP_LEN, NSA_KV_HEADS * NSA_DH)
    w1l = w1.reshape(CMP_LEN, NSA_DH, CMP_HIDDEN)
    z1 = jnp.zeros_like(w1l)
    w1bd = jnp.concatenate([jnp.concatenate([w1l, z1], axis=2),
                            jnp.concatenate([z1, w1l], axis=2)], axis=1).astype(BF16)
    b1bd = jnp.concatenate([b1, b1])[None, :]
    z2 = jnp.zeros_like(w2)
    w2bd = jnp.concatenate([jnp.concatenate([w2, z2], axis=1),
                            jnp.concatenate([z2, w2], axis=1)], axis=0).astype(BF16)
    w1pair = w1bd.reshape(CMP_LEN // 2, 2 * LANE, 2 * CMP_HIDDEN)
    return [pe2, w1pair, b1bd, w2bd]


def kernel(x, ffn1_norm_w, ffn1_w_gate, ffn1_w_up, ffn1_w_down, mix_norm_w, w_in, ret_norm_w, cmp_pe_k, cmp_k_w1, cmp_k_b1, cmp_k_w2, cmp_pe_v, cmp_v_w1, cmp_v_b1, cmp_v_w2, w_out, ffn2_norm_w, ffn2_w_gate, ffn2_w_up, ffn2_w_down, final_norm_w):
    bsz, seq, d = x.shape
    assert d == D_MODEL and seq % TM == 0 and seq % PROJ_TM == 0 and seq % TQ == 0 and seq >= WINDOW
    assert ffn1_norm_w.shape[0] == 1, "the final norm is fused into the (single) layer's second FFN"
    t = bsz * seq
    h = x.reshape(t, d)
    n_layers = ffn1_norm_w.shape[0]
    ret_tabs = _retention_tables()
    ret_rope = _rope_tables(seq, RET_DK, RET_THETA, RET_DK)
    nsa_rope = _rope_tables(seq, ROPE_DIM, ROPE_THETA, NSA_DH)
    mcs_t = _cmp_to_sel_t(seq)
    onehot = _block_onehot(seq)
    for layer in range(n_layers):
        h1 = _ffn1(h, ffn1_norm_w[layer][None, :], ffn1_w_gate[layer], ffn1_w_up[layer], ffn1_w_down[layer])
        (rq, rk, rv, rg, qr_t, qp_t, kc, vc, ks, vs_t, kw, vw_t, ng_t) = _proj(
            h1, mix_norm_w[layer][None, :], _pack_w_in(w_in[layer]), ret_rope + nsa_rope, seq)
        ret = _retention(rq, rk, rv, rg, ret_tabs, ret_norm_w[layer][None, :], bsz, seq)
        kcmp, vcmp_t = _compress(
            kc, vc,
            _pack_cmp(cmp_pe_k[layer], cmp_k_w1[layer], cmp_k_b1[layer], cmp_k_w2[layer]),
            _pack_cmp(cmp_pe_v[layer], cmp_v_w1[layer], cmp_v_b1[layer], cmp_v_w2[layer]),
            bsz, seq)
        nsa = _nsa(qr_t, qp_t, kcmp, vcmp_t, ks, vs_t, kw, vw_t, ng_t, mcs_t, onehot, bsz, seq)
        h = _ffn2(h1, ret, nsa, w_out[layer], ffn2_norm_w[layer][None, :], ffn2_w_gate[layer],
                  ffn2_w_up[layer], ffn2_w_down[layer], final_norm_w[None, :])
    return h.reshape(bsz, seq, d)
```

```python
import functools

import numpy as np
import jax
import jax.numpy as jnp
from jax import lax
from jax.experimental import pallas as pl
from jax.experimental.pallas import tpu as pltpu

F32 = jnp.float32
BF16 = jnp.bfloat16

D_MODEL = 1024
D_FF = 2816
RMS_EPS = 1e-6
GN_EPS = 1e-5

RET_HEADS = 4
RET_DK = 64
RET_DV = 128
RET_THETA = 10000.0

NSA_HEADS = 8
NSA_KV_HEADS = 2
NSA_GROUP = 4
NSA_DH = 64
CMP_LEN = 32
CMP_STRIDE = 16
CMP_HIDDEN = 256
SEL_LEN = 64
SEL_TOP = 16
WINDOW = 512
N_GATES = 3
ROPE_THETA = 500000.0
ROPE_DIM = 16

NEG_INF = -1e30
LOG2E = 1.4426950408889634
FORCE_BONUS = 1e4

LANE = 128
VMEM_LIMIT = 56 * 1024 * 1024

TM = 512
PROJ_TM = 1024
RET_CHUNK = 128
RET_GROUP = 8
TQ = 256
TK = 256
SEL_UNROLL = 8
SEL_TAIL = 4
LOOP_LOOKAHEAD = 6
CMP_PREFIX_STEPS = 4
NSA_STEP_TILES = 2
SCORE_SLOTS = 7
SCORE_PAD_ROWS = 8
ACC_ROWS = 80
FF_SPLIT = (1280, 1536)
W_CHUNKS = 8

SEG = dict(rq=(0, 256), rk=(256, 512), rv=(512, 1024), rg=(1024, 1536), nq=(1536, 2048),
           kvc=(2048, 2304), kvs=(2304, 2560), kvw=(2560, 2816), ng=(2816, 2944))
D_CAT = 2944
GATE_ROWS = 32


def _rms(x, w):
    return x * lax.rsqrt(jnp.mean(x * x, axis=-1, keepdims=True) + RMS_EPS) * w


def _params(sem):
    return pltpu.CompilerParams(dimension_semantics=sem, vmem_limit_bytes=VMEM_LIMIT)


def _const_spec(shape):
    nd = len(shape)
    return pl.BlockSpec(shape, lambda *_: (0,) * nd, pipeline_mode=pl.Buffered(1))


def _swiglu_half_step(h, nw, wg_ref, wu_ref, wd_ref):
    hw = (h * nw).astype(BF16)
    inv_rms = lax.rsqrt(jnp.mean(h * h, axis=-1, keepdims=True) + RMS_EPS)
    acc = None
    off = 0
    for width in FF_SPLIT:
        g = jnp.dot(hw, wg_ref[:, off:off + width], preferred_element_type=F32) * inv_rms
        u = jnp.dot(hw, wu_ref[:, off:off + width], preferred_element_type=F32) * inv_rms
        a = (g * jax.nn.sigmoid(g) * u).astype(BF16)
        y = jnp.dot(a, wd_ref[off:off + width, :], preferred_element_type=F32)
        acc = y if acc is None else acc + y
        off += width
    return h + 0.5 * acc


def _convert_weight_chunk(step, pairs):
    for src_ref, dst_ref in pairs:
        rows = src_ref.shape[0]
        dst_ref[pl.ds(pl.multiple_of(step * rows, 16), rows), :] = src_ref[...].astype(BF16)


def _ffn1_body(x_ref, nw_ref, wg_ref, wu_ref, wd_ref, o_ref, wg_bf, wu_bf, wd_bf):
    step = pl.program_id(0)

    @pl.when(step < W_CHUNKS)
    def _():
        _convert_weight_chunk(step, [(wg_ref, wg_bf), (wu_ref, wu_bf), (wd_ref, wd_bf)])

    @pl.when(step >= W_CHUNKS)
    def _():
        o_ref[...] = _swiglu_half_step(x_ref[...], nw_ref[...], wg_bf, wu_bf, wd_bf)


def _token_tile(width):
    return pl.BlockSpec((TM, width), lambda i: (jnp.maximum(i - W_CHUNKS, 0), 0))


def _weight_chunk(rows, cols):
    return pl.BlockSpec((rows // W_CHUNKS, cols), lambda i: (jnp.minimum(i, W_CHUNKS - 1), 0))


def _ffn1(x2, nw, wg, wu, wd):
    t = x2.shape[0]
    return pl.pallas_call(
        _ffn1_body,
        grid=(W_CHUNKS + t // TM,),
        in_specs=[_token_tile(D_MODEL), _const_spec((1, D_MODEL)), _weight_chunk(D_MODEL, D_FF),
                  _weight_chunk(D_MODEL, D_FF), _weight_chunk(D_FF, D_MODEL)],
        out_specs=_token_tile(D_MODEL),
        out_shape=jax.ShapeDtypeStruct((t, D_MODEL), F32),
        scratch_shapes=[pltpu.VMEM((D_MODEL, D_FF), BF16), pltpu.VMEM((D_MODEL, D_FF), BF16),
                        pltpu.VMEM((D_FF, D_MODEL), BF16)],
        compiler_params=_params(("arbitrary",)),
        name="ffn1",
    )(x2, nw, wg, wu, wd)


def _ffn2_body(h_ref, ret_ref, nsa_ref, wo_ref, nw_ref, wg_ref, wu_ref, wd_ref, fw_ref, o_ref,
               wo_bf, wg_bf, wu_bf, wd_bf):
    step = pl.program_id(0)

    @pl.when(step < W_CHUNKS)
    def _():
        _convert_weight_chunk(step, [(wo_ref, wo_bf), (wg_ref, wg_bf), (wu_ref, wu_bf), (wd_ref, wd_bf)])

    @pl.when(step >= W_CHUNKS)
    def _():
        half = RET_HEADS * RET_DV
        mix = (jnp.dot(ret_ref[...], wo_bf[:half, :], preferred_element_type=F32)
               + jnp.dot(nsa_ref[...], wo_bf[half:, :], preferred_element_type=F32))
        h = h_ref[...] + mix
        h = _swiglu_half_step(h, nw_ref[...], wg_bf, wu_bf, wd_bf)
        o_ref[...] = _rms(h, fw_ref[...])


def _ffn2(h1, ret, nsa, wo, nw, wg, wu, wd, fw):
    t = h1.shape[0]
    return pl.pallas_call(
        _ffn2_body,
        grid=(W_CHUNKS + t // TM,),
        in_specs=[_token_tile(D_MODEL), _token_tile(D_MODEL // 2), _token_tile(D_MODEL // 2),
                  _weight_chunk(D_MODEL, D_MODEL), _const_spec((1, D_MODEL)),
                  _weight_chunk(D_MODEL, D_FF), _weight_chunk(D_MODEL, D_FF),
                  _weight_chunk(D_FF, D_MODEL), _const_spec((1, D_MODEL))],
        out_specs=_token_tile(D_MODEL),
        out_shape=jax.ShapeDtypeStruct((t, D_MODEL), F32),
        scratch_shapes=[pltpu.VMEM((D_MODEL, D_MODEL), BF16), pltpu.VMEM((D_MODEL, D_FF), BF16),
                        pltpu.VMEM((D_MODEL, D_FF), BF16), pltpu.VMEM((D_FF, D_MODEL), BF16)],
        compiler_params=_params(("arbitrary",)),
        name="ffn2",
    )(h1, ret, nsa, wo, nw, wg, wu, wd, fw)


def _rope128(x, cos, sin_a, sin_b, half):
    return (x * cos + pltpu.roll(x, LANE - half, 1) * sin_a + pltpu.roll(x, half, 1) * sin_b)


def _proj_body(h_ref, nw_ref, w_ref, rc_ref, ra_ref, rb_ref, nc_ref, na_ref, nb_ref,
               rq_ref, rk_ref, rv_ref, rg_ref, qr_ref, qp_ref, kc_ref, vc_ref,
               ks_ref, vs_ref, kw_ref, vw_ref, ng_ref):
    u = _rms(h_ref[...], nw_ref[...]).astype(BF16)

    def seg(name):
        a, b = SEG[name]
        return jnp.dot(u, w_ref[:, a:b], preferred_element_type=F32)

    rc, ra, rb = rc_ref[...], ra_ref[...], rb_ref[...]
    nc, na, nb = nc_ref[...], na_ref[...], nb_ref[...]
    ret_half = RET_DK // 2
    nsa_half = ROPE_DIM // 2
    nsa_scale = NSA_DH ** -0.5 * LOG2E

    p = seg("nq")
    for j in range(4):
        sl = slice(j * LANE, (j + 1) * LANE)
        raw_t = (p[:, sl] * nsa_scale).T.astype(BF16)
        rot_t = (_rope128(p[:, sl], nc, na, nb, nsa_half) * nsa_scale).T.astype(BF16)
        for n in range(PROJ_TM // TQ):
            cols = slice(n * TQ, (n + 1) * TQ)
            qr_ref[n, sl, :] = raw_t[:, cols]
            qp_ref[n, sl, :] = rot_t[:, cols]
    p = seg("kvc")
    kc_ref[...] = p[:, :LANE]
    vc_ref[...] = p[:, LANE:]
    p = seg("kvs")
    ks_ref[...] = _rope128(p[:, :LANE], nc, na, nb, nsa_half).astype(BF16)
    vs_ref[...] = p[:, LANE:].T.astype(BF16)
    p = seg("kvw")
    kw_ref[...] = _rope128(p[:, :LANE], nc, na, nb, nsa_half).astype(BF16)
    vw_ref[...] = p[:, LANE:].T.astype(BF16)
    ng_ref[...] = seg("ng").T[:GATE_ROWS, :]
    p = seg("rq")
    for j in range(2):
        sl = slice(j * LANE, (j + 1) * LANE)
        rq_ref[:, sl] = _rope128(p[:, sl], rc, ra, rb, ret_half)
    p = seg("rk")
    for j in range(2):
        sl = slice(j * LANE, (j + 1) * LANE)
        rk_ref[:, sl] = _rope128(p[:, sl], rc, ra, rb, ret_half) * (RET_DK ** -0.5)
    rv_ref[...] = seg("rv").astype(BF16)
    rg_ref[...] = seg("rg")


def _proj(h1, nw, wcat, tabs, seq):
    t = h1.shape[0]
    bsz = t // seq
    per_seq = seq // PROJ_TM

    def tok(width, dtype):
        return pl.BlockSpec((PROJ_TM, width), lambda i: (i, 0)), jax.ShapeDtypeStruct((t, width), dtype)

    def tr(rows, dtype):
        return (pl.BlockSpec((None, rows, PROJ_TM), lambda i: (i // per_seq, 0, i % per_seq)),
                jax.ShapeDtypeStruct((bsz, rows, seq), dtype))

    tab = pl.BlockSpec((PROJ_TM, LANE), lambda i: (i % per_seq, 0))
    width = NSA_HEADS * NSA_DH
    per_tm = PROJ_TM // TQ
    q_tiles = (pl.BlockSpec((None, per_tm, width, TQ), lambda i: (i // per_seq, i % per_seq, 0, 0)),
               jax.ShapeDtypeStruct((bsz, seq // TQ, width, TQ), BF16))
    outs = [tok(256, F32), tok(256, F32), tok(512, BF16), tok(512, F32),
            q_tiles, q_tiles,
            tok(LANE, F32), tok(LANE, F32),
            tok(LANE, BF16), tr(LANE, BF16), tok(LANE, BF16), tr(LANE, BF16),
            tr(GATE_ROWS, F32)]
    return pl.pallas_call(
        _proj_body,
        grid=(t // PROJ_TM,),
        in_specs=[tok(D_MODEL, F32)[0], _const_spec((1, D_MODEL)), _const_spec((D_MODEL, D_CAT))] + [tab] * 6,
        out_specs=[o[0] for o in outs],
        out_shape=[o[1] for o in outs],
        compiler_params=_params(("parallel",)),
        name="proj",
    )(h1, nw, wcat, *tabs)


def _ret_body(q_ref, k_ref, v_ref, g_ref, dm_ref, xi_ref, ze_ref, gc_ref, avg_ref, nw_ref, o_ref, st_ref, *, seq):
    c = RET_CHUNK
    st_ref[...] = jnp.zeros_like(st_ref)

    def chunk_group(n, carry):
        heads = range(RET_HEADS)
        members = range(RET_GROUP)
        ks = [slice(h * RET_DK, (h + 1) * RET_DK) for h in heads]
        vs = [slice(h * RET_DV, (h + 1) * RET_DV) for h in heads]
        rows = [pl.ds(pl.multiple_of((n * RET_GROUP + ci) * c, c), c) for ci in members]
        q = [[q_ref[rows[ci], ks[h]] for h in heads] for ci in members]
        k = [[k_ref[rows[ci], ks[h]] for h in heads] for ci in members]
        v = [[v_ref[rows[ci], vs[h]] for h in heads] for ci in members]
        inner = [[lax.dot_general(q[ci][h].astype(BF16), k[ci][h].astype(BF16), (((1,), (1,)), ((), ())),
                                  preferred_element_type=F32) for h in heads] for ci in members]
        upd = [[lax.dot_general((k[ci][h] * ze_ref[h]).astype(BF16), v[ci][h], (((0,), (0,)), ((), ())),
                                preferred_element_type=F32) for h in heads] for ci in members]
        cross = [[None] * RET_HEADS for _ in members]
        for h in heads:
            state = st_ref[h]
            for ci in members:
                cross[ci][h] = jnp.dot((q[ci][h] * xi_ref[h]).astype(BF16), state.astype(BF16),
                                       preferred_element_type=F32)
                state = gc_ref[h] * state + upd[ci][h]
            st_ref[h] = state
        pairs = [(ci, h) for ci in members for h in heads]
        out = {u: jnp.dot((inner[u[0]][u[1]] * dm_ref[u[1]]).astype(BF16), v[u[0]][u[1]],
                          preferred_element_type=F32) + cross[u[0]][u[1]] for u in pairs}
        mu = {u: jnp.dot(out[u].astype(BF16), avg_ref[...], preferred_element_type=F32) for u in pairs}
        dev = {u: out[u] - mu[u] for u in pairs}
        var = {u: jnp.dot((dev[u] * dev[u]).astype(BF16), avg_ref[...], preferred_element_type=F32)
               for u in pairs}
        for ci, h in pairs:
            y = dev[(ci, h)] * lax.rsqrt(var[(ci, h)] + GN_EPS) * nw_ref[:, vs[h]]
            gate = g_ref[rows[ci], vs[h]]
            o_ref[rows[ci], vs[h]] = (y * (gate * jax.nn.sigmoid(gate))).astype(BF16)
        return carry

    lax.fori_loop(0, seq // (c * RET_GROUP), chunk_group, 0)


def _retention(rq, rk, rv, rg, tabs, nw, bsz, seq):
    t = bsz * seq
    c = RET_CHUNK
    assert seq % (c * RET_GROUP) == 0

    def seq_blk(width):
        return pl.BlockSpec((seq, width), lambda b: (b, 0))

    return pl.pallas_call(
        functools.partial(_ret_body, seq=seq),
        grid=(bsz,),
        in_specs=[seq_blk(256), seq_blk(256), seq_blk(512), seq_blk(512),
                  _const_spec((RET_HEADS, c, c)), _const_spec((RET_HEADS, c, RET_DK)),
                  _const_spec((RET_HEADS, c, RET_DK)), _const_spec((RET_HEADS, 1, 1)),
                  _const_spec((RET_DV, RET_DV)), _const_spec((1, RET_HEADS * RET_DV))],
        out_specs=seq_blk(512),
        out_shape=jax.ShapeDtypeStruct((t, RET_HEADS * RET_DV), BF16),
        scratch_shapes=[pltpu.VMEM((RET_HEADS, RET_DK, RET_DV), F32)],
        compiler_params=_params(("parallel",)),
        name="retention",
    )(rq, rk, rv, rg, *tabs, nw)


def _gelu_tanh(x):
    return 0.5 * x * (1.0 + jnp.tanh(np.sqrt(2.0 / np.pi).astype(np.float32) * (x + 0.044715 * (x * x * x))))


def _cmp_body(kc_ref, vc_ref, pek_ref, w1k_ref, b1k_ref, w2k_ref, pev_ref, w1v_ref, b1v_ref, w2v_ref,
              ko_ref, vo_ref, *, seq):
    n_grp = seq // CMP_STRIDE

    def one(x_ref, pe_ref, w1_ref, b1_ref, w2_ref, o_ref, transposed):
        def half_sum(base):
            total = None
            for l in range(0, CMP_STRIDE, 2):
                xs = [(x_ref[pl.ds(l + d, n_grp, stride=CMP_STRIDE), :]
                       + pe_ref[base + l + d:base + l + d + 1, :]).astype(BF16) for d in range(2)]
                part = jnp.dot(jnp.concatenate(xs, axis=1), w1_ref[(base + l) // 2],
                               preferred_element_type=F32)
                total = part if total is None else total + part
            return total

        first = half_sum(0)
        second = half_sum(CMP_STRIDE)
        pre = first + pltpu.roll(second, n_grp - 1, 0) + b1_ref[...]
        hid = _gelu_tanh(pre).astype(BF16)
        res = jnp.dot(hid, w2_ref[...], preferred_element_type=F32)
        row = lax.broadcasted_iota(jnp.int32, res.shape, 0)
        res = jnp.where(row < n_grp - 1, res, 0.0)
        if transposed:
            o_ref[...] = res.T.astype(BF16)
        else:
            o_ref[...] = res.astype(BF16)

    one(kc_ref, pek_ref, w1k_ref, b1k_ref, w2k_ref, ko_ref, False)
    one(vc_ref, pev_ref, w1v_ref, b1v_ref, w2v_ref, vo_ref, True)


def _compress(kc, vc, wk, wv, bsz, seq):
    n_grp = seq // CMP_STRIDE
    blk = pl.BlockSpec((seq, LANE), lambda b: (b, 0))
    wspecs = [_const_spec((CMP_LEN, LANE)), _const_spec((CMP_LEN // 2, 2 * LANE, 2 * CMP_HIDDEN)),
              _const_spec((1, 2 * CMP_HIDDEN)), _const_spec((2 * CMP_HIDDEN, LANE))]
    return pl.pallas_call(
        functools.partial(_cmp_body, seq=seq),
        grid=(bsz,),
        in_specs=[blk, blk] + wspecs + wspecs,
        out_specs=[pl.BlockSpec((None, n_grp, LANE), lambda b: (b, 0, 0)),
                   pl.BlockSpec((None, LANE, n_grp), lambda b: (b, 0, 0))],
        out_shape=[jax.ShapeDtypeStruct((bsz, n_grp, LANE), BF16),
                   jax.ShapeDtypeStruct((bsz, LANE, n_grp), BF16)],
        compiler_params=_params(("parallel",)),
        name="compress",
    )(kc, vc, *wk, *wv)


def _nsa_body(qr_ref, qp_ref, kc_ref, vc_ref, ks_ref, vs_ref, kw_ref, vw_ref, ng_ref, mcs_ref, oh_ref,
              o_ref, *scratch, n_top):
    def tile(k, carry):
        cols = pl.ds(pl.multiple_of(k * TQ, TQ), TQ)
        _nsa_tile(pl.program_id(1) * NSA_STEP_TILES + k, pl.num_programs(1) * NSA_STEP_TILES,
                  qr_ref.at[k], qp_ref.at[k], kc_ref, vc_ref, ks_ref, vs_ref, kw_ref, vw_ref,
                  ng_ref.at[:, cols], mcs_ref, oh_ref, o_ref.at[cols, :], *scratch, n_top=n_top)
        return carry

    lax.fori_loop(0, NSA_STEP_TILES, tile, 0)


def _nsa_tile(i, n_tiles, qr_ref, qp_ref, kc_ref, vc_ref, ks_ref, vs_ref, kw_ref, vw_ref, ng_ref, mcs_ref,
              oh_ref, o_ref, imp_ref, cnt_ref, qa_ref, m_ref, acc_ref, res_ref, s_ref, *, n_top):
    n_grp = kc_ref.shape[0]
    n_sel = mcs_ref.shape[0]
    t0 = i * TQ
    gates = jax.nn.sigmoid(ng_ref[...])
    lane_head = lax.broadcasted_iota(jnp.int32, (TK, LANE), 1) // NSA_DH
    rel0 = (lax.broadcasted_iota(jnp.int32, (TK, TQ), 0)
            - lax.broadcasted_iota(jnp.int32, (TK, TQ), 1))
    groups = range(NSA_KV_HEADS)

    def gate(g, kq, idx):
        r = (g * NSA_GROUP + kq) * N_GATES + idx
        return gates[r:r + 1, :]

    def load_keys(task):
        g, br, k_ref, v_ref, other, j, kind, arg = task
        rows = pl.ds(pl.multiple_of(j * TK, TK), TK)
        k_aug = jnp.where(lane_head == g, k_ref[rows, :], other(rows))
        v_t = jnp.concatenate([v_ref[g * NSA_DH:(g + 1) * NSA_DH, rows],
                               jnp.ones((ACC_ROWS - NSA_DH, TK), BF16)], axis=0)
        return k_aug, v_t

    def softmax(task, kq, s):
        g, br, k_ref, v_ref, other, j, kind, arg = task
        m_prev = m_ref[g, br, kq]
        if kind in ("causal", "lower"):
            h = TK // 2
            lo, hi = slice(0, h), slice(h, TK)
            if kind == "causal":
                tri = rel0[lo, lo] <= 0
                quads = {(0, 0): jnp.where(tri, s[lo, lo], NEG_INF), (0, 1): s[lo, hi],
                         (1, 1): jnp.where(tri, s[hi, hi], NEG_INF)}
            else:
                tri = rel0[lo, lo] > arg
                quads = {(0, 0): jnp.where(tri, s[lo, lo], NEG_INF),
                         (1, 0): s[hi, lo] + jnp.where(arg == 0, 0.0, NEG_INF),
                         (1, 1): jnp.where(tri, s[hi, hi], NEG_INF)}
            col_max = []
            for c in (0, 1):
                parts = [jnp.max(x, axis=0, keepdims=True) for (r, cc), x in quads.items() if cc == c]
                col_max.append(functools.reduce(jnp.maximum, parts))
            m_next = jnp.maximum(m_prev, jnp.concatenate(col_max, axis=1))
            m_cols = (m_next[:, lo], m_next[:, hi])
            zero = jnp.zeros((h, h), BF16)
            blocks = [[jnp.exp2(quads[(r, c)] - m_cols[c]).astype(BF16) if (r, c) in quads else zero
                       for c in (0, 1)] for r in (0, 1)]
            p = jnp.concatenate([jnp.concatenate(row, axis=1) for row in blocks], axis=0)
        else:
            if kind == "bias":
                s = s + arg
            m_next = jnp.maximum(m_prev, jnp.max(s, axis=0, keepdims=True))
            p = jnp.exp2(s - m_next).astype(BF16)
        m_ref[g, br, kq] = m_next
        return p, jnp.exp2(m_prev - m_next)

    def run_tasks(tasks, lookahead, pv_lag):
        operands = [load_keys(task) for task in tasks]
        units = [(n, kq) for n in range(len(tasks)) for kq in range(NSA_GROUP)]
        def score(pos):
            n, kq = units[pos]
            s_ref[pos % SCORE_SLOTS, 0:TK, :] = jnp.dot(operands[n][0], qa_ref[tasks[n][0], kq],
                                                        preferred_element_type=F32)

        for pos in range(min(lookahead, len(units))):
            score(pos)
        def accumulate(n2, kq2, p, alpha):
            g, br = tasks[n2][0], tasks[n2][1]
            acc_ref[g, br, kq2] = alpha * acc_ref[g, br, kq2] + jnp.dot(
                operands[n2][1], p, preferred_element_type=F32)

        waiting = []
        for pos, (n, kq) in enumerate(units):
            if pos + lookahead < len(units):
                score(pos + lookahead)
            waiting.append((n, kq) + softmax(tasks[n], kq, s_ref[pos % SCORE_SLOTS, 0:TK, :]))
            if len(waiting) > pv_lag:
                accumulate(*waiting.pop(0))
        for item in waiting:
            accumulate(*item)

    def compressed_branch(n_rows):
        c_end = (lax.broadcasted_iota(jnp.int32, (n_rows, TQ), 0) * CMP_STRIDE + (CMP_LEN - 1))
        mask = c_end <= t0 + lax.broadcasted_iota(jnp.int32, (n_rows, TQ), 1)
        heads = range(NSA_HEADS)
        kv_rows = [slice(g * NSA_DH, (g + 1) * NSA_DH) for g in groups]
        scores = [jnp.dot(kc_ref[0:n_rows, kv_rows[hh // NSA_GROUP]], qr_ref[hh * NSA_DH:(hh + 1) * NSA_DH, :],
                          preferred_element_type=F32) for hh in heads]
        psum = [jnp.zeros((n_rows, TQ), F32) for _ in groups]
        probs = []
        for hh in heads:
            s = jnp.where(mask, scores[hh], NEG_INF)
            m = jnp.max(s, axis=0, keepdims=True)
            e = jnp.exp2(s - m)
            l = jnp.sum(e, axis=0, keepdims=True)
            p = e * jnp.where(m > 0.5 * NEG_INF, 1.0 / l, 0.0)
            psum[hh // NSA_GROUP] = psum[hh // NSA_GROUP] + p
            probs.append(p.astype(BF16))
        for hh in heads:
            g, kq = divmod(hh, NSA_GROUP)
            o = jnp.dot(vc_ref[kv_rows[g], 0:n_rows], probs[hh], preferred_element_type=F32)
            res_ref[hh * NSA_DH:(hh + 1) * NSA_DH, :] = o * gate(g, kq, 0)
        mcs = mcs_ref[:, 0:n_rows]
        blk = lax.broadcasted_iota(jnp.int32, (n_sel, TQ), 0)
        cur = jnp.right_shift(t0 + lax.broadcasted_iota(jnp.int32, (n_sel, TQ), 1), 6)
        valid = blk <= cur
        bonus = jnp.where((blk == 0) | (blk == cur) | (blk == cur - 1), FORCE_BONUS, 0.0)
        for g in groups:
            hi = psum[g].astype(BF16)
            lo = (psum[g] - hi.astype(F32)).astype(BF16)
            imp = jnp.dot(mcs, hi, preferred_element_type=F32) + jnp.dot(mcs, lo, preferred_element_type=F32)
            imp_ref[g] = jnp.where(valid, imp + bonus, NEG_INF)

    tiles_per_step = n_tiles // CMP_PREFIX_STEPS
    for step in range(CMP_PREFIX_STEPS):
        @pl.when(i // tiles_per_step == step)
        def _(step=step):
            compressed_branch((step + 1) * n_grp // CMP_PREFIX_STEPS)

    cnt_ref[...] = jnp.zeros_like(cnt_ref)
    per_tile = TQ // SEL_LEN
    for rnd in range(n_sel // per_tile):
        @pl.when(rnd <= i)
        def _(rnd=rnd):
            sub = 8
            for g in groups:
                contenders = [imp_ref[g, kk:kk + 1, :] for kk in range(rnd * per_tile, (rnd + 1) * per_tile)]
                for jg in range(n_sel // sub):
                    rows_j = slice(jg * sub, (jg + 1) * sub)
                    grp = imp_ref[g, rows_j, :]
                    cnt = cnt_ref[g, rows_j, :]
                    for off, row in enumerate(contenders):
                        kk = rnd * per_tile + off
                        if kk < jg * sub:
                            beats = row >= grp
                        elif kk >= (jg + 1) * sub:
                            beats = row > grp
                        else:
                            idx = jg * sub + lax.broadcasted_iota(jnp.int32, (sub, TQ), 0)
                            beats = (row > grp) | ((row == grp) & (idx > kk))
                        cnt = cnt + jnp.where(beats, 1.0, 0.0)
                    cnt_ref[g, rows_j, :] = cnt

    for g in groups:
        bias = jnp.where(cnt_ref[g] < n_top, 0.0, NEG_INF).astype(BF16)
        pad = jnp.zeros((NSA_DH - n_sel, TQ), BF16)
        for kq in range(NSA_GROUP):
            hh = g * NSA_GROUP + kq
            q_t = qp_ref[hh * NSA_DH:(hh + 1) * NSA_DH, :]
            parts = [q_t, bias, pad] if g == 0 else [bias, pad, q_t]
            qa_ref[g, kq] = jnp.concatenate([x for x in parts if x.shape[0] > 0], axis=0)

    m_ref[...] = jnp.full_like(m_ref, NEG_INF)
    acc_ref[...] = jnp.zeros_like(acc_ref)
    one_hot = lambda rows: oh_ref[rows, :]
    zeros = lambda rows: jnp.zeros((TK, LANE), BF16)

    for g in groups:
        def full_tiles(first, count, g=g):
            return [(g, 0, ks_ref, vs_ref, one_hot, first + n, None, None) for n in range(count)]

        def sel_quad(jj, carry, full_tiles=full_tiles):
            run_tasks(full_tiles(SEL_UNROLL * jj, SEL_UNROLL), LOOP_LOOKAHEAD, 1)
            return carry

        lax.fori_loop(0, i // SEL_UNROLL, sel_quad, 0)

        done_tiles = (i // SEL_UNROLL) * SEL_UNROLL
        leftover = i - done_tiles

        @pl.when(leftover >= SEL_TAIL)
        def _(full_tiles=full_tiles):
            run_tasks(full_tiles(done_tiles, SEL_TAIL), LOOP_LOOKAHEAD, 1)

        for left in range(1, SEL_TAIL):
            @pl.when(leftover % SEL_TAIL == left)
            def _(left=left, full_tiles=full_tiles):
                run_tasks(full_tiles(i - left, left), LOOP_LOOKAHEAD, 1)

    tail = []
    for g in groups:
        tail += [
            (g, 0, ks_ref, vs_ref, one_hot, i, "causal", None),
            (g, 1, kw_ref, vw_ref, zeros, jnp.maximum(i - 2, 0), "lower", jnp.where(i >= 2, 0, 2 ** 30)),
            (g, 1, kw_ref, vw_ref, zeros, jnp.maximum(i - 1, 0), "bias", jnp.where(i >= 1, 0.0, NEG_INF)),
            (g, 1, kw_ref, vw_ref, zeros, i, "causal", None),
        ]
    run_tasks(tail, LOOP_LOOKAHEAD, 1)

    for g in groups:
        for br, idx in ((0, 1), (1, 2)):
            for kq in range(NSA_GROUP):
                hh = g * NSA_GROUP + kq
                dst = slice(hh * NSA_DH, (hh + 1) * NSA_DH)
                o = acc_ref[g, br, kq, 0:NSA_DH, :] / acc_ref[g, br, kq, NSA_DH:NSA_DH + 1, :]
                res_ref[dst, :] += o * gate(g, kq, idx)

    o_ref[...] = res_ref[...].T.astype(BF16)


def _nsa(qr_t, qp_t, kcmp, vcmp_t, ks, vs_t, kw, vw_t, ng_t, mcs_t, onehot, bsz, seq):
    assert WINDOW == 2 * TK and TQ == TK and TQ % SEL_LEN == 0 and SEL_UNROLL == 2 * SEL_TAIL
    t = bsz * seq
    nq = seq // TQ
    n_grp = seq // CMP_STRIDE
    n_sel = seq // SEL_LEN
    assert n_sel <= NSA_DH and n_sel % 16 == 0
    width = NSA_HEADS * NSA_DH
    per_step = NSA_STEP_TILES
    q_t = pl.BlockSpec((None, per_step, width, TQ), lambda b, i: (b, i, 0, 0))
    kv = pl.BlockSpec((seq, LANE), lambda b, i: (b, 0))
    kv_t = pl.BlockSpec((None, LANE, seq), lambda b, i: (b, 0, 0))
    return pl.pallas_call(
        functools.partial(_nsa_body, n_top=min(SEL_TOP, n_sel)),
        grid=(bsz, nq // per_step),
        in_specs=[q_t, q_t,
                  pl.BlockSpec((None, n_grp, LANE), lambda b, i: (b, 0, 0)),
                  pl.BlockSpec((None, LANE, n_grp), lambda b, i: (b, 0, 0)),
                  kv, kv_t, kv, kv_t,
                  pl.BlockSpec((None, GATE_ROWS, per_step * TQ), lambda b, i: (b, 0, i)),
                  _const_spec((n_sel, n_grp)), _const_spec((seq, LANE))],
        out_specs=pl.BlockSpec((per_step * TQ, width), lambda b, i: (b * (nq // per_step) + i, 0)),
        out_shape=jax.ShapeDtypeStruct((t, width), BF16),
        scratch_shapes=[pltpu.VMEM((NSA_KV_HEADS, n_sel, TQ), F32), pltpu.VMEM((NSA_KV_HEADS, n_sel, TQ), F32),
                        pltpu.VMEM((NSA_KV_HEADS, NSA_GROUP, LANE, TQ), BF16),
                        pltpu.VMEM((NSA_KV_HEADS, 2, NSA_GROUP, 1, TQ), F32),
                        pltpu.VMEM((NSA_KV_HEADS, 2, NSA_GROUP, ACC_ROWS, TQ), F32),
                        pltpu.VMEM((width, TQ), F32), pltpu.VMEM((SCORE_SLOTS, TK + SCORE_PAD_ROWS, TQ), F32)],
        compiler_params=_params(("parallel", "parallel")),
        name="nsa",
    )(qr_t, qp_t, kcmp, vcmp_t, ks, vs_t, kw, vw_t, ng_t, mcs_t, onehot)


def _rope_tables(seq, rot_dim, theta, head_dim):
    half = rot_dim // 2
    inv_freq = theta ** (-2.0 * np.arange(half, dtype=np.float64) / rot_dim)
    ang = np.arange(seq, dtype=np.float64)[:, None] * inv_freq[None, :]
    cos, sin = np.cos(ang), np.sin(ang)
    rest = head_dim - rot_dim
    ones = np.ones((seq, rest))
    zr = np.zeros((seq, rest))
    zh = np.zeros((seq, half))
    cos_t = np.concatenate([cos, cos, ones], axis=1)
    sin_a = np.concatenate([-sin, zh, zr], axis=1)
    sin_b = np.concatenate([zh, sin, zr], axis=1)
    rep = LANE // head_dim
    return [jnp.asarray(np.tile(a, (1, rep)), dtype=F32) for a in (cos_t, sin_a, sin_b)]


def _retention_tables():
    c = RET_CHUNK
    gamma = 1.0 - 2.0 ** (-5.0 - np.arange(RET_HEADS, dtype=np.float64))
    log_g = np.log(gamma)
    j = np.arange(c, dtype=np.float64)
    diff = j[:, None] - j[None, :]
    dmat = np.where(diff >= 0, np.exp(log_g[:, None, None] * np.maximum(diff, 0.0)), 0.0)
    xi = np.exp(log_g[:, None] * (j[None, :] + 1.0))[:, :, None]
    zeta = np.exp(log_g[:, None] * (c - 1.0 - j[None, :]))[:, :, None]
    xi = np.broadcast_to(xi, (RET_HEADS, c, RET_DK))
    zeta = np.broadcast_to(zeta, (RET_HEADS, c, RET_DK))
    g_chunk = np.exp(log_g * c)[:, None, None]
    avg = jnp.full((RET_DV, RET_DV), 1.0 / RET_DV, BF16)
    return [jnp.asarray(a, dtype=F32) for a in (dmat, xi, zeta, g_chunk)] + [avg]


def _cmp_to_sel_t(seq):
    n_grp = seq // CMP_STRIDE
    n_sel = seq // SEL_LEN
    c_start = np.arange(n_grp) * CMP_STRIDE
    s_start = np.arange(n_sel) * SEL_LEN
    overlap = (np.minimum(c_start[None, :] + CMP_LEN, s_start[:, None] + SEL_LEN)
               - np.maximum(c_start[None, :], s_start[:, None]))
    m = np.clip(overlap, 0, None) / CMP_LEN
    m[:, n_grp - 1] = 0.0
    return jnp.asarray(m, dtype=BF16)


def _block_onehot(seq):
    n_sel = seq // SEL_LEN
    e = (np.arange(seq) // SEL_LEN)[:, None] == np.arange(NSA_DH)[None, :]
    return jnp.asarray(np.concatenate([e, e], axis=1), dtype=BF16)


def _pack_w_in(w_in):
    pad = jnp.zeros((D_MODEL, D_CAT - w_in.shape[1]), w_in.dtype)
    return jnp.concatenate([w_in, pad], axis=1).astype(BF16)


def _pack_cmp(pe, w1, b1, w2):
    pe2 = pe.reshape(CMP_LEN, NSA_KV_HEADS * NSA_DH)
    w1l = w1.reshape(CMP_LEN, NSA_DH, CMP_HIDDEN)
    z1 = jnp.zeros_like(w1l)
    w1bd = jnp.concatenate([jnp.concatenate([w1l, z1], axis=2),
                            jnp.concatenate([z1, w1l], axis=2)], axis=1).astype(BF16)
    b1bd = jnp.concatenate([b1, b1])[None, :]
    z2 = jnp.zeros_like(w2)
    w2bd = jnp.concatenate([jnp.concatenate([w2, z2], axis=1),
                            jnp.concatenate([z2, w2], axis=1)], axis=0).astype(BF16)
    w1pair = w1bd.reshape(CMP_LEN // 2, 2 * LANE, 2 * CMP_HIDDEN)
    return [pe2, w1pair, b1bd, w2bd]


def kernel(x, ffn1_norm_w, ffn1_w_gate, ffn1_w_up, ffn1_w_down, mix_norm_w, w_in, ret_norm_w, cmp_pe_k, cmp_k_w1, cmp_k_b1, cmp_k_w2, cmp_pe_v, cmp_v_w1, cmp_v_b1, cmp_v_w2, w_out, ffn2_norm_w, ffn2_w_gate, ffn2_w_up, ffn2_w_down, final_norm_w):
    bsz, seq, d = x.shape
    assert d == D_MODEL and seq % TM == 0 and seq % PROJ_TM == 0 and seq % TQ == 0 and seq >= WINDOW
    assert ffn1_norm_w.shape[0] == 1, "the final norm is fused into the (single) layer's second FFN"
    t = bsz * seq
    h = x.reshape(t, d)
    n_layers = ffn1_norm_w.shape[0]
    ret_tabs = _retention_tables()
    ret_rope = _rope_tables(seq, RET_DK, RET_THETA, RET_DK)
    nsa_rope = _rope_tables(seq, ROPE_DIM, ROPE_THETA, NSA_DH)
    mcs_t = _cmp_to_sel_t(seq)
    onehot = _block_onehot(seq)
    for layer in range(n_layers):
        h1 = _ffn1(h, ffn1_norm_w[layer][None, :], ffn1_w_gate[layer], ffn1_w_up[layer], ffn1_w_down[layer])
        (rq, rk, rv, rg, qr_t, qp_t, kc, vc, ks, vs_t, kw, vw_t, ng_t) = _proj(
            h1, mix_norm_w[layer][None, :], _pack_w_in(w_in[layer]), ret_rope + nsa_rope, seq)
        ret = _retention(rq, rk, rv, rg, ret_tabs, ret_norm_w[layer][None, :], bsz, seq)
        kcmp, vcmp_t = _compress(
            kc, vc,
            _pack_cmp(cmp_pe_k[layer], cmp_k_w1[layer], cmp_k_b1[layer], cmp_k_w2[layer]),
            _pack_cmp(cmp_pe_v[layer], cmp_v_w1[layer], cmp_v_b1[layer], cmp_v_w2[layer]),
            bsz, seq)
        nsa = _nsa(qr_t, qp_t, kcmp, vcmp_t, ks, vs_t, kw, vw_t, ng_t, mcs_t, onehot, bsz, seq)
        h = _ffn2(h1, ret, nsa, w_out[layer], ffn2_norm_w[layer][None, :], ffn2_w_gate[layer],
                  ffn2_w_up[layer], ffn2_w_down[layer], final_norm_w[None, :])
    return h.reshape(bsz, seq, d)
```
